```python
import math
import jax
import jax.numpy as jnp
from jax import lax
import numpy as np

D_MODEL = 1024
BATCH = 8
SEQ = 4096
DEPTH = 2

GRID_W = 64
CTX_LEN = 256
DA_HEADS = 6
DA_DK = 64
DA_DV = 2 * DA_DK
FN_GROUPS = 4
FN_CH = 64
A_COLS = DA_HEADS * (4 * DA_DK + DA_DV)
B_COLS = FN_GROUPS * FN_CH
ML_INNER = 2 * D_MODEL
ML_HEADS = 8
ML_DH = ML_INNER // ML_HEADS
ML_BLOCK = 4
ML_CONV_W = 5
ML_CHUNK = 128
D_FF = -(-(8 * D_MODEL) // (3 * 256)) * 256
ALPHA = (2 * DEPTH) ** 0.25
BETA = (8 * DEPTH) ** -0.25
ROPE_BASE = 10000.0
Q_BLOCK = 128
LN_EPS = 1e-5
N_EVEN = (DEPTH + 1) // 2
N_ODD = DEPTH // 2

kernel_name = 'hybrid_diffattn_fnet_mlstm_block'


def _ln(x):
    xf = x.astype(jnp.float32)
    mu = jnp.mean(xf, -1, keepdims=True)
    var = jnp.mean(jnp.square(xf - mu), -1, keepdims=True)
    return (xf - mu) * lax.rsqrt(var + LN_EPS)


def post_norm(x, y, g, b):
    return (_ln(ALPHA * x + y) * g + b).astype(x.dtype)


def modulate(x, shift, scale):
    return (_ln(x) * (1.0 + scale) + shift).astype(x.dtype)


def swiglu(u, w1, w3, w2):
    return (jax.nn.silu(u @ w1) * (u @ w3)) @ w2


def axial_rope(n_tok, head_dim):
    rows = n_tok // GRID_W
    row = jnp.repeat(jnp.arange(rows, dtype=jnp.float32), GRID_W)
    col = jnp.tile(jnp.arange(GRID_W, dtype=jnp.float32), rows)
    n_freq = head_dim // 4
    inv_freq = ROPE_BASE ** (-jnp.arange(n_freq, dtype=jnp.float32) / n_freq)
    ang = jnp.concatenate([row[:, None] * inv_freq, col[:, None] * inv_freq], -1)
    return jnp.cos(ang), jnp.sin(ang)


def apply_rope(x, cos, sin):
    half = x.shape[-1] // 2
    x1, x2 = x[..., :half], x[..., half:]
    cs = cos[None, :, None, None, :].astype(x.dtype)
    sn = sin[None, :, None, None, :].astype(x.dtype)
    return jnp.concatenate([x1 * cs - x2 * sn, x1 * sn + x2 * cs], -1)


def split_even(u, w_in):
    p = u @ w_in
    b, n = p.shape[:2]
    pa = p[..., :A_COLS].reshape(b, n, DA_HEADS, 4 * DA_DK + DA_DV)
    q = pa[..., :2 * DA_DK].reshape(b, n, DA_HEADS, 2, DA_DK)
    k = pa[..., 2 * DA_DK:4 * DA_DK].reshape(b, n, DA_HEADS, 2, DA_DK)
    v = pa[..., 4 * DA_DK:]
    f = p[..., A_COLS:].reshape(b, n, FN_GROUPS, FN_CH)
    return q, k, v, f


def diff_softmax_mix(q, keys, vals, lam):
    s = jnp.einsum('bqhmd,bkhmd->bhmqk', q, keys).astype(jnp.float32) * (DA_DK ** -0.5)
    p = jax.nn.softmax(s, axis=-1)
    a = p[:, :, 0] - lam * p[:, :, 1]
    return jnp.einsum('bhqk,bkhd->bqhd', a.astype(vals.dtype), vals)


def diff_attn_blocks(q, keys, vals, lam):
    b, n = q.shape[:2]
    nb = n // Q_BLOCK
    qb = jnp.moveaxis(q.reshape(b, nb, Q_BLOCK, DA_HEADS, 2, DA_DK), 1, 0)
    out = lax.map(lambda qq: diff_softmax_mix(qq, keys, vals, lam), qb)
    return jnp.moveaxis(out, 0, 1).reshape(b, n, DA_HEADS, DA_DV)


def fourier_mix(f):
    ff = jnp.fft.fft2(jnp.swapaxes(f, 1, 2).astype(jnp.float32), norm='ortho').real
    return jnp.swapaxes(ff, 1, 2).astype(f.dtype)


def merge_even(a, f, head_g, lam_init):
    b, n = a.shape[:2]
    af = a.astype(jnp.float32)
    a_n = af * lax.rsqrt(jnp.mean(jnp.square(af), -1, keepdims=True) + LN_EPS) * head_g * (1.0 - lam_init)
    return jnp.concatenate([a_n.astype(f.dtype).reshape(b, n, DA_HEADS * DA_DV), f.reshape(b, n, B_COLS)], -1)


def even_mixer(u, uc, cos, sin, w_in, w_out, lq1, lk1, lq2, lk2, head_g, lam_init, need_ctx):
    q, k, v, f = split_even(u, w_in)
    qc, kc, vc, fc = split_even(uc, w_in)
    lam = (jnp.exp(jnp.sum(lq1 * lk1).astype(jnp.float32))
           - jnp.exp(jnp.sum(lq2 * lk2).astype(jnp.float32)) + lam_init)
    keys = jnp.concatenate([apply_rope(k, cos, sin), kc], axis=1)
    vals = jnp.concatenate([v, vc], axis=1)
    a = diff_attn_blocks(apply_rope(q, cos, sin), keys, vals, lam)
    y = merge_even(a, fourier_mix(f), head_g, lam_init) @ w_out
    yc = None
    if need_ctx:
        ac = diff_softmax_mix(qc, kc, vc, lam)
        yc = merge_even(ac, fourier_mix(fc), head_g, lam_init) @ w_out
    return y, yc


def centred_conv(x, w, b):
    y = lax.conv_general_dilated(x, w[:, None, :], window_strides=(1,),
                                 padding=[(ML_CONV_W // 2, ML_CONV_W // 2)],
                                 dimension_numbers=('NWC', 'WIO', 'NWC'),
                                 feature_group_count=x.shape[-1])
    return y + b


def block_diag(x, w):
    b, n, _ = x.shape
    return jnp.einsum('bngi,gij->bngj', x.reshape(b, n, -1, ML_BLOCK), w).reshape(b, n, ML_INNER)


def to_heads(x):
    b, n, _ = x.shape
    return x.reshape(b, n, ML_HEADS, ML_DH).transpose(0, 2, 1, 3)


def mlstm_features(u, w_in, conv_w, conv_b, wq, wk, wv, w_ig, b_ig, w_fg, b_fg):
    p = u @ w_in
    xm, z = p[..., :ML_INNER], p[..., ML_INNER:]
    xc = jax.nn.silu(centred_conv(xm, conv_w, conv_b))
    q, k, v = block_diag(xc, wq), block_diag(xc, wk), block_diag(xm, wv)
    g_in = jnp.concatenate([q, k, v], -1)
    log_i = (jnp.einsum('bnc,dch->dbhn', g_in, w_ig) + b_ig[:, None, :, None]).astype(jnp.float32)
    log_f = jax.nn.log_sigmoid(
        (jnp.einsum('bnc,dch->dbhn', g_in, w_fg) + b_fg[:, None, :, None]).astype(jnp.float32))
    return to_heads(q), to_heads(k) * (ML_DH ** -0.5), to_heads(v), log_i, log_f, xc, z


def zero_state(b):
    return (jnp.zeros((b, ML_HEADS, ML_DH, ML_DH), jnp.float32),
            jnp.zeros((b, ML_HEADS, ML_DH), jnp.float32),
            jnp.zeros((b, ML_HEADS), jnp.float32))


def mlstm_scan(q, k, v, log_i, log_f, state, with_output):
    b, h, n, dh = q.shape
    nc = n // ML_CHUNK

    def chunks(a):
        return jnp.moveaxis(a.reshape(b, h, nc, ML_CHUNK, *a.shape[3:]), 2, 0).astype(jnp.float32)

    xs = (chunks(q), chunks(k), chunks(v), chunks(log_i), chunks(log_f))
    ordered = jnp.tril(jnp.ones((ML_CHUNK, ML_CHUNK), dtype=bool))

    def step(carry, inp):
        C, nv, m = carry
        qc, kc, vc, li, lf = inp
        bcum = jnp.cumsum(lf, axis=-1)
        b_end = bcum[..., -1]
        hc = None
        if with_output:
            dmat = jnp.where(ordered, bcum[..., :, None] - bcum[..., None, :] + li[..., None, :], -jnp.inf)
            inter = bcum + m[..., None]
            m_t = jnp.maximum(jnp.max(dmat, -1), inter)
            dw = jnp.exp(dmat - m_t[..., None])
            iw = jnp.exp(inter - m_t)
            s = jnp.einsum('bhld,bhsd->bhls', qc, kc) * dw
            num = iw[..., None] * jnp.einsum('bhld,bhde->bhle', qc, C) + jnp.einsum('bhls,bhse->bhle', s, vc)
            den = iw * jnp.einsum('bhld,bhd->bhl', qc, nv) + jnp.sum(s, -1)
            hc = num / jnp.maximum(jnp.abs(den), jnp.exp(-m_t))[..., None]
        w_end = b_end[..., None] - bcum + li
        m_new = jnp.maximum(b_end + m, jnp.max(w_end, -1))
        keep = jnp.exp(b_end + m - m_new)
        w = jnp.exp(w_end - m_new[..., None])
        C_new = keep[..., None, None] * C + jnp.einsum('bhl,bhld,bhle->bhde', w, kc, vc)
        n_new = keep[..., None] * nv + jnp.einsum('bhl,bhld->bhd', w, kc)
        return (C_new, n_new, m_new), hc

    state, hs = lax.scan(step, state, xs)
    if with_output:
        hs = jnp.moveaxis(hs, 0, 2).reshape(b, h, n, dh)
    return hs, state


def flip_t(a, rev):
    return jnp.flip(a, axis=2) if rev else a


def mlstm_out(h, xc, z, skip, head_g, w_out):
    b, _, n, _ = h.shape
    hn = _ln(h).transpose(0, 2, 1, 3).reshape(b, n, ML_INNER) * head_g
    y = (hn + skip * xc) * jax.nn.silu(z)
    return y.astype(xc.dtype) @ w_out


def odd_mixer(u, uc, w_in, w_out, conv_w, conv_b, wq, wk, wv, w_ig, b_ig, w_fg, b_fg, skip, head_g, need_ctx):
    q, k, v, li, lf, xc, z = mlstm_features(u, w_in, conv_w, conv_b, wq, wk, wv, w_ig, b_ig, w_fg, b_fg)
    qc, kc, vc, lic, lfc, xcc, zc = mlstm_features(uc, w_in, conv_w, conv_b, wq, wk, wv, w_ig, b_ig, w_fg, b_fg)
    h_lat = 0.0
    h_ctx = 0.0
    for d in range(2):
        rev = d == 1
        hcd, st = mlstm_scan(flip_t(qc, rev), flip_t(kc, rev), flip_t(vc, rev),
                             flip_t(lic[d], rev), flip_t(lfc[d], rev), zero_state(uc.shape[0]), need_ctx)
        hld, _ = mlstm_scan(flip_t(q, rev), flip_t(k, rev), flip_t(v, rev),
                            flip_t(li[d], rev), flip_t(lf[d], rev), st, True)
        h_lat = h_lat + flip_t(hld, rev)
        if need_ctx:
            h_ctx = h_ctx + flip_t(hcd, rev)
    y = mlstm_out(h_lat, xc, z, skip, head_g, w_out)
    yc = mlstm_out(h_ctx, xcc, zc, skip, head_g, w_out) if need_ctx else None
    return y, yc


def setup_inputs(seed: int = 0) -> dict:
    key = jax.random.key(seed)
    ks = jax.random.split(key, 32)
    d = D_MODEL

    def nrm(k, shape, s):
        return s * jax.random.normal(k, shape, jnp.float32)

    return {
        'x': nrm(ks[0], (BATCH, SEQ, d), 1.0),
        'c': nrm(ks[1], (BATCH, d), 1.0),
        'ctx': nrm(ks[2], (BATCH, CTX_LEN, d), 1.0),
        'c_ctx': nrm(ks[3], (d,), 1.0),
        'w_mod': nrm(ks[4], (DEPTH, d, 6 * d), 0.5 * d ** -0.5),
        'b_mod': nrm(ks[5], (DEPTH, 6 * d), 0.01),
        'ln_g': 1.0 + nrm(ks[6], (DEPTH, 2, d), 0.01),
        'ln_b': nrm(ks[7], (DEPTH, 2, d), 0.01),
        'w_ff1': nrm(ks[8], (DEPTH, d, D_FF), d ** -0.5),
        'w_ff3': nrm(ks[9], (DEPTH, d, D_FF), d ** -0.5),
        'w_ff2': nrm(ks[10], (DEPTH, D_FF, d), BETA * D_FF ** -0.5),
        'a_w_in': nrm(ks[11], (N_EVEN, d, A_COLS + B_COLS), d ** -0.5),
        'a_w_out': nrm(ks[12], (N_EVEN, DA_HEADS * DA_DV + B_COLS, d), BETA * (DA_HEADS * DA_DV + B_COLS) ** -0.5),
        'da_lq1': nrm(ks[13], (N_EVEN, DA_DK), 0.1),
        'da_lk1': nrm(ks[14], (N_EVEN, DA_DK), 0.1),
        'da_lq2': nrm(ks[15], (N_EVEN, DA_DK), 0.1),
        'da_lk2': nrm(ks[16], (N_EVEN, DA_DK), 0.1),
        'da_head_g': 1.0 + nrm(ks[17], (N_EVEN, DA_DV), 0.01),
        'm_w_in': nrm(ks[18], (N_ODD, d, 2 * ML_INNER), d ** -0.5),
        'm_w_out': nrm(ks[19], (N_ODD, ML_INNER, d), BETA * ML_INNER ** -0.5),
        'm_conv_w': nrm(ks[20], (N_ODD, ML_CONV_W, ML_INNER), ML_CONV_W ** -0.5),
        'm_conv_b': nrm(ks[21], (N_ODD, ML_INNER), 0.01),
        'm_wq': nrm(ks[22], (N_ODD, ML_INNER // ML_BLOCK, ML_BLOCK, ML_BLOCK), ML_BLOCK ** -0.5),
        'm_wk': nrm(ks[23], (N_ODD, ML_INNER // ML_BLOCK, ML_BLOCK, ML_BLOCK), ML_BLOCK ** -0.5),
        'm_wv': nrm(ks[24], (N_ODD, ML_INNER // ML_BLOCK, ML_BLOCK, ML_BLOCK), ML_BLOCK ** -0.5),
        'm_w_ig': nrm(ks[25], (N_ODD, 2, 3 * ML_INNER, ML_HEADS), 0.1 * (3 * ML_INNER) ** -0.5),
        'm_b_ig': nrm(ks[26], (N_ODD, 2, ML_HEADS), 0.1),
        'm_w_fg': nrm(ks[27], (N_ODD, 2, 3 * ML_INNER, ML_HEADS), 0.1 * (3 * ML_INNER) ** -0.5),
        'm_b_fg': jnp.linspace(3.0, 6.0, ML_HEADS)[None, None, :] + nrm(ks[28], (N_ODD, 2, ML_HEADS), 0.01),
        'm_skip': 1.0 + nrm(ks[29], (N_ODD, ML_INNER), 0.01),
        'm_head_g': 1.0 + nrm(ks[30], (N_ODD, ML_INNER), 0.01),
    }


def reference(x, c, ctx, c_ctx, w_mod, b_mod, ln_g, ln_b, w_ff1, w_ff3, w_ff2,
              a_w_in, a_w_out, da_lq1, da_lk1, da_lq2, da_lk2, da_head_g,
              m_w_in, m_w_out, m_conv_w, m_conv_b, m_wq, m_wk, m_wv,
              m_w_ig, m_b_ig, m_w_fg, m_b_fg, m_skip, m_head_g):
    cos, sin = axial_rope(x.shape[1], DA_DK)
    h, hc = x, ctx
    for l in range(DEPTH):
        need_ctx = l < DEPTH - 1
        i = l // 2
        mod = jnp.split((jax.nn.silu(c) @ w_mod[l] + b_mod[l])[:, None, :], 6, axis=-1)
        mod_c = jnp.split(jax.nn.silu(c_ctx) @ w_mod[l] + b_mod[l], 6, axis=-1)
        u = modulate(h, mod[0], mod[1])
        uc = modulate(hc, mod_c[0], mod_c[1])
        if l % 2 == 0:
            lam_init = 0.8 - 0.6 * math.exp(-0.3 * l)
            y, yc = even_mixer(u, uc, cos, sin, a_w_in[i], a_w_out[i], da_lq1[i], da_lk1[i],
                               da_lq2[i], da_lk2[i], da_head_g[i], lam_init, need_ctx)
        else:
            y, yc = odd_mixer(u, uc, m_w_in[i], m_w_out[i], m_conv_w[i], m_conv_b[i], m_wq[i], m_wk[i], m_wv[i],
                              m_w_ig[i], m_b_ig[i], m_w_fg[i], m_b_fg[i], m_skip[i], m_head_g[i], need_ctx)
        h = post_norm(h, mod[2] * y, ln_g[l, 0], ln_b[l, 0])
        h = post_norm(h, mod[5] * swiglu(modulate(h, mod[3], mod[4]), w_ff1[l], w_ff3[l], w_ff2[l]),
                      ln_g[l, 1], ln_b[l, 1])
        if need_ctx:
            hc = post_norm(hc, mod_c[2] * yc, ln_g[l, 0], ln_b[l, 0])
            hc = post_norm(hc, mod_c[5] * swiglu(modulate(hc, mod_c[3], mod_c[4]), w_ff1[l], w_ff3[l], w_ff2[l]),
                           ln_g[l, 1], ln_b[l, 1])
    return h
```

```python
import functools
import math

import numpy as np
import jax
import jax.numpy as jnp
from jax import lax
from jax.experimental import pallas as pl
from jax.experimental.pallas import tpu as pltpu

F32 = jnp.float32
BF16 = jnp.bfloat16

DA_HEADS = 6
DA_DK = 64
DA_DV = 2 * DA_DK
DA_HEAD_COLS = 4 * DA_DK + DA_DV
FN_GROUPS = 4
FN_CH = 64
FN_COLS = FN_GROUPS * FN_CH
ML_HEADS = 8
ML_BLOCK = 4
ML_CONV_W = 5
GRID_W = 64
ROPE_BASE = 10000.0
LN_EPS = 1e-5
DEPTH = 2
ALPHA = (2 * DEPTH) ** 0.25

LANES = 128
SUBLANES = 8
MXU_DIM = 256
VMEM_LIMIT_BYTES = 60 * 1024 * 1024

TOKEN_TILE = 256
CONV_TILE = 128
MOD_ROWS_PAD = 8


def _params(*sem):
    return pltpu.CompilerParams(dimension_semantics=sem, vmem_limit_bytes=VMEM_LIMIT_BYTES)


def _ln(x):
    mu = jnp.mean(x, axis=-1, keepdims=True)
    xc = x - mu
    var = jnp.mean(xc * xc, axis=-1, keepdims=True)
    return xc * lax.rsqrt(var + LN_EPS)


def _silu(x):
    return x * jax.nn.sigmoid(x)


def _dot(a, b):
    return jnp.dot(a, b, preferred_element_type=F32)


def _dot_nt(a, b):
    return lax.dot_general(a, b, (((1,), (1,)), ((), ())), preferred_element_type=F32)


def _full(shape):
    n = len(shape)
    return pl.BlockSpec(shape, lambda *_: (0,) * n)


def _mod_kernel(c_ref, w_ref, b_ref, o_ref):
    s = _silu(c_ref[...]).astype(BF16)
    o_ref[0] = _dot(s, w_ref[0].astype(BF16)) + b_ref[0]


def _modulation(cvec, w_mod, b_mod):
    depth, d, d6 = w_mod.shape
    r = cvec.shape[0]
    tn = d6 // 4
    return pl.pallas_call(
        _mod_kernel,
        grid=(depth, d6 // tn),
        in_specs=[pl.BlockSpec((r, d), lambda l, j: (0, 0)),
                  pl.BlockSpec((1, d, tn), lambda l, j: (l, 0, j)),
                  pl.BlockSpec((1, 1, tn), lambda l, j: (l, 0, j))],
        out_specs=pl.BlockSpec((1, r, tn), lambda l, j: (l, 0, j)),
        out_shape=jax.ShapeDtypeStruct((depth, r, d6), F32),
        compiler_params=_params("parallel", "parallel"),
        name="modulation",
    )(cvec, w_mod, b_mod.reshape(depth, 1, d6))


def _mod_spec(nb, n_lat_tiles, d):
    return pl.BlockSpec((1, 6, d), lambda b, i: (jnp.where(i < n_lat_tiles, b, nb), 0, 0))


def _proj_even_kernel(h_ref, mod_ref, w_ref, c_ref, s_ref, q_ref, k_ref, v_ref, f_ref):
    x = h_ref[0]
    u = (_ln(x) * (1.0 + mod_ref[0, 1:2, :]) + mod_ref[0, 0:1, :]).astype(BF16)
    p = _dot(u, w_ref[...])
    c = c_ref[...]
    s = s_ref[...]
    blk = DA_HEAD_COLS
    for hd in range(DA_HEADS):
        pq = p[:, hd * blk: hd * blk + LANES]
        pk = p[:, hd * blk + LANES: hd * blk + 2 * LANES]
        q_ref[0, hd] = (pq * c + pltpu.roll(pq, LANES // 2, 1) * s).astype(BF16)
        k_ref[0, hd] = (pk * c + pltpu.roll(pk, LANES // 2, 1) * s).astype(BF16)
        v_ref[0, hd] = p[:, hd * blk + 2 * LANES: (hd + 1) * blk].astype(BF16)
    f_ref[0] = p[:, DA_HEADS * blk:]


def _proj_even(hcat, mod, w, rope_c, rope_s, n_lat):
    nb, ntot, d = hcat.shape
    tm = TOKEN_TILE
    nt = ntot // tm
    ncols = w.shape[1]
    qkv_shape = jax.ShapeDtypeStruct((nb, DA_HEADS, ntot, LANES), BF16)
    qkv_spec = pl.BlockSpec((1, DA_HEADS, tm, LANES), lambda b, i: (b, 0, i, 0))
    return pl.pallas_call(
        _proj_even_kernel,
        grid=(nb, nt),
        in_specs=[pl.BlockSpec((1, tm, d), lambda b, i: (b, i, 0)),
                  _mod_spec(nb, n_lat // tm, d),
                  _full((d, ncols)),
                  pl.BlockSpec((tm, LANES), lambda b, i: (i, 0)),
                  pl.BlockSpec((tm, LANES), lambda b, i: (i, 0))],
        out_specs=[qkv_spec, qkv_spec, qkv_spec,
                   pl.BlockSpec((1, tm, FN_COLS), lambda b, i: (b, i, 0))],
        out_shape=[qkv_shape, qkv_shape, qkv_shape,
                   jax.ShapeDtypeStruct((nb, ntot, FN_COLS), F32)],
        compiler_params=_params("parallel", "parallel"),
        name="proj_even",
    )(hcat, mod, w, rope_c, rope_s)


def _attn_kernel(lp_ref, g_ref, q_ref, k_ref, v_ref, o_ref, m_sc, l_sc, acc_sc,
                 *, tq, tk, n_lat_tiles, n_lat_kt, n_all_kt, lam_init):
    i = pl.program_id(2)
    lp = lp_ref[...]
    lam = (jnp.exp(jnp.sum(lp[0:1] * lp[1:2], keepdims=True))
           - jnp.exp(jnp.sum(lp[2:3] * lp[3:4], keepdims=True)) + lam_init)
    qf = q_ref[0, 0].astype(F32)
    lane = lax.broadcasted_iota(jnp.int32, (1, LANES), 1)
    map1 = (lane % (LANES // 2)) < (LANES // 4)
    qq = jnp.concatenate([jnp.where(map1, qf, 0.0), jnp.where(map1, 0.0, qf)], axis=0).astype(BF16)
    m_sc[...] = jnp.full(m_sc.shape, -jnp.inf, F32)
    l_sc[...] = jnp.zeros(l_sc.shape, F32)
    acc_sc[...] = jnp.zeros(acc_sc.shape, F32)
    is_lat = i < n_lat_tiles
    j0 = jnp.where(is_lat, 0, n_lat_kt)

    def body(j, carry):
        off = pl.multiple_of(j * tk, tk)
        kk = k_ref[0, 0, pl.ds(off, tk), :]
        vv = v_ref[0, 0, pl.ds(off, tk), :]
        s = _dot_nt(qq, kk)
        m_prev = m_sc[...]
        m_new = jnp.maximum(m_prev, jnp.max(s, axis=1, keepdims=True))
        alpha = jnp.exp(m_prev - m_new)
        p = jnp.exp(s - m_new)
        l_sc[...] = alpha * l_sc[...] + jnp.sum(p, axis=1, keepdims=True)
        acc_sc[...] = alpha * acc_sc[...] + _dot(p.astype(BF16), vv)
        m_sc[...] = m_new
        return carry

    lax.fori_loop(j0, n_all_kt, body, 0)
    o = acc_sc[...] / l_sc[...]
    a = o[:tq] - lam * o[tq:]
    a_n = a * lax.rsqrt(jnp.mean(a * a, axis=-1, keepdims=True) + LN_EPS) * (g_ref[...] * (1.0 - lam_init))
    o_ref[0] = a_n.astype(BF16)


def _attention(q, k, v, lam_params, head_g, n_lat, lam_init):
    nb, nh, ntot, _ = q.shape
    tq = TOKEN_TILE
    tk = TOKEN_TILE
    kern = functools.partial(_attn_kernel, tq=tq, tk=tk, n_lat_tiles=n_lat // tq,
                             n_lat_kt=n_lat // tk, n_all_kt=ntot // tk, lam_init=lam_init)
    kv_spec = pl.BlockSpec((1, 1, ntot, LANES), lambda b, h, i: (b, h, 0, 0))
    return pl.pallas_call(
        kern,
        grid=(nb, nh, ntot // tq),
        in_specs=[_full(lam_params.shape), _full(head_g.shape),
                  pl.BlockSpec((1, 1, tq, LANES), lambda b, h, i: (b, h, i, 0)),
                  kv_spec, kv_spec],
        out_specs=pl.BlockSpec((1, tq, LANES), lambda b, h, i: (b, i, h)),
        out_shape=jax.ShapeDtypeStruct((nb, ntot, nh * LANES), BF16),
        scratch_shapes=[pltpu.VMEM((2 * tq, 1), F32), pltpu.VMEM((2 * tq, 1), F32),
                        pltpu.VMEM((2 * tq, LANES), F32)],
        compiler_params=_params("parallel", "parallel", "arbitrary"),
        name="diff_attention",
    )(lam_params, head_g, q, k, v)


def _fourier_factor(n):
    a = 1 << (int(math.log2(n)) // 2)
    assert a * (n // a) == n and a % SUBLANES == 0 and (n // a) % SUBLANES == 0
    return a, n // a


def _fourier_tables(n):
    a, bn = _fourier_factor(n)
    j = np.arange(FN_CH)
    ang = 2.0 * np.pi * np.outer(j, j) / FN_CH
    eye = np.eye(FN_GROUPS)
    wc = np.concatenate([np.kron(eye, np.cos(ang)), -np.kron(eye, np.sin(ang))], axis=1)
    ia = np.arange(a)
    ang_a = 2.0 * np.pi * np.outer(ia, ia) / a
    ca, sa = np.cos(ang_a), np.sin(ang_a)
    m1 = np.block([[ca, sa], [-sa, ca]])
    ib = np.arange(bn)
    ang_t = 2.0 * np.pi * np.outer(ia, ib) / n
    ang_b = 2.0 * np.pi * np.outer(ib, ib) / bn
    m3 = np.concatenate([np.cos(ang_b), np.sin(ang_b)], axis=1) / math.sqrt(n * FN_CH)
    tw_c = jnp.repeat(jnp.asarray(np.cos(ang_t), F32), FN_COLS, axis=1)
    tw_s = jnp.repeat(jnp.asarray(np.sin(ang_t), F32), FN_COLS, axis=1)
    return (jnp.asarray(wc, BF16), jnp.asarray(m1, BF16), tw_c, tw_s, jnp.asarray(m3, BF16))


def _fourier_chan_kernel(x_ref, w_ref, o_ref):
    g = _dot(x_ref[0].astype(BF16), w_ref[...])
    o_ref[0, 0] = g[:, :FN_COLS]
    o_ref[0, 1] = g[:, FN_COLS:]


def _fourier_stage1_kernel(g_ref, m_ref, c_ref, s_ref, o_ref, *, a, bb):
    t1 = _dot(m_ref[...], g_ref[0].astype(BF16))
    tr, ti = t1[:a], t1[a:]
    c = c_ref[...]
    s = s_ref[...]
    t2r = tr * c + ti * s
    t2i = ti * c - tr * s
    for b in range(bb):
        o_ref[0, 0, b] = t2r[:, b * FN_COLS:(b + 1) * FN_COLS]
        o_ref[0, 1, b] = t2i[:, b * FN_COLS:(b + 1) * FN_COLS]


def _fourier_stage2_kernel(t_ref, m_ref, *rest):
    o_ref = rest[-1]
    o_ref[0] = _dot(m_ref[...], t_ref[0].astype(BF16))


def _fourier_mix(fcat, row0, n, out_buf):
    nb, ntot, _ = fcat.shape
    a, bn = _fourier_factor(n)
    wc, m1, tw_c, tw_s, m3 = _fourier_tables(n)
    tm = min(TOKEN_TILE, n)
    blk0 = row0 // tm
    g = pl.pallas_call(
        _fourier_chan_kernel,
        grid=(nb, n // tm),
        in_specs=[pl.BlockSpec((1, tm, FN_COLS), lambda b, i: (b, blk0 + i, 0)),
                  _full(wc.shape)],
        out_specs=pl.BlockSpec((1, 2, tm, FN_COLS), lambda b, i: (b, 0, i, 0)),
        out_shape=jax.ShapeDtypeStruct((nb, 2, n, FN_COLS), F32),
        compiler_params=_params("parallel", "parallel"),
        name="fourier_chan",
    )(fcat, wc)
    bb = min(8, bn)
    tc = bb * FN_COLS
    t2 = pl.pallas_call(
        functools.partial(_fourier_stage1_kernel, a=a, bb=bb),
        grid=(bn // bb, nb),
        in_specs=[pl.BlockSpec((1, 2 * a, tc), lambda j, b: (b, 0, j)),
                  _full(m1.shape),
                  pl.BlockSpec((a, tc), lambda j, b: (0, j)),
                  pl.BlockSpec((a, tc), lambda j, b: (0, j))],
        out_specs=pl.BlockSpec((1, 2, bb, a, FN_COLS), lambda j, b: (b, 0, j, 0, 0)),
        out_shape=jax.ShapeDtypeStruct((nb, 2, bn, a, FN_COLS), F32),
        compiler_params=_params("parallel", "parallel"),
        name="fourier_stage1",
    )(g.reshape(nb, 2 * a, bn * FN_COLS), m1, tw_c, tw_s)
    assert ntot % a == 0 and (row0 // a) % bn == 0
    tc2 = min(8, a) * FN_COLS
    out_rows = ntot // a
    in_specs = [pl.BlockSpec((1, 2 * bn, tc2), lambda b, j: (b, 0, j)), _full(m3.shape)]
    args = [t2.reshape(nb, 2 * bn, a * FN_COLS), m3]
    aliases = {}
    if out_buf is not None:
        in_specs.append(pl.BlockSpec(memory_space=pl.ANY))
        args.append(out_buf.reshape(nb, out_rows, a * FN_COLS))
        aliases = {2: 0}
    rblk = (row0 // a) // bn
    out = pl.pallas_call(
        _fourier_stage2_kernel,
        grid=(nb, a * FN_COLS // tc2),
        in_specs=in_specs,
        out_specs=pl.BlockSpec((1, bn, tc2), lambda b, j: (b, rblk, j)),
        out_shape=jax.ShapeDtypeStruct((nb, out_rows, a * FN_COLS), F32),
        input_output_aliases=aliases,
        compiler_params=_params("parallel", "parallel"),
        name="fourier_stage2",
    )(*args)
    return out.reshape(nb, ntot, FN_COLS)


def _post_norm(h, y, g, b):
    return _ln(ALPHA * h + y) * g + b


def _out_even_kernel(a_ref, f_ref, h_ref, mod_ref, w_ref, g_ref, b_ref, o_ref):
    na = a_ref.shape[-1]
    y = _dot(a_ref[0], w_ref[:na, :]) + _dot(f_ref[0].astype(BF16), w_ref[na:, :])
    o_ref[0] = _post_norm(h_ref[0], mod_ref[0, 2:3, :] * y, g_ref[...], b_ref[...])


def _out_even(a_n, fm, hcat, mod, w_out, ln_g, ln_b, n_lat):
    nb, ntot, d = hcat.shape
    tm = TOKEN_TILE
    na = a_n.shape[-1]
    tok = lambda c: pl.BlockSpec((1, tm, c), lambda b, i: (b, i, 0))
    return pl.pallas_call(
        _out_even_kernel,
        grid=(nb, ntot // tm),
        in_specs=[tok(na), tok(FN_COLS), tok(d), _mod_spec(nb, n_lat // tm, d),
                  _full(w_out.shape), _full((1, d)), _full((1, d))],
        out_specs=tok(d),
        out_shape=jax.ShapeDtypeStruct((nb, ntot, d), F32),
        compiler_params=_params("parallel", "parallel"),
        name="out_even",
    )(a_n, fm, hcat, mod, w_out, ln_g, ln_b)


def _ffn_kernel(h_ref, mod_ref, w1_ref, w3_ref, w2_ref, g_ref, b_ref, o_ref):
    h = h_ref[0]
    u = (_ln(h) * (1.0 + mod_ref[0, 4:5, :]) + mod_ref[0, 3:4, :]).astype(BF16)
    hid = (_silu(_dot(u, w1_ref[...])) * _dot(u, w3_ref[...])).astype(BF16)
    y = _dot(hid, w2_ref[...])
    o_ref[0] = _post_norm(h, mod_ref[0, 5:6, :] * y, g_ref[...], b_ref[...])


def _ffn(h, n_rows, mod, w1, w3, w2, ln_g, ln_b, n_lat):
    nb, _, d = h.shape
    tm = TOKEN_TILE
    tok = pl.BlockSpec((1, tm, d), lambda b, i: (b, i, 0))
    resident = lambda s: pl.BlockSpec(s, lambda b, i: (0, 0), pipeline_mode=pl.Buffered(1))
    return pl.pallas_call(
        _ffn_kernel,
        grid=(nb, n_rows // tm),
        in_specs=[tok, _mod_spec(nb, n_lat // tm, d),
                  resident(w1.shape), resident(w3.shape), resident(w2.shape),
                  _full((1, d)), _full((1, d))],
        out_specs=tok,
        out_shape=jax.ShapeDtypeStruct((nb, n_rows, d), F32),
        compiler_params=_params("parallel", "parallel"),
        name="ffn",
    )(h, mod, w1, w3, w2, ln_g, ln_b)


def _proj_odd_kernel(h_ref, mod_ref, w_ref, xm_ref, z_ref):
    x = h_ref[0]
    u = (_ln(x) * (1.0 + mod_ref[0, 1:2, :]) + mod_ref[0, 0:1, :]).astype(BF16)
    p = _dot(u, w_ref[...])
    inner = xm_ref.shape[-1]
    xm_ref[0] = p[:, :inner]
    z_ref[0] = p[:, inner:]


def _proj_odd(hcat, mod, w, n_lat):
    nb, ntot, d = hcat.shape
    tm = TOKEN_TILE
    inner = w.shape[1] // 2
    out_spec = pl.BlockSpec((1, tm, inner), lambda b, i: (b, i, 0))
    out_shape = jax.ShapeDtypeStruct((nb, ntot, inner), F32)
    return pl.pallas_call(
        _proj_odd_kernel,
        grid=(nb, ntot // tm),
        in_specs=[pl.BlockSpec((1, tm, d), lambda b, i: (b, i, 0)),
                  _mod_spec(nb, n_lat // tm, d),
                  _full(w.shape)],
        out_specs=[out_spec, out_spec],
        out_shape=[out_shape, out_shape],
        compiler_params=_params("parallel", "parallel"),
        name="proj_odd",
    )(hcat, mod, w)


def _mfeat_kernel(xm_ref, cw_ref, cb_ref, wq_ref, wk_ref, wkt_ref, wv_ref, wg_ref,
                  xc_ref, q_ref, k_ref, v_ref, kt_ref, gr_ref, *, n_lat, chunk, row_group):
    ntot = xm_ref.shape[1]
    ct = CONV_TILE
    half = ML_CONV_W // 2
    cw = cw_ref[...]
    cb = cb_ref[...]

    def conv_tile(t, carry):
        r0 = pl.multiple_of(t * ct, ct)
        at_start = jnp.logical_or(r0 == 0, r0 == n_lat)
        at_end = jnp.logical_or(r0 + ct == n_lat, r0 + ct == ntot)
        p0 = pl.multiple_of(jnp.maximum(r0 - SUBLANES, 0), SUBLANES)
        n0 = pl.multiple_of(jnp.minimum(r0 + ct, ntot - SUBLANES), SUBLANES)
        prev = xm_ref[0, pl.ds(p0, SUBLANES), :]
        nxt = xm_ref[0, pl.ds(n0, SUBLANES), :]
        prev = jnp.where(at_start, 0.0, prev)
        nxt = jnp.where(at_end, 0.0, nxt)
        xe = jnp.concatenate([prev, xm_ref[0, pl.ds(r0, ct), :], nxt], axis=0)
        acc = cb
        for dd in range(ML_CONV_W):
            lo = SUBLANES - half + dd
            acc = acc + cw[dd:dd + 1, :] * xe[lo:lo + ct, :]
        xc_ref[0, pl.ds(r0, ct), :] = _silu(acc)
        return carry

    lax.fori_loop(0, ntot // ct, conv_tile, 0)

    for g in range(ntot // row_group):
        rows = pl.ds(g * row_group, row_group)
        xcb = xc_ref[0, rows, :].astype(BF16)
        xmb = xm_ref[0, rows, :].astype(BF16)
        qb = _dot(xcb, wq_ref[0]).astype(BF16)
        kb = _dot(xcb, wk_ref[0]).astype(BF16)
        vb = _dot(xmb, wv_ref[0]).astype(BF16)
        q_ref[0, rows, :] = qb
        k_ref[0, rows, :] = kb
        v_ref[0, rows, :] = vb
        gr_ref[0, 0, :, rows] = _dot_nt(wg_ref[0], jnp.concatenate([qb, kb, vb], axis=1))

    def kt_chunk(c, carry):
        r0 = pl.multiple_of(c * chunk, chunk)
        kt_ref[0, 0, c] = _dot_nt(wkt_ref[0], xc_ref[0, pl.ds(r0, chunk), :].astype(BF16)).astype(BF16)
        return carry

    lax.fori_loop(0, ntot // chunk, kt_chunk, 0)


def _mfeat(xm, conv_w, conv_b, wq, wk, wkt, wv, wg, n_lat, chunk):
    nb, ntot, inner = xm.shape
    nh = ML_HEADS
    dh = inner // nh
    ng = 2 if ntot % (2 * LANES) == 0 else 1
    kern = functools.partial(_mfeat_kernel, n_lat=n_lat, chunk=chunk, row_group=ntot // ng)
    seq = pl.BlockSpec((1, ntot, dh), lambda b, h: (b, 0, h))
    wblk = pl.BlockSpec((1, dh, dh), lambda b, h: (h, 0, 0))
    n_gates = wg.shape[1]
    seq_bf = jax.ShapeDtypeStruct((nb, ntot, inner), BF16)
    return pl.pallas_call(
        kern,
        grid=(nb, nh),
        in_specs=[seq,
                  pl.BlockSpec((ML_CONV_W, dh), lambda b, h: (0, h)),
                  pl.BlockSpec((1, dh), lambda b, h: (0, h)),
                  wblk, wblk, wblk, wblk,
                  pl.BlockSpec((1, n_gates, 3 * dh), lambda b, h: (h, 0, 0))],
        out_specs=[seq, seq, seq, seq,
                   pl.BlockSpec((1, 1, ntot // chunk, dh, chunk), lambda b, h: (b, h, 0, 0, 0)),
                   pl.BlockSpec((1, 1, n_gates, ntot), lambda b, h: (b, h, 0, 0))],
        out_shape=[jax.ShapeDtypeStruct((nb, ntot, inner), F32), seq_bf, seq_bf, seq_bf,
                   jax.ShapeDtypeStruct((nb, nh, ntot // chunk, dh, chunk), BF16),
                   jax.ShapeDtypeStruct((nb, nh, n_gates, ntot), F32)],
        compiler_params=_params("parallel", "parallel"),
        name="mlstm_features",
    )(xm, conv_w, conv_b, wq, wk, wkt, wv, wg)


def _split3(x):
    hi = x.astype(BF16)
    r1 = x - hi.astype(F32)
    mid = r1.astype(BF16)
    lo = (r1 - mid.astype(F32)).astype(BF16)
    return hi, mid, lo


def _gates_kernel(g_ref, b_ref, rows_ref, cols_ref):
    nh = ML_HEADS
    chunk = g_ref.shape[-1]
    pre = jnp.sum(g_ref[0], axis=0) + b_ref[...]
    log_i = pre[:2 * nh]
    xf = pre[2 * nh:]
    log_f = jnp.minimum(xf, 0.0) - jnp.log1p(jnp.exp(-jnp.abs(xf)))
    s_idx = lax.broadcasted_iota(jnp.int32, (chunk, chunk), 0)
    t_idx = lax.broadcasted_iota(jnp.int32, (chunk, chunk), 1)
    tri_f = jnp.where(s_idx <= t_idx, 1.0, 0.0).astype(BF16)
    tri_b = jnp.where(s_idx >= t_idx, 1.0, 0.0).astype(BF16)
    cum_f = sum(_dot(p, tri_f) for p in _split3(log_f[:nh]))
    cum_b = sum(_dot(p, tri_b) for p in _split3(log_f[nh:]))
    rows = jnp.concatenate([log_i, cum_f, cum_b], axis=0)
    rows_ref[0, 0] = rows
    pad = jnp.zeros((LANES - 4 * nh, chunk), F32)
    cols_ref[0] = jnp.transpose(jnp.concatenate([rows, pad], axis=0))[:, :4 * nh]


def _gates(g_part, bias, chunk):
    nb, nh, n_gates, ntot = g_part.shape
    nct = ntot // chunk
    return pl.pallas_call(
        _gates_kernel,
        grid=(nb, nct),
        in_specs=[pl.BlockSpec((1, nh, n_gates, chunk), lambda b, c: (b, 0, 0, c)),
                  _full(bias.shape)],
        out_specs=[pl.BlockSpec((1, 1, n_gates, chunk), lambda b, c: (b, c, 0, 0)),
                   pl.BlockSpec((1, chunk, n_gates), lambda b, c: (b, c, 0))],
        out_shape=[jax.ShapeDtypeStruct((nb, nct, n_gates, chunk), F32),
                   jax.ShapeDtypeStruct((nb, ntot, n_gates), F32)],
        compiler_params=_params("parallel", "parallel"),
        name="mlstm_gates",
    )(g_part, bias)


def _scan_kernel(q_ref, k_ref, v_ref, kt_ref, rows_ref, cols_ref, hs_ref, c_sc, n_sc, m_sc,
                 *, n_lat_chunks, n_ctx_chunks, chunk):
    nh = ML_HEADS
    h = pl.program_id(1)
    c_sc[...] = jnp.zeros(c_sc.shape, F32)
    n_sc[...] = jnp.zeros(n_sc.shape, F32)
    m_sc[...] = jnp.zeros(m_sc.shape, F32)
    t_idx = lax.broadcasted_iota(jnp.int32, (chunk, chunk), 0)
    s_idx = lax.broadcasted_iota(jnp.int32, (chunk, chunk), 1)
    gate_lane = lax.broadcasted_iota(jnp.int32, (1, 4 * nh), 1)

    def step(d, cidx, with_out):
        off = pl.multiple_of(cidx * chunk, chunk)
        rows = pl.ds(off, chunk)
        qc = q_ref[0, rows, :]
        kc = k_ref[0, rows, :]
        vc = v_ref[0, rows, :]
        ktc = kt_ref[0, 0, cidx]
        li_r = rows_ref[0, cidx, pl.ds(d * nh + h, 1), :]
        bc_r = rows_ref[0, cidx, pl.ds((2 + d) * nh + h, 1), :]
        ctile = cols_ref[0, rows, :]
        li_c = jnp.sum(jnp.where(gate_lane == d * nh + h, ctile, 0.0), axis=1, keepdims=True)
        bc_c = jnp.sum(jnp.where(gate_lane == (2 + d) * nh + h, ctile, 0.0), axis=1, keepdims=True)
        m = m_sc[d]
        cmat = c_sc[d]
        nrow = n_sc[d]
        b_end = bc_r[:, chunk - 1:chunk] if d == 0 else bc_r[:, 0:1]
        if with_out:
            ordered = (s_idx <= t_idx) if d == 0 else (s_idx >= t_idx)
            dmat = jnp.where(ordered, bc_c - bc_r + li_r, -jnp.inf)
            inter = bc_c + m
            m_t = jnp.maximum(jnp.max(dmat, axis=1, keepdims=True), inter)
            dw = jnp.exp(dmat - m_t)
            iw = jnp.exp(inter - m_t)
            s = _dot(qc, ktc) * dw
            num = iw * _dot(qc, cmat.astype(BF16)) + _dot(s.astype(BF16), vc)
            qn = jnp.sum(qc.astype(F32) * nrow, axis=1, keepdims=True)
            den = iw * qn + jnp.sum(s, axis=1, keepdims=True)
            hs_ref[d, 0, rows, :] = num / jnp.maximum(jnp.abs(den), jnp.exp(-m_t))
        w_end_r = b_end - bc_r + li_r
        w_end_c = b_end - bc_c + li_c
        m_new = jnp.maximum(b_end + m, jnp.max(w_end_r, axis=1, keepdims=True))
        keep = jnp.exp(b_end + m - m_new)
        w_r = jnp.exp(w_end_r - m_new)
        w_c = jnp.exp(w_end_c - m_new)
        kw = (ktc.astype(F32) * w_r).astype(BF16)
        c_sc[d] = keep * cmat + _dot(kw, vc)
        n_sc[d] = keep * nrow + jnp.sum(kc.astype(F32) * w_c, axis=0, keepdims=True)
        m_sc[d] = m_new

    def ctx_body(c, carry):
        step(0, n_lat_chunks + c, False)
        step(1, n_lat_chunks + n_ctx_chunks - 1 - c, False)
        return carry

    def lat_body(c, carry):
        step(0, c, True)
        step(1, n_lat_chunks - 1 - c, True)
        return carry

    lax.fori_loop(0, n_ctx_chunks, ctx_body, 0)
    lax.fori_loop(0, n_lat_chunks, lat_body, 0)


def _scan(q, k, v, kt, rows, cols, n_lat, chunk):
    nb, ntot, inner = q.shape
    nh = ML_HEADS
    dh = inner // nh
    nct = ntot // chunk
    n_gates = rows.shape[2]
    kern = functools.partial(_scan_kernel, n_lat_chunks=n_lat // chunk,
                             n_ctx_chunks=(ntot - n_lat) // chunk, chunk=chunk)
    seq = pl.BlockSpec((1, ntot, dh), lambda b, h: (b, 0, h))
    return pl.pallas_call(
        kern,
        grid=(nb, nh),
        in_specs=[seq, seq, seq,
                  pl.BlockSpec((1, 1, nct, dh, chunk), lambda b, h: (b, h, 0, 0, 0)),
                  pl.BlockSpec((1, nct, n_gates, chunk), lambda b, h: (b, 0, 0, 0)),
                  pl.BlockSpec((1, ntot, n_gates), lambda b, h: (b, 0, 0))],
        out_specs=pl.BlockSpec((2, 1, n_lat, dh), lambda b, h: (0, b, 0, h)),
        out_shape=jax.ShapeDtypeStruct((2, nb, n_lat, inner), F32),
        scratch_shapes=[pltpu.VMEM((2, dh, dh), F32), pltpu.VMEM((2, 1, dh), F32),
                        pltpu.VMEM((2, 1, 1), F32)],
        compiler_params=_params("parallel", "parallel"),
        name="mlstm_scan",
    )(q, k, v, kt, rows, cols)


def _out_odd_kernel(hs_ref, xc_ref, z_ref, h_ref, mod_ref, skip_ref, hg_ref, w_ref, g_ref, b_ref, o_ref):
    hsum = hs_ref[0, 0] + hs_ref[1, 0]
    dh = hsum.shape[-1] // ML_HEADS
    hn = jnp.concatenate([_ln(hsum[:, i * dh:(i + 1) * dh]) for i in range(ML_HEADS)], axis=1)
    y = (hn * hg_ref[...] + skip_ref[...] * xc_ref[0]) * _silu(z_ref[0])
    yo = _dot(y.astype(BF16), w_ref[...])
    o_ref[0] = _post_norm(h_ref[0], mod_ref[0, 2:3, :] * yo, g_ref[...], b_ref[...])


def _out_odd(hs, xc, z, hcat, mod, skip, head_g, w_out, ln_g, ln_b, n_lat):
    nb, _, d = hcat.shape
    inner = xc.shape[-1]
    tm = TOKEN_TILE
    tok = lambda c: pl.BlockSpec((1, tm, c), lambda b, i: (b, i, 0))
    return pl.pallas_call(
        _out_odd_kernel,
        grid=(nb, n_lat // tm),
        in_specs=[pl.BlockSpec((2, 1, tm, inner), lambda b, i: (0, b, i, 0)),
                  tok(inner), tok(inner), tok(d),
                  pl.BlockSpec((1, 6, d), lambda b, i: (b, 0, 0)),
                  _full((1, inner)), _full((1, inner)), _full(w_out.shape),
                  _full((1, d)), _full((1, d))],
        out_specs=tok(d),
        out_shape=jax.ShapeDtypeStruct((nb, n_lat, d), F32),
        compiler_params=_params("parallel", "parallel"),
        name="out_odd",
    )(hs, xc, z, hcat, mod, skip, head_g, w_out, ln_g, ln_b)


def _even_in_columns():
    cols, scale = [], []
    quarter = DA_DK // 2
    for hd in range(DA_HEADS):
        base = hd * DA_HEAD_COLS
        for blk in range(2):
            b0 = base + blk * 2 * DA_DK
            for half in range(2):
                for m in range(2):
                    cols += [b0 + m * DA_DK + half * quarter + j for j in range(quarter)]
            scale += [DA_DK ** -0.5 if blk == 0 else 1.0] * (2 * DA_DK)
        cols += list(range(base + 4 * DA_DK, base + DA_HEAD_COLS))
        scale += [1.0] * DA_DV
    cols += list(range(DA_HEADS * DA_HEAD_COLS, DA_HEADS * DA_HEAD_COLS + FN_COLS))
    scale += [1.0] * FN_COLS
    return np.asarray(cols, np.int32), np.asarray(scale, np.float32)


def _rope_tables(n_lat, ntot):
    rows = n_lat // GRID_W
    row = jnp.repeat(jnp.arange(rows, dtype=F32), GRID_W)
    col = jnp.tile(jnp.arange(GRID_W, dtype=F32), rows)
    n_freq = DA_DK // 4
    inv_freq = ROPE_BASE ** (-jnp.arange(n_freq, dtype=F32) / n_freq)
    ang = jnp.concatenate([row[:, None] * inv_freq, col[:, None] * inv_freq], -1)
    cos, sin = jnp.cos(ang), jnp.sin(ang)
    c = jnp.concatenate([cos, cos, cos, cos], axis=1)
    s = jnp.concatenate([-sin, -sin, sin, sin], axis=1)
    n_ctx = ntot - n_lat
    c = jnp.concatenate([c, jnp.ones((n_ctx, LANES), F32)], axis=0)
    s = jnp.concatenate([s, jnp.zeros((n_ctx, LANES), F32)], axis=0)
    return c, s


def _block_diag_heads(w, dh):
    nblk = w.shape[0]
    per_head = dh // ML_BLOCK
    wh = w.reshape(nblk // per_head, per_head, ML_BLOCK, ML_BLOCK)
    eye = jnp.eye(per_head, dtype=w.dtype)
    dense = jnp.einsum('hgij,gk->hgikj', wh, eye)
    return dense.reshape(nblk // per_head, dh, dh)


def kernel(x, c, ctx, c_ctx, w_mod, b_mod, ln_g, ln_b, w_ff1, w_ff3, w_ff2, a_w_in, a_w_out, da_lq1, da_lk1, da_lq2, da_lk2, da_head_g, m_w_in, m_w_out, m_conv_w, m_conv_b, m_wq, m_wk, m_wv, m_w_ig, m_b_ig, m_w_fg, m_b_fg, m_skip, m_head_g):
    nb, n_lat, d = x.shape
    n_ctx = ctx.shape[1]
    ntot = n_lat + n_ctx
    assert w_mod.shape[0] == DEPTH == 2
    assert n_lat % TOKEN_TILE == 0 and n_ctx % TOKEN_TILE == 0 and n_lat % n_ctx == 0 and n_lat % GRID_W == 0

    r = -(-(nb + 1) // MOD_ROWS_PAD) * MOD_ROWS_PAD
    cvec = jnp.concatenate([c, c_ctx[None, :], jnp.zeros((r - nb - 1, d), F32)], axis=0)
    mod_all = _modulation(cvec, w_mod, b_mod).reshape(DEPTH, r, 6, d)

    hcat = jnp.concatenate([x, ctx], axis=1)

    lam_init0 = 0.8 - 0.6 * math.exp(-0.3 * 0)
    cols, colscale = _even_in_columns()
    w_in = (a_w_in[0][:, cols] * colscale[None, :]).astype(BF16)
    rope_c, rope_s = _rope_tables(n_lat, ntot)
    q, k, v, f = _proj_even(hcat, mod_all[0], w_in, rope_c, rope_s, n_lat)
    lam_params = jnp.stack([da_lq1[0], da_lk1[0], da_lq2[0], da_lk2[0]], axis=0)
    a_n = _attention(q, k, v, lam_params, da_head_g[0][None, :], n_lat, lam_init0)
    fm = _fourier_mix(f, 0, n_lat, None)
    fm = _fourier_mix(f, n_lat, n_ctx, fm)
    h1 = _out_even(a_n, fm, hcat, mod_all[0], a_w_out[0].astype(BF16),
                   ln_g[0, 0][None, :], ln_b[0, 0][None, :], n_lat)
    hcat = _ffn(h1, ntot, mod_all[0], w_ff1[0].astype(BF16), w_ff3[0].astype(BF16), w_ff2[0].astype(BF16),
                ln_g[0, 1][None, :], ln_b[0, 1][None, :], n_lat)

    inner = m_w_in.shape[2] // 2
    dh = inner // ML_HEADS
    chunk = MXU_DIM if (n_lat % MXU_DIM == 0 and n_ctx % MXU_DIM == 0) else LANES
    xm, z = _proj_odd(hcat, mod_all[1], m_w_in[0].astype(BF16), n_lat)
    wq = _block_diag_heads(m_wq[0], dh).astype(BF16)
    wk_f = _block_diag_heads(m_wk[0], dh) * (dh ** -0.5)
    wk = wk_f.astype(BF16)
    wkt = jnp.swapaxes(wk_f, 1, 2).astype(BF16)
    wv = _block_diag_heads(m_wv[0], dh).astype(BF16)
    wg_all = jnp.concatenate([m_w_ig[0, 0], m_w_ig[0, 1], m_w_fg[0, 0], m_w_fg[0, 1]], axis=1)
    wg_all = wg_all.reshape(3, ML_HEADS, dh, 4 * ML_HEADS) * jnp.asarray([1.0, dh ** 0.5, 1.0], F32)[:, None, None, None]
    wg = jnp.transpose(wg_all, (1, 3, 0, 2)).reshape(ML_HEADS, 4 * ML_HEADS, 3 * dh).astype(BF16)
    g_bias = jnp.concatenate([m_b_ig[0, 0], m_b_ig[0, 1], m_b_fg[0, 0], m_b_fg[0, 1]])[:, None]
    xc, qm, km, vm, kt, g_part = _mfeat(xm, m_conv_w[0], m_conv_b[0][None, :], wq, wk, wkt, wv, wg, n_lat, chunk)
    rows, colsg = _gates(g_part, g_bias, chunk)
    hs = _scan(qm, km, vm, kt, rows, colsg, n_lat, chunk)
    h1 = _out_odd(hs, xc, z, hcat, mod_all[1], m_skip[0][None, :], m_head_g[0][None, :],
                  m_w_out[0].astype(BF16), ln_g[1, 0][None, :], ln_b[1, 0][None, :], n_lat)
    return _ffn(h1, n_lat, mod_all[1], w_ff1[1].astype(BF16), w_ff3[1].astype(BF16), w_ff2[1].astype(BF16),
                ln_g[1, 1][None, :], ln_b[1, 1][None, :], n_lat)
```

```python
import functools
import math

import numpy as np
import jax
import jax.numpy as jnp
from jax import lax
from jax.experimental import pallas as pl
from jax.experimental.pallas import tpu as pltpu

F32 = jnp.float32
BF16 = jnp.bfloat16

DA_HEADS = 6
DA_DK = 64
DA_DV = 2 * DA_DK
DA_HEAD_COLS = 4 * DA_DK + DA_DV
FN_GROUPS = 4
FN_CH = 64
FN_COLS = FN_GROUPS * FN_CH
ML_HEADS = 8
ML_BLOCK = 4
ML_CONV_W = 5
GRID_W = 64
ROPE_BASE = 10000.0
LN_EPS = 1e-5
DEPTH = 2
ALPHA = (2 * DEPTH) ** 0.25

LANES = 128
SUBLANES = 8
MXU_DIM = 256
VMEM_LIMIT_BYTES = 60 * 1024 * 1024

TOKEN_TILE = 256
CONV_TILE = 128
MOD_ROWS_PAD = 8
SWEEP_UNROLL = 4


def _params(*sem):
    return pltpu.CompilerParams(dimension_semantics=sem, vmem_limit_bytes=VMEM_LIMIT_BYTES)


def _ln(x):
    mu = jnp.mean(x, axis=-1, keepdims=True)
    xc = x - mu
    var = jnp.mean(xc * xc, axis=-1, keepdims=True)
    return xc * lax.rsqrt(var + LN_EPS)


def _silu(x):
    return x * jax.nn.sigmoid(x)


def _dot(a, b):
    return jnp.dot(a, b, preferred_element_type=F32)


def _dot_nt(a, b):
    return lax.dot_general(a, b, (((1,), (1,)), ((), ())), preferred_element_type=F32)


def _full(shape):
    n = len(shape)
    return pl.BlockSpec(shape, lambda *_: (0,) * n)


def _mod_kernel(c_ref, w_ref, b_ref, o_ref):
    s = _silu(c_ref[...]).astype(BF16)
    o_ref[0] = _dot(s, w_ref[0].astype(BF16)) + b_ref[0]


def _modulation(cvec, w_mod, b_mod):
    depth, d, d6 = w_mod.shape
    r = cvec.shape[0]
    tn = d6 // 4
    return pl.pallas_call(
        _mod_kernel,
        grid=(depth, d6 // tn),
        in_specs=[pl.BlockSpec((r, d), lambda l, j: (0, 0)),
                  pl.BlockSpec((1, d, tn), lambda l, j: (l, 0, j)),
                  pl.BlockSpec((1, 1, tn), lambda l, j: (l, 0, j))],
        out_specs=pl.BlockSpec((1, r, tn), lambda l, j: (l, 0, j)),
        out_shape=jax.ShapeDtypeStruct((depth, r, d6), F32),
        compiler_params=_params("parallel", "parallel"),
        name="modulation",
    )(cvec, w_mod, b_mod.reshape(depth, 1, d6))


def _mod_spec(nb, n_lat_tiles, d):
    return pl.BlockSpec((1, 6, d), lambda b, i: (jnp.where(i < n_lat_tiles, b, nb), 0, 0))


def _proj_even_kernel(h_ref, mod_ref, w_ref, c_ref, s_ref, q_ref, k_ref, v_ref, f_ref):
    x = h_ref[0]
    u = (_ln(x) * (1.0 + mod_ref[0, 1:2, :]) + mod_ref[0, 0:1, :]).astype(BF16)
    p = _dot(u, w_ref[...])
    c = c_ref[...]
    s = s_ref[...]
    blk = DA_HEAD_COLS
    for hd in range(DA_HEADS):
        pq = p[:, hd * blk: hd * blk + LANES]
        pk = p[:, hd * blk + LANES: hd * blk + 2 * LANES]
        q_ref[0, hd] = (pq * c + pltpu.roll(pq, LANES // 2, 1) * s).astype(BF16)
        k_ref[0, hd] = (pk * c + pltpu.roll(pk, LANES // 2, 1) * s).astype(BF16)
        v_ref[0, hd] = p[:, hd * blk + 2 * LANES: (hd + 1) * blk].astype(BF16)
    f_ref[0] = p[:, DA_HEADS * blk:]


def _proj_even(hcat, mod, w, rope_c, rope_s, n_lat):
    nb, ntot, d = hcat.shape
    tm = TOKEN_TILE
    nt = ntot // tm
    ncols = w.shape[1]
    qkv_shape = jax.ShapeDtypeStruct((nb, DA_HEADS, ntot, LANES), BF16)
    qkv_spec = pl.BlockSpec((1, DA_HEADS, tm, LANES), lambda b, i: (b, 0, i, 0))
    return pl.pallas_call(
        _proj_even_kernel,
        grid=(nb, nt),
        in_specs=[pl.BlockSpec((1, tm, d), lambda b, i: (b, i, 0)),
                  _mod_spec(nb, n_lat // tm, d),
                  _full((d, ncols)),
                  pl.BlockSpec((tm, LANES), lambda b, i: (i, 0)),
                  pl.BlockSpec((tm, LANES), lambda b, i: (i, 0))],
        out_specs=[qkv_spec, qkv_spec, qkv_spec,
                   pl.BlockSpec((1, tm, FN_COLS), lambda b, i: (b, i, 0))],
        out_shape=[qkv_shape, qkv_shape, qkv_shape,
                   jax.ShapeDtypeStruct((nb, ntot, FN_COLS), F32)],
        compiler_params=_params("parallel", "parallel"),
        name="proj_even",
    )(hcat, mod, w, rope_c, rope_s)


def _stack_maps(q):
    qf = q.astype(F32)
    lane = lax.broadcasted_iota(jnp.int32, (1, LANES), 1)
    map1 = (lane % (LANES // 2)) < (LANES // 4)
    return jnp.concatenate([jnp.where(map1, qf, 0.0), jnp.where(map1, 0.0, qf)], axis=0).astype(BF16)


def _lane_block_max(s):
    blk = s[:, :LANES]
    for c in range(1, s.shape[1] // LANES):
        blk = jnp.maximum(blk, s[:, c * LANES:(c + 1) * LANES])
    return blk


def _exp_blocks(s, mb):
    return [jnp.exp(s[:, c * LANES:(c + 1) * LANES] - mb) for c in range(s.shape[1] // LANES)]


def _diff_merge(o, lp, head_g, lam_init):
    tq = o.shape[0] // 2
    lam = (jnp.exp(jnp.sum(lp[0:1] * lp[1:2], keepdims=True))
           - jnp.exp(jnp.sum(lp[2:3] * lp[3:4], keepdims=True)) + lam_init)
    a = o[:tq] - lam * o[tq:]
    a_n = a * lax.rsqrt(jnp.mean(a * a, axis=-1, keepdims=True) + LN_EPS) * (head_g * (1.0 - lam_init))
    return a_n.astype(BF16)


def _attn_kernel(lp_ref, g_ref, qc_ref, qn_ref, k_ref, v_ref, o_ref, sa_sc, sb_sc, mcur_sc, mnxt_sc, l_sc, acc_sc,
                 *, tk, nkt, lam_init):
    i = pl.program_id(2)
    rows = mcur_sc.shape[0]

    def score_chunk(qq, s_ref, j):
        off = pl.multiple_of(j * tk, tk)
        s = _dot_nt(qq, k_ref[0, 0, pl.ds(off, tk), :])
        s_ref[j] = s
        mnxt_sc[...] = jnp.maximum(mnxt_sc[...], _lane_block_max(s))

    def row_max():
        return jnp.broadcast_to(jnp.max(mnxt_sc[...], axis=1, keepdims=True), (rows, LANES))

    @pl.when(i == 0)
    def _():
        qq0 = _stack_maps(qc_ref[0, 0])
        mnxt_sc[...] = jnp.full(mnxt_sc.shape, -jnp.inf, F32)

        def first(j, carry):
            score_chunk(qq0, sa_sc, j)
            return carry

        lax.fori_loop(0, nkt, first, 0)
        mcur_sc[...] = row_max()

    qqn = _stack_maps(qn_ref[0, 0])
    mnxt_sc[...] = jnp.full(mnxt_sc.shape, -jnp.inf, F32)
    l_sc[...] = jnp.zeros(l_sc.shape, F32)
    acc_sc[...] = jnp.zeros(acc_sc.shape, F32)

    def sweep(cur_ref, nxt_ref):
        def body(j, carry):
            score_chunk(qqn, nxt_ref, j)
            off = pl.multiple_of(j * tk, tk)
            ps = _exp_blocks(cur_ref[j], mcur_sc[...])
            part = ps[0]
            for pc in ps[1:]:
                part = part + pc
            l_sc[...] += part
            acc_sc[...] += _dot(jnp.concatenate(ps, axis=1).astype(BF16), v_ref[0, 0, pl.ds(off, tk), :])
            return carry

        lax.fori_loop(0, nkt, body, 0, unroll=SWEEP_UNROLL)

    @pl.when(i % 2 == 0)
    def _():
        sweep(sa_sc, sb_sc)

    @pl.when(i % 2 == 1)
    def _():
        sweep(sb_sc, sa_sc)

    o = acc_sc[...] / jnp.sum(l_sc[...], axis=1, keepdims=True)
    o_ref[0] = _diff_merge(o, lp_ref[...], g_ref[...], lam_init)
    mcur_sc[...] = row_max()


def _attn_ctx_kernel(lp_ref, g_ref, q_ref, k_ref, v_ref, buf_ref, o_ref, *, lam_init):
    del buf_ref
    s = _dot_nt(_stack_maps(q_ref[0, 0]), k_ref[0, 0])
    p = jnp.exp(s - jnp.max(s, axis=1, keepdims=True))
    o = _dot(p.astype(BF16), v_ref[0, 0]) / jnp.sum(p, axis=1, keepdims=True)
    o_ref[0] = _diff_merge(o, lp_ref[...], g_ref[...], lam_init)


def _attention(q, k, v, lam_params, head_g, n_lat, lam_init):
    nb, nh, ntot, _ = q.shape
    n_ctx = ntot - n_lat
    tq = TOKEN_TILE
    tk = TOKEN_TILE
    nq = n_lat // tq
    nkt = ntot // tk
    rows = 2 * tq
    kv_spec = pl.BlockSpec((1, 1, ntot, LANES), lambda b, h, i: (b, h, 0, 0))
    a_lat = pl.pallas_call(
        functools.partial(_attn_kernel, tk=tk, nkt=nkt, lam_init=lam_init),
        grid=(nb, nh, nq),
        in_specs=[_full(lam_params.shape), _full(head_g.shape),
                  pl.BlockSpec((1, 1, tq, LANES), lambda b, h, i: (b, h, i, 0)),
                  pl.BlockSpec((1, 1, tq, LANES), lambda b, h, i: (b, h, jnp.minimum(i + 1, nq - 1), 0)),
                  kv_spec, kv_spec],
        out_specs=pl.BlockSpec((1, tq, LANES), lambda b, h, i: (b, i, h)),
        out_shape=jax.ShapeDtypeStruct((nb, ntot, nh * LANES), BF16),
        scratch_shapes=[pltpu.VMEM((nkt, rows, tk), F32)] * 2 + [pltpu.VMEM((rows, LANES), F32)] * 4,
        compiler_params=_params("parallel", "parallel", "arbitrary"),
        name="diff_attention",
    )(lam_params, head_g, q, q, k, v)
    cblk = n_lat // n_ctx
    ctx_spec = pl.BlockSpec((1, 1, n_ctx, LANES), lambda b, h: (b, h, cblk, 0))
    return pl.pallas_call(
        functools.partial(_attn_ctx_kernel, lam_init=lam_init),
        grid=(nb, nh),
        in_specs=[_full(lam_params.shape), _full(head_g.shape), ctx_spec, ctx_spec, ctx_spec,
                  pl.BlockSpec(memory_space=pl.ANY)],
        out_specs=pl.BlockSpec((1, n_ctx, LANES), lambda b, h: (b, cblk, h)),
        out_shape=jax.ShapeDtypeStruct(a_lat.shape, a_lat.dtype),
        input_output_aliases={5: 0},
        compiler_params=_params("parallel", "parallel"),
        name="diff_attention_ctx",
    )(lam_params, head_g, q, k, v, a_lat)


def _fourier_factor(n):
    a = 1 << (int(math.log2(n)) // 2)
    assert a * (n // a) == n and a % SUBLANES == 0 and (n // a) % SUBLANES == 0
    return a, n // a


def _fourier_tables(n):
    a, bn = _fourier_factor(n)
    j = np.arange(FN_CH)
    ang = 2.0 * np.pi * np.outer(j, j) / FN_CH
    eye = np.eye(FN_GROUPS)
    wc = np.concatenate([np.kron(eye, np.cos(ang)), -np.kron(eye, np.sin(ang))], axis=1)
    ia = np.arange(a)
    ang_a = 2.0 * np.pi * np.outer(ia, ia) / a
    ca, sa = np.cos(ang_a), np.sin(ang_a)
    m1 = np.block([[ca, sa], [-sa, ca]])
    ib = np.arange(bn)
    ang_t = 2.0 * np.pi * np.outer(ia, ib) / n
    ang_b = 2.0 * np.pi * np.outer(ib, ib) / bn
    m3 = np.concatenate([np.cos(ang_b), np.sin(ang_b)], axis=1) / math.sqrt(n * FN_CH)
    tw_c = jnp.repeat(jnp.asarray(np.cos(ang_t), F32), FN_COLS, axis=1)
    tw_s = jnp.repeat(jnp.asarray(np.sin(ang_t), F32), FN_COLS, axis=1)
    return (jnp.asarray(wc, BF16), jnp.asarray(m1, BF16), tw_c, tw_s, jnp.asarray(m3, BF16))


def _fourier_chan_kernel(x_ref, w_ref, o_ref):
    g = _dot(x_ref[0].astype(BF16), w_ref[...])
    o_ref[0, 0] = g[:, :FN_COLS]
    o_ref[0, 1] = g[:, FN_COLS:]


def _fourier_stage1_kernel(g_ref, m_ref, c_ref, s_ref, o_ref, *, a, bb):
    t1 = _dot(m_ref[...], g_ref[0].astype(BF16))
    tr, ti = t1[:a], t1[a:]
    c = c_ref[...]
    s = s_ref[...]
    t2r = tr * c + ti * s
    t2i = ti * c - tr * s
    for b in range(bb):
        o_ref[0, 0, b] = t2r[:, b * FN_COLS:(b + 1) * FN_COLS]
        o_ref[0, 1, b] = t2i[:, b * FN_COLS:(b + 1) * FN_COLS]


def _fourier_stage2_kernel(t_ref, m_ref, *rest):
    o_ref = rest[-1]
    o_ref[0] = _dot(m_ref[...], t_ref[0].astype(BF16))


def _fourier_mix(fcat, row0, n, out_buf):
    nb, ntot, _ = fcat.shape
    a, bn = _fourier_factor(n)
    wc, m1, tw_c, tw_s, m3 = _fourier_tables(n)
    tm = min(TOKEN_TILE, n)
    blk0 = row0 // tm
    g = pl.pallas_call(
        _fourier_chan_kernel,
        grid=(nb, n // tm),
        in_specs=[pl.BlockSpec((1, tm, FN_COLS), lambda b, i: (b, blk0 + i, 0)),
                  _full(wc.shape)],
        out_specs=pl.BlockSpec((1, 2, tm, FN_COLS), lambda b, i: (b, 0, i, 0)),
        out_shape=jax.ShapeDtypeStruct((nb, 2, n, FN_COLS), F32),
        compiler_params=_params("parallel", "parallel"),
        name="fourier_chan",
    )(fcat, wc)
    bb = min(8, bn)
    tc = bb * FN_COLS
    t2 = pl.pallas_call(
        functools.partial(_fourier_stage1_kernel, a=a, bb=bb),
        grid=(bn // bb, nb),
        in_specs=[pl.BlockSpec((1, 2 * a, tc), lambda j, b: (b, 0, j)),
                  _full(m1.shape),
                  pl.BlockSpec((a, tc), lambda j, b: (0, j)),
                  pl.BlockSpec((a, tc), lambda j, b: (0, j))],
        out_specs=pl.BlockSpec((1, 2, bb, a, FN_COLS), lambda j, b: (b, 0, j, 0, 0)),
        out_shape=jax.ShapeDtypeStruct((nb, 2, bn, a, FN_COLS), F32),
        compiler_params=_params("parallel", "parallel"),
        name="fourier_stage1",
    )(g.reshape(nb, 2 * a, bn * FN_COLS), m1, tw_c, tw_s)
    assert ntot % a == 0 and (row0 // a) % bn == 0
    tc2 = min(8, a) * FN_COLS
    out_rows = ntot // a
    in_specs = [pl.BlockSpec((1, 2 * bn, tc2), lambda b, j: (b, 0, j)), _full(m3.shape)]
    args = [t2.reshape(nb, 2 * bn, a * FN_COLS), m3]
    aliases = {}
    if out_buf is not None:
        in_specs.append(pl.BlockSpec(memory_space=pl.ANY))
        args.append(out_buf.reshape(nb, out_rows, a * FN_COLS))
        aliases = {2: 0}
    rblk = (row0 // a) // bn
    out = pl.pallas_call(
        _fourier_stage2_kernel,
        grid=(nb, a * FN_COLS // tc2),
        in_specs=in_specs,
        out_specs=pl.BlockSpec((1, bn, tc2), lambda b, j: (b, rblk, j)),
        out_shape=jax.ShapeDtypeStruct((nb, out_rows, a * FN_COLS), F32),
        input_output_aliases=aliases,
        compiler_params=_params("parallel", "parallel"),
        name="fourier_stage2",
    )(*args)
    return out.reshape(nb, ntot, FN_COLS)


def _post_norm(h, y, g, b):
    return _ln(ALPHA * h + y) * g + b


def _out_even_kernel(a_ref, f_ref, h_ref, mod_ref, w_ref, g_ref, b_ref, o_ref):
    na = a_ref.shape[-1]
    y = _dot(a_ref[0], w_ref[:na, :]) + _dot(f_ref[0].astype(BF16), w_ref[na:, :])
    o_ref[0] = _post_norm(h_ref[0], mod_ref[0, 2:3, :] * y, g_ref[...], b_ref[...])


def _out_even(a_n, fm, hcat, mod, w_out, ln_g, ln_b, n_lat):
    nb, ntot, d = hcat.shape
    tm = TOKEN_TILE
    na = a_n.shape[-1]
    tok = lambda c: pl.BlockSpec((1, tm, c), lambda b, i: (b, i, 0))
    return pl.pallas_call(
        _out_even_kernel,
        grid=(nb, ntot // tm),
        in_specs=[tok(na), tok(FN_COLS), tok(d), _mod_spec(nb, n_lat // tm, d),
                  _full(w_out.shape), _full((1, d)), _full((1, d))],
        out_specs=tok(d),
        out_shape=jax.ShapeDtypeStruct((nb, ntot, d), F32),
        compiler_params=_params("parallel", "parallel"),
        name="out_even",
    )(a_n, fm, hcat, mod, w_out, ln_g, ln_b)


def _ffn_kernel(h_ref, mod_ref, w1_ref, w3_ref, w2_ref, g_ref, b_ref, o_ref):
    h = h_ref[0]
    u = (_ln(h) * (1.0 + mod_ref[0, 4:5, :]) + mod_ref[0, 3:4, :]).astype(BF16)
    hid = (_silu(_dot(u, w1_ref[...])) * _dot(u, w3_ref[...])).astype(BF16)
    y = _dot(hid, w2_ref[...])
    o_ref[0] = _post_norm(h, mod_ref[0, 5:6, :] * y, g_ref[...], b_ref[...])


def _ffn(h, n_rows, mod, w1, w3, w2, ln_g, ln_b, n_lat):
    nb, _, d = h.shape
    tm = TOKEN_TILE
    tok = pl.BlockSpec((1, tm, d), lambda b, i: (b, i, 0))
    resident = lambda s: pl.BlockSpec(s, lambda b, i: (0, 0), pipeline_mode=pl.Buffered(1))
    return pl.pallas_call(
        _ffn_kernel,
        grid=(nb, n_rows // tm),
        in_specs=[tok, _mod_spec(nb, n_lat // tm, d),
                  resident(w1.shape), resident(w3.shape), resident(w2.shape),
                  _full((1, d)), _full((1, d))],
        out_specs=tok,
        out_shape=jax.ShapeDtypeStruct((nb, n_rows, d), F32),
        compiler_params=_params("parallel", "parallel"),
        name="ffn",
    )(h, mod, w1, w3, w2, ln_g, ln_b)


def _proj_odd_kernel(h_ref, mod_ref, w_ref, xm_ref, z_ref):
    x = h_ref[0]
    u = (_ln(x) * (1.0 + mod_ref[0, 1:2, :]) + mod_ref[0, 0:1, :]).astype(BF16)
    p = _dot(u, w_ref[...])
    inner = xm_ref.shape[-1]
    xm_ref[0] = p[:, :inner]
    z_ref[0] = p[:, inner:]


def _proj_odd(hcat, mod, w, n_lat):
    nb, ntot, d = hcat.shape
    tm = TOKEN_TILE
    inner = w.shape[1] // 2
    out_spec = pl.BlockSpec((1, tm, inner), lambda b, i: (b, i, 0))
    out_shape = jax.ShapeDtypeStruct((nb, ntot, inner), F32)
    return pl.pallas_call(
        _proj_odd_kernel,
        grid=(nb, ntot // tm),
        in_specs=[pl.BlockSpec((1, tm, d), lambda b, i: (b, i, 0)),
                  _mod_spec(nb, n_lat // tm, d),
                  _full(w.shape)],
        out_specs=[out_spec, out_spec],
        out_shape=[out_shape, out_shape],
        compiler_params=_params("parallel", "parallel"),
        name="proj_odd",
    )(hcat, mod, w)


def _mfeat_kernel(xm_ref, cw_ref, cb_ref, wq_ref, wk_ref, wkt_ref, wv_ref, wg_ref,
                  xc_ref, q_ref, k_ref, v_ref, kt_ref, gr_ref, *, n_lat, chunk, row_group):
    ntot = xm_ref.shape[1]
    ct = CONV_TILE
    half = ML_CONV_W // 2
    cw = cw_ref[...]
    cb = cb_ref[...]

    def conv_tile(t, carry):
        r0 = pl.multiple_of(t * ct, ct)
        at_start = jnp.logical_or(r0 == 0, r0 == n_lat)
        at_end = jnp.logical_or(r0 + ct == n_lat, r0 + ct == ntot)
        p0 = pl.multiple_of(jnp.maximum(r0 - SUBLANES, 0), SUBLANES)
        n0 = pl.multiple_of(jnp.minimum(r0 + ct, ntot - SUBLANES), SUBLANES)
        prev = xm_ref[0, pl.ds(p0, SUBLANES), :]
        nxt = xm_ref[0, pl.ds(n0, SUBLANES), :]
        prev = jnp.where(at_start, 0.0, prev)
        nxt = jnp.where(at_end, 0.0, nxt)
        xe = jnp.concatenate([prev, xm_ref[0, pl.ds(r0, ct), :], nxt], axis=0)
        acc = cb
        for dd in range(ML_CONV_W):
            lo = SUBLANES - half + dd
            acc = acc + cw[dd:dd + 1, :] * xe[lo:lo + ct, :]
        xc_ref[0, pl.ds(r0, ct), :] = _silu(acc)
        return carry

    lax.fori_loop(0, ntot // ct, conv_tile, 0)

    for g in range(ntot // row_group):
        rows = pl.ds(g * row_group, row_group)
        xcb = xc_ref[0, rows, :].astype(BF16)
        xmb = xm_ref[0, rows, :].astype(BF16)
        qb = _dot(xcb, wq_ref[0]).astype(BF16)
        kb = _dot(xcb, wk_ref[0]).astype(BF16)
        vb = _dot(xmb, wv_ref[0]).astype(BF16)
        q_ref[0, rows, :] = qb
        k_ref[0, rows, :] = kb
        v_ref[0, rows, :] = vb
        gr_ref[0, 0, :, rows] = _dot_nt(wg_ref[0], jnp.concatenate([qb, kb, vb], axis=1))

    def kt_chunk(c, carry):
        r0 = pl.multiple_of(c * chunk, chunk)
        kt_ref[0, 0, c] = _dot_nt(wkt_ref[0], xc_ref[0, pl.ds(r0, chunk), :].astype(BF16)).astype(BF16)
        return carry

    lax.fori_loop(0, ntot // chunk, kt_chunk, 0)


def _mfeat(xm, conv_w, conv_b, wq, wk, wkt, wv, wg, n_lat, chunk):
    nb, ntot, inner = xm.shape
    nh = ML_HEADS
    dh = inner // nh
    ng = 2 if ntot % (2 * LANES) == 0 else 1
    kern = functools.partial(_mfeat_kernel, n_lat=n_lat, chunk=chunk, row_group=ntot // ng)
    seq = pl.BlockSpec((1, ntot, dh), lambda b, h: (b, 0, h))
    wblk = pl.BlockSpec((1, dh, dh), lambda b, h: (h, 0, 0))
    n_gates = wg.shape[1]
    seq_bf = jax.ShapeDtypeStruct((nb, ntot, inner), BF16)
    return pl.pallas_call(
        kern,
        grid=(nb, nh),
        in_specs=[seq,
                  pl.BlockSpec((ML_CONV_W, dh), lambda b, h: (0, h)),
                  pl.BlockSpec((1, dh), lambda b, h: (0, h)),
                  wblk, wblk, wblk, wblk,
                  pl.BlockSpec((1, n_gates, 3 * dh), lambda b, h: (h, 0, 0))],
        out_specs=[seq, seq, seq, seq,
                   pl.BlockSpec((1, 1, ntot // chunk, dh, chunk), lambda b, h: (b, h, 0, 0, 0)),
                   pl.BlockSpec((1, 1, n_gates, ntot), lambda b, h: (b, h, 0, 0))],
        out_shape=[jax.ShapeDtypeStruct((nb, ntot, inner), F32), seq_bf, seq_bf, seq_bf,
                   jax.ShapeDtypeStruct((nb, nh, ntot // chunk, dh, chunk), BF16),
                   jax.ShapeDtypeStruct((nb, nh, n_gates, ntot), F32)],
        compiler_params=_params("parallel", "parallel"),
        name="mlstm_features",
    )(xm, conv_w, conv_b, wq, wk, wkt, wv, wg)


def _split3(x):
    hi = x.astype(BF16)
    r1 = x - hi.astype(F32)
    mid = r1.astype(BF16)
    lo = (r1 - mid.astype(F32)).astype(BF16)
    return hi, mid, lo


def _gates_kernel(g_ref, b_ref, rows_ref, cols_ref):
    nh = ML_HEADS
    chunk = g_ref.shape[-1]
    pre = jnp.sum(g_ref[0], axis=0) + b_ref[...]
    log_i = pre[:2 * nh]
    xf = pre[2 * nh:]
    log_f = jnp.minimum(xf, 0.0) - jnp.log1p(jnp.exp(-jnp.abs(xf)))
    s_idx = lax.broadcasted_iota(jnp.int32, (chunk, chunk), 0)
    t_idx = lax.broadcasted_iota(jnp.int32, (chunk, chunk), 1)
    tri_f = jnp.where(s_idx <= t_idx, 1.0, 0.0).astype(BF16)
    tri_b = jnp.where(s_idx >= t_idx, 1.0, 0.0).astype(BF16)
    cum_f = sum(_dot(p, tri_f) for p in _split3(log_f[:nh]))
    cum_b = sum(_dot(p, tri_b) for p in _split3(log_f[nh:]))
    rows = jnp.concatenate([log_i, cum_f, cum_b], axis=0)
    rows_ref[0, 0] = rows
    pad = jnp.zeros((LANES - 4 * nh, chunk), F32)
    cols_ref[0] = jnp.transpose(jnp.concatenate([rows, pad], axis=0))[:, :4 * nh]


def _gates(g_part, bias, chunk):
    nb, nh, n_gates, ntot = g_part.shape
    nct = ntot // chunk
    return pl.pallas_call(
        _gates_kernel,
        grid=(nb, nct),
        in_specs=[pl.BlockSpec((1, nh, n_gates, chunk), lambda b, c: (b, 0, 0, c)),
                  _full(bias.shape)],
        out_specs=[pl.BlockSpec((1, 1, n_gates, chunk), lambda b, c: (b, c, 0, 0)),
                   pl.BlockSpec((1, chunk, n_gates), lambda b, c: (b, c, 0))],
        out_shape=[jax.ShapeDtypeStruct((nb, nct, n_gates, chunk), F32),
                   jax.ShapeDtypeStruct((nb, ntot, n_gates), F32)],
        compiler_params=_params("parallel", "parallel"),
        name="mlstm_gates",
    )(g_part, bias)


def _scan_kernel(q_ref, k_ref, v_ref, kt_ref, rows_ref, cols_ref, hs_ref, c_sc, n_sc, m_sc,
                 *, n_lat_chunks, n_ctx_chunks, chunk):
    nh = ML_HEADS
    h = pl.program_id(1)
    c_sc[...] = jnp.zeros(c_sc.shape, F32)
    n_sc[...] = jnp.zeros(n_sc.shape, F32)
    m_sc[...] = jnp.zeros(m_sc.shape, F32)
    t_idx = lax.broadcasted_iota(jnp.int32, (chunk, chunk), 0)
    s_idx = lax.broadcasted_iota(jnp.int32, (chunk, chunk), 1)
    gate_lane = lax.broadcasted_iota(jnp.int32, (1, 4 * nh), 1)

    def step(d, cidx, with_out):
        off = pl.multiple_of(cidx * chunk, chunk)
        rows = pl.ds(off, chunk)
        qc = q_ref[0, rows, :]
        kc = k_ref[0, rows, :]
        vc = v_ref[0, rows, :]
        ktc = kt_ref[0, 0, cidx]
        li_r = rows_ref[0, cidx, pl.ds(d * nh + h, 1), :]
        bc_r = rows_ref[0, cidx, pl.ds((2 + d) * nh + h, 1), :]
        ctile = cols_ref[0, rows, :]
        li_c = jnp.sum(jnp.where(gate_lane == d * nh + h, ctile, 0.0), axis=1, keepdims=True)
        bc_c = jnp.sum(jnp.where(gate_lane == (2 + d) * nh + h, ctile, 0.0), axis=1, keepdims=True)
        m = m_sc[d]
        cmat = c_sc[d]
        nrow = n_sc[d]
        b_end = bc_r[:, chunk - 1:chunk] if d == 0 else bc_r[:, 0:1]
        if with_out:
            ordered = (s_idx <= t_idx) if d == 0 else (s_idx >= t_idx)
            dmat = jnp.where(ordered, bc_c - bc_r + li_r, -jnp.inf)
            inter = bc_c + m
            m_t = jnp.maximum(jnp.max(dmat, axis=1, keepdims=True), inter)
            dw = jnp.exp(dmat - m_t)
            iw = jnp.exp(inter - m_t)
            s = _dot(qc, ktc) * dw
            num = iw * _dot(qc, cmat.astype(BF16)) + _dot(s.astype(BF16), vc)
            qn = jnp.sum(qc.astype(F32) * nrow, axis=1, keepdims=True)
            den = iw * qn + jnp.sum(s, axis=1, keepdims=True)
            hs_ref[d, 0, rows, :] = num / jnp.maximum(jnp.abs(den), jnp.exp(-m_t))
        w_end_r = b_end - bc_r + li_r
        w_end_c = b_end - bc_c + li_c
        m_new = jnp.maximum(b_end + m, jnp.max(w_end_r, axis=1, keepdims=True))
        keep = jnp.exp(b_end + m - m_new)
        w_r = jnp.exp(w_end_r - m_new)
        w_c = jnp.exp(w_end_c - m_new)
        kw = (ktc.astype(F32) * w_r).astype(BF16)
        c_sc[d] = keep * cmat + _dot(kw, vc)
        n_sc[d] = keep * nrow + jnp.sum(kc.astype(F32) * w_c, axis=0, keepdims=True)
        m_sc[d] = m_new

    def ctx_body(c, carry):
        step(0, n_lat_chunks + c, False)
        step(1, n_lat_chunks + n_ctx_chunks - 1 - c, False)
        return carry

    def lat_body(c, carry):
        step(0, c, True)
        step(1, n_lat_chunks - 1 - c, True)
        return carry

    lax.fori_loop(0, n_ctx_chunks, ctx_body, 0)
    lax.fori_loop(0, n_lat_chunks, lat_body, 0)


def _scan(q, k, v, kt, rows, cols, n_lat, chunk):
    nb, ntot, inner = q.shape
    nh = ML_HEADS
    dh = inner // nh
    nct = ntot // chunk
    n_gates = rows.shape[2]
    kern = functools.partial(_scan_kernel, n_lat_chunks=n_lat // chunk,
                             n_ctx_chunks=(ntot - n_lat) // chunk, chunk=chunk)
    seq = pl.BlockSpec((1, ntot, dh), lambda b, h: (b, 0, h))
    return pl.pallas_call(
        kern,
        grid=(nb, nh),
        in_specs=[seq, seq, seq,
                  pl.BlockSpec((1, 1, nct, dh, chunk), lambda b, h: (b, h, 0, 0, 0)),
                  pl.BlockSpec((1, nct, n_gates, chunk), lambda b, h: (b, 0, 0, 0)),
                  pl.BlockSpec((1, ntot, n_gates), lambda b, h: (b, 0, 0))],
        out_specs=pl.BlockSpec((2, 1, n_lat, dh), lambda b, h: (0, b, 0, h)),
        out_shape=jax.ShapeDtypeStruct((2, nb, n_lat, inner), F32),
        scratch_shapes=[pltpu.VMEM((2, dh, dh), F32), pltpu.VMEM((2, 1, dh), F32),
                        pltpu.VMEM((2, 1, 1), F32)],
        compiler_params=_params("parallel", "parallel"),
        name="mlstm_scan",
    )(q, k, v, kt, rows, cols)


def _out_odd_kernel(hs_ref, xc_ref, z_ref, h_ref, mod_ref, skip_ref, hg_ref, w_ref, g_ref, b_ref, o_ref):
    hsum = hs_ref[0, 0] + hs_ref[1, 0]
    dh = hsum.shape[-1] // ML_HEADS
    hn = jnp.concatenate([_ln(hsum[:, i * dh:(i + 1) * dh]) for i in range(ML_HEADS)], axis=1)
    y = (hn * hg_ref[...] + skip_ref[...] * xc_ref[0]) * _silu(z_ref[0])
    yo = _dot(y.astype(BF16), w_ref[...])
    o_ref[0] = _post_norm(h_ref[0], mod_ref[0, 2:3, :] * yo, g_ref[...], b_ref[...])


def _out_odd(hs, xc, z, hcat, mod, skip, head_g, w_out, ln_g, ln_b, n_lat):
    nb, _, d = hcat.shape
    inner = xc.shape[-1]
    tm = TOKEN_TILE
    tok = lambda c: pl.BlockSpec((1, tm, c), lambda b, i: (b, i, 0))
    return pl.pallas_call(
        _out_odd_kernel,
        grid=(nb, n_lat // tm),
        in_specs=[pl.BlockSpec((2, 1, tm, inner), lambda b, i: (0, b, i, 0)),
                  tok(inner), tok(inner), tok(d),
                  pl.BlockSpec((1, 6, d), lambda b, i: (b, 0, 0)),
                  _full((1, inner)), _full((1, inner)), _full(w_out.shape),
                  _full((1, d)), _full((1, d))],
        out_specs=tok(d),
        out_shape=jax.ShapeDtypeStruct((nb, n_lat, d), F32),
        compiler_params=_params("parallel", "parallel"),
        name="out_odd",
    )(hs, xc, z, hcat, mod, skip, head_g, w_out, ln_g, ln_b)


def _even_in_columns():
    cols, scale = [], []
    quarter = DA_DK // 2
    for hd in range(DA_HEADS):
        base = hd * DA_HEAD_COLS
        for blk in range(2):
            b0 = base + blk * 2 * DA_DK
            for half in range(2):
                for m in range(2):
                    cols += [b0 + m * DA_DK + half * quarter + j for j in range(quarter)]
            scale += [DA_DK ** -0.5 if blk == 0 else 1.0] * (2 * DA_DK)
        cols += list(range(base + 4 * DA_DK, base + DA_HEAD_COLS))
        scale += [1.0] * DA_DV
    cols += list(range(DA_HEADS * DA_HEAD_COLS, DA_HEADS * DA_HEAD_COLS + FN_COLS))
    scale += [1.0] * FN_COLS
    return np.asarray(cols, np.int32), np.asarray(scale, np.float32)


def _rope_tables(n_lat, ntot):
    rows = n_lat // GRID_W
    row = jnp.repeat(jnp.arange(rows, dtype=F32), GRID_W)
    col = jnp.tile(jnp.arange(GRID_W, dtype=F32), rows)
    n_freq = DA_DK // 4
    inv_freq = ROPE_BASE ** (-jnp.arange(n_freq, dtype=F32) / n_freq)
    ang = jnp.concatenate([row[:, None] * inv_freq, col[:, None] * inv_freq], -1)
    cos, sin = jnp.cos(ang), jnp.sin(ang)
    c = jnp.concatenate([cos, cos, cos, cos], axis=1)
    s = jnp.concatenate([-sin, -sin, sin, sin], axis=1)
    n_ctx = ntot - n_lat
    c = jnp.concatenate([c, jnp.ones((n_ctx, LANES), F32)], axis=0)
    s = jnp.concatenate([s, jnp.zeros((n_ctx, LANES), F32)], axis=0)
    return c, s


def _block_diag_heads(w, dh):
    nblk = w.shape[0]
    per_head = dh // ML_BLOCK
    wh = w.reshape(nblk // per_head, per_head, ML_BLOCK, ML_BLOCK)
    eye = jnp.eye(per_head, dtype=w.dtype)
    dense = jnp.einsum('hgij,gk->hgikj', wh, eye)
    return dense.reshape(nblk // per_head, dh, dh)


def kernel(x, c, ctx, c_ctx, w_mod, b_mod, ln_g, ln_b, w_ff1, w_ff3, w_ff2, a_w_in, a_w_out, da_lq1, da_lk1, da_lq2, da_lk2, da_head_g, m_w_in, m_w_out, m_conv_w, m_conv_b, m_wq, m_wk, m_wv, m_w_ig, m_b_ig, m_w_fg, m_b_fg, m_skip, m_head_g):
    nb, n_lat, d = x.shape
    n_ctx = ctx.shape[1]
    ntot = n_lat + n_ctx
    assert w_mod.shape[0] == DEPTH == 2
    assert n_lat % TOKEN_TILE == 0 and n_ctx % TOKEN_TILE == 0 and n_lat % n_ctx == 0 and n_lat % GRID_W == 0

    r = -(-(nb + 1) // MOD_ROWS_PAD) * MOD_ROWS_PAD
    cvec = jnp.concatenate([c, c_ctx[None, :], jnp.zeros((r - nb - 1, d), F32)], axis=0)
    mod_all = _modulation(cvec, w_mod, b_mod).reshape(DEPTH, r, 6, d)

    hcat = jnp.concatenate([x, ctx], axis=1)

    lam_init0 = 0.8 - 0.6 * math.exp(-0.3 * 0)
    cols, colscale = _even_in_columns()
    w_in = (a_w_in[0][:, cols] * colscale[None, :]).astype(BF16)
    rope_c, rope_s = _rope_tables(n_lat, ntot)
    q, k, v, f = _proj_even(hcat, mod_all[0], w_in, rope_c, rope_s, n_lat)
    lam_params = jnp.stack([da_lq1[0], da_lk1[0], da_lq2[0], da_lk2[0]], axis=0)
    a_n = _attention(q, k, v, lam_params, da_head_g[0][None, :], n_lat, lam_init0)
    fm = _fourier_mix(f, 0, n_lat, None)
    fm = _fourier_mix(f, n_lat, n_ctx, fm)
    h1 = _out_even(a_n, fm, hcat, mod_all[0], a_w_out[0].astype(BF16),
                   ln_g[0, 0][None, :], ln_b[0, 0][None, :], n_lat)
    hcat = _ffn(h1, ntot, mod_all[0], w_ff1[0].astype(BF16), w_ff3[0].astype(BF16), w_ff2[0].astype(BF16),
                ln_g[0, 1][None, :], ln_b[0, 1][None, :], n_lat)

    inner = m_w_in.shape[2] // 2
    dh = inner // ML_HEADS
    chunk = MXU_DIM if (n_lat % MXU_DIM == 0 and n_ctx % MXU_DIM == 0) else LANES
    xm, z = _proj_odd(hcat, mod_all[1], m_w_in[0].astype(BF16), n_lat)
    wq = _block_diag_heads(m_wq[0], dh).astype(BF16)
    wk_f = _block_diag_heads(m_wk[0], dh) * (dh ** -0.5)
    wk = wk_f.astype(BF16)
    wkt = jnp.swapaxes(wk_f, 1, 2).astype(BF16)
    wv = _block_diag_heads(m_wv[0], dh).astype(BF16)
    wg_all = jnp.concatenate([m_w_ig[0, 0], m_w_ig[0, 1], m_w_fg[0, 0], m_w_fg[0, 1]], axis=1)
    wg_all = wg_all.reshape(3, ML_HEADS, dh, 4 * ML_HEADS) * jnp.asarray([1.0, dh ** 0.5, 1.0], F32)[:, None, None, None]
    wg = jnp.transpose(wg_all, (1, 3, 0, 2)).reshape(ML_HEADS, 4 * ML_HEADS, 3 * dh).astype(BF16)
    g_bias = jnp.concatenate([m_b_ig[0, 0], m_b_ig[0, 1], m_b_fg[0, 0], m_b_fg[0, 1]])[:, None]
    xc, qm, km, vm, kt, g_part = _mfeat(xm, m_conv_w[0], m_conv_b[0][None, :], wq, wk, wkt, wv, wg, n_lat, chunk)
    rows, colsg = _gates(g_part, g_bias, chunk)
    hs = _scan(qm, km, vm, kt, rows, colsg, n_lat, chunk)
    h1 = _out_odd(hs, xc, z, hcat, mod_all[1], m_skip[0][None, :], m_head_g[0][None, :],
                  m_w_out[0].astype(BF16), ln_g[1, 0][None, :], ln_b[1, 0][None, :], n_lat)
    return _ffn(h1, n_lat, mod_all[1], w_ff1[1].astype(BF16), w_ff3[1].astype(BF16), w_ff2[1].astype(BF16),
                ln_g[1, 1][None, :], ln_b[1, 1][None, :], n_lat)
```

```python
import functools
import math

import numpy as np
import jax
import jax.numpy as jnp
from jax import lax
from jax.experimental import pallas as pl
from jax.experimental.pallas import tpu as pltpu

F32 = jnp.float32
BF16 = jnp.bfloat16

DA_HEADS = 6
DA_DK = 64
DA_DV = 2 * DA_DK
DA_HEAD_COLS = 4 * DA_DK + DA_DV
FN_GROUPS = 4
FN_CH = 64
FN_COLS = FN_GROUPS * FN_CH
ML_HEADS = 8
ML_BLOCK = 4
ML_CONV_W = 5
GRID_W = 64
ROPE_BASE = 10000.0
LN_EPS = 1e-5
DEPTH = 2
ALPHA = (2 * DEPTH) ** 0.25

LANES = 128
SUBLANES = 8
MXU_DIM = 256
VMEM_LIMIT_BYTES = 60 * 1024 * 1024

TOKEN_TILE = 256
CONV_TILE = 128
MOD_ROWS_PAD = 8
SWEEP_UNROLL = 8


def _params(*sem):
    return pltpu.CompilerParams(dimension_semantics=sem, vmem_limit_bytes=VMEM_LIMIT_BYTES)


def _ln(x):
    mu = jnp.mean(x, axis=-1, keepdims=True)
    xc = x - mu
    var = jnp.mean(xc * xc, axis=-1, keepdims=True)
    return xc * lax.rsqrt(var + LN_EPS)


def _silu(x):
    return x * jax.nn.sigmoid(x)


def _dot(a, b):
    return jnp.dot(a, b, preferred_element_type=F32)


def _dot_nt(a, b):
    return lax.dot_general(a, b, (((1,), (1,)), ((), ())), preferred_element_type=F32)


def _full(shape):
    n = len(shape)
    return pl.BlockSpec(shape, lambda *_: (0,) * n)


def _mod_kernel(c_ref, w_ref, b_ref, o_ref):
    s = _silu(c_ref[...]).astype(BF16)
    o_ref[0] = _dot(s, w_ref[0].astype(BF16)) + b_ref[0]


def _modulation(cvec, w_mod, b_mod):
    depth, d, d6 = w_mod.shape
    r = cvec.shape[0]
    tn = d6 // 4
    return pl.pallas_call(
        _mod_kernel,
        grid=(depth, d6 // tn),
        in_specs=[pl.BlockSpec((r, d), lambda l, j: (0, 0)),
                  pl.BlockSpec((1, d, tn), lambda l, j: (l, 0, j)),
                  pl.BlockSpec((1, 1, tn), lambda l, j: (l, 0, j))],
        out_specs=pl.BlockSpec((1, r, tn), lambda l, j: (l, 0, j)),
        out_shape=jax.ShapeDtypeStruct((depth, r, d6), F32),
        compiler_params=_params("parallel", "parallel"),
        name="modulation",
    )(cvec, w_mod, b_mod.reshape(depth, 1, d6))


def _mod_spec(nb, n_lat_tiles, d):
    return pl.BlockSpec((1, 6, d), lambda b, i: (jnp.where(i < n_lat_tiles, b, nb), 0, 0))


def _proj_even_kernel(h_ref, mod_ref, w_ref, c_ref, s_ref, q_ref, k_ref, v_ref, f_ref):
    x = h_ref[0]
    u = (_ln(x) * (1.0 + mod_ref[0, 1:2, :]) + mod_ref[0, 0:1, :]).astype(BF16)
    p = _dot(u, w_ref[...])
    c = c_ref[...]
    s = s_ref[...]
    blk = DA_HEAD_COLS
    for hd in range(DA_HEADS):
        pq = p[:, hd * blk: hd * blk + LANES]
        pk = p[:, hd * blk + LANES: hd * blk + 2 * LANES]
        q_ref[0, hd] = (pq * c + pltpu.roll(pq, LANES // 2, 1) * s).astype(BF16)
        k_ref[0, hd] = (pk * c + pltpu.roll(pk, LANES // 2, 1) * s).astype(BF16)
        v_ref[0, hd] = p[:, hd * blk + 2 * LANES: (hd + 1) * blk].astype(BF16)
    f_ref[0] = p[:, DA_HEADS * blk:]


def _proj_even(hcat, mod, w, rope_c, rope_s, n_lat):
    nb, ntot, d = hcat.shape
    tm = TOKEN_TILE
    nt = ntot // tm
    ncols = w.shape[1]
    qkv_shape = jax.ShapeDtypeStruct((nb, DA_HEADS, ntot, LANES), BF16)
    qkv_spec = pl.BlockSpec((1, DA_HEADS, tm, LANES), lambda b, i: (b, 0, i, 0))
    return pl.pallas_call(
        _proj_even_kernel,
        grid=(nb, nt),
        in_specs=[pl.BlockSpec((1, tm, d), lambda b, i: (b, i, 0)),
                  _mod_spec(nb, n_lat // tm, d),
                  _full((d, ncols)),
                  pl.BlockSpec((tm, LANES), lambda b, i: (i, 0)),
                  pl.BlockSpec((tm, LANES), lambda b, i: (i, 0))],
        out_specs=[qkv_spec, qkv_spec, qkv_spec,
                   pl.BlockSpec((1, tm, FN_COLS), lambda b, i: (b, i, 0))],
        out_shape=[qkv_shape, qkv_shape, qkv_shape,
                   jax.ShapeDtypeStruct((nb, ntot, FN_COLS), F32)],
        compiler_params=_params("parallel", "parallel"),
        name="proj_even",
    )(hcat, mod, w, rope_c, rope_s)


def _stack_maps(q):
    qf = q.astype(F32)
    lane = lax.broadcasted_iota(jnp.int32, (1, LANES), 1)
    map1 = (lane % (LANES // 2)) < (LANES // 4)
    return jnp.concatenate([jnp.where(map1, qf, 0.0), jnp.where(map1, 0.0, qf)], axis=0).astype(BF16)


def _lane_block_max(s):
    blk = s[:, :LANES]
    for c in range(1, s.shape[1] // LANES):
        blk = jnp.maximum(blk, s[:, c * LANES:(c + 1) * LANES])
    return blk


def _exp_blocks(s, mb):
    return [jnp.exp(s[:, c * LANES:(c + 1) * LANES] - mb) for c in range(s.shape[1] // LANES)]


def _diff_merge(o, lp, head_g, lam_init):
    tq = o.shape[0] // 2
    lam = (jnp.exp(jnp.sum(lp[0:1] * lp[1:2], keepdims=True))
           - jnp.exp(jnp.sum(lp[2:3] * lp[3:4], keepdims=True)) + lam_init)
    a = o[:tq] - lam * o[tq:]
    a_n = a * lax.rsqrt(jnp.mean(a * a, axis=-1, keepdims=True) + LN_EPS) * (head_g * (1.0 - lam_init))
    return a_n.astype(BF16)


def _attn_kernel(lp_ref, g_ref, qc_ref, qn_ref, k_ref, v_ref, o_ref, sa_sc, sb_sc, mcur_sc, mnxt_sc, l_sc, acc_sc,
                 *, tk, nkt, lam_init):
    i = pl.program_id(2)
    rows = mcur_sc.shape[0]

    def score_chunk(qq, s_ref, j):
        off = pl.multiple_of(j * tk, tk)
        s = _dot_nt(qq, k_ref[0, 0, pl.ds(off, tk), :])
        s_ref[j] = s
        mnxt_sc[...] = jnp.maximum(mnxt_sc[...], _lane_block_max(s))

    def row_max():
        return jnp.broadcast_to(jnp.max(mnxt_sc[...], axis=1, keepdims=True), (rows, LANES))

    @pl.when(i == 0)
    def _():
        qq0 = _stack_maps(qc_ref[0, 0])
        mnxt_sc[...] = jnp.full(mnxt_sc.shape, -jnp.inf, F32)

        def first(j, carry):
            score_chunk(qq0, sa_sc, j)
            return carry

        lax.fori_loop(0, nkt, first, 0)
        mcur_sc[...] = row_max()

    qqn = _stack_maps(qn_ref[0, 0])
    mnxt_sc[...] = jnp.full(mnxt_sc.shape, -jnp.inf, F32)
    l_sc[...] = jnp.zeros(l_sc.shape, F32)
    acc_sc[...] = jnp.zeros(acc_sc.shape, F32)

    def sweep(cur_ref, nxt_ref):
        def body(j, carry):
            score_chunk(qqn, nxt_ref, j)
            off = pl.multiple_of(j * tk, tk)
            ps = _exp_blocks(cur_ref[j], mcur_sc[...])
            part = ps[0]
            for pc in ps[1:]:
                part = part + pc
            l_sc[...] += part
            acc_sc[...] += _dot(jnp.concatenate(ps, axis=1).astype(BF16), v_ref[0, 0, pl.ds(off, tk), :])
            return carry

        lax.fori_loop(0, nkt, body, 0, unroll=SWEEP_UNROLL)

    @pl.when(i % 2 == 0)
    def _():
        sweep(sa_sc, sb_sc)

    @pl.when(i % 2 == 1)
    def _():
        sweep(sb_sc, sa_sc)

    o = acc_sc[...] / jnp.sum(l_sc[...], axis=1, keepdims=True)
    o_ref[0] = _diff_merge(o, lp_ref[...], g_ref[...], lam_init)
    mcur_sc[...] = row_max()


def _attn_ctx_kernel(lp_ref, g_ref, q_ref, k_ref, v_ref, buf_ref, o_ref, *, lam_init):
    del buf_ref
    s = _dot_nt(_stack_maps(q_ref[0, 0]), k_ref[0, 0])
    p = jnp.exp(s - jnp.max(s, axis=1, keepdims=True))
    o = _dot(p.astype(BF16), v_ref[0, 0]) / jnp.sum(p, axis=1, keepdims=True)
    o_ref[0] = _diff_merge(o, lp_ref[...], g_ref[...], lam_init)


def _attention(q, k, v, lam_params, head_g, n_lat, lam_init):
    nb, nh, ntot, _ = q.shape
    n_ctx = ntot - n_lat
    tq = TOKEN_TILE
    tk = TOKEN_TILE
    nq = n_lat // tq
    nkt = ntot // tk
    rows = 2 * tq
    kv_spec = pl.BlockSpec((1, 1, ntot, LANES), lambda b, h, i: (b, h, 0, 0))
    a_lat = pl.pallas_call(
        functools.partial(_attn_kernel, tk=tk, nkt=nkt, lam_init=lam_init),
        grid=(nb, nh, nq),
        in_specs=[_full(lam_params.shape), _full(head_g.shape),
                  pl.BlockSpec((1, 1, tq, LANES), lambda b, h, i: (b, h, i, 0)),
                  pl.BlockSpec((1, 1, tq, LANES), lambda b, h, i: (b, h, jnp.minimum(i + 1, nq - 1), 0)),
                  kv_spec, kv_spec],
        out_specs=pl.BlockSpec((1, tq, LANES), lambda b, h, i: (b, i, h)),
        out_shape=jax.ShapeDtypeStruct((nb, ntot, nh * LANES), BF16),
        scratch_shapes=[pltpu.VMEM((nkt, rows, tk), F32)] * 2 + [pltpu.VMEM((rows, LANES), F32)] * 4,
        compiler_params=_params("parallel", "parallel", "arbitrary"),
        name="diff_attention",
    )(lam_params, head_g, q, q, k, v)
    cblk = n_lat // n_ctx
    ctx_spec = pl.BlockSpec((1, 1, n_ctx, LANES), lambda b, h: (b, h, cblk, 0))
    return pl.pallas_call(
        functools.partial(_attn_ctx_kernel, lam_init=lam_init),
        grid=(nb, nh),
        in_specs=[_full(lam_params.shape), _full(head_g.shape), ctx_spec, ctx_spec, ctx_spec,
                  pl.BlockSpec(memory_space=pl.ANY)],
        out_specs=pl.BlockSpec((1, n_ctx, LANES), lambda b, h: (b, cblk, h)),
        out_shape=jax.ShapeDtypeStruct(a_lat.shape, a_lat.dtype),
        input_output_aliases={5: 0},
        compiler_params=_params("parallel", "parallel"),
        name="diff_attention_ctx",
    )(lam_params, head_g, q, k, v, a_lat)


def _fourier_factor(n):
    a = 1 << (int(math.log2(n)) // 2)
    assert a * (n // a) == n and a % SUBLANES == 0 and (n // a) % SUBLANES == 0
    return a, n // a


def _fourier_tables(n):
    a, bn = _fourier_factor(n)
    j = np.arange(FN_CH)
    ang = 2.0 * np.pi * np.outer(j, j) / FN_CH
    eye = np.eye(FN_GROUPS)
    wc = np.concatenate([np.kron(eye, np.cos(ang)), -np.kron(eye, np.sin(ang))], axis=1)
    ia = np.arange(a)
    ang_a = 2.0 * np.pi * np.outer(ia, ia) / a
    ca, sa = np.cos(ang_a), np.sin(ang_a)
    m1 = np.block([[ca, sa], [-sa, ca]])
    ib = np.arange(bn)
    ang_t = 2.0 * np.pi * np.outer(ia, ib) / n
    ang_b = 2.0 * np.pi * np.outer(ib, ib) / bn
    m3 = np.concatenate([np.cos(ang_b), np.sin(ang_b)], axis=1) / math.sqrt(n * FN_CH)
    tw_c = jnp.repeat(jnp.asarray(np.cos(ang_t), F32), FN_COLS, axis=1)
    tw_s = jnp.repeat(jnp.asarray(np.sin(ang_t), F32), FN_COLS, axis=1)
    return (jnp.asarray(wc, BF16), jnp.asarray(m1, BF16), tw_c, tw_s, jnp.asarray(m3, BF16))


def _fourier_chan_kernel(x_ref, w_ref, o_ref):
    g = _dot(x_ref[0].astype(BF16), w_ref[...])
    o_ref[0, 0] = g[:, :FN_COLS]
    o_ref[0, 1] = g[:, FN_COLS:]


def _fourier_stage1_kernel(g_ref, m_ref, c_ref, s_ref, o_ref, *, a, bb):
    t1 = _dot(m_ref[...], g_ref[0].astype(BF16))
    tr, ti = t1[:a], t1[a:]
    c = c_ref[...]
    s = s_ref[...]
    t2r = tr * c + ti * s
    t2i = ti * c - tr * s
    for b in range(bb):
        o_ref[0, 0, b] = t2r[:, b * FN_COLS:(b + 1) * FN_COLS]
        o_ref[0, 1, b] = t2i[:, b * FN_COLS:(b + 1) * FN_COLS]


def _fourier_stage2_kernel(t_ref, m_ref, *rest):
    o_ref = rest[-1]
    o_ref[0] = _dot(m_ref[...], t_ref[0].astype(BF16))


def _fourier_mix(fcat, row0, n, out_buf):
    nb, ntot, _ = fcat.shape
    a, bn = _fourier_factor(n)
    wc, m1, tw_c, tw_s, m3 = _fourier_tables(n)
    tm = min(TOKEN_TILE, n)
    blk0 = row0 // tm
    g = pl.pallas_call(
        _fourier_chan_kernel,
        grid=(nb, n // tm),
        in_specs=[pl.BlockSpec((1, tm, FN_COLS), lambda b, i: (b, blk0 + i, 0)),
                  _full(wc.shape)],
        out_specs=pl.BlockSpec((1, 2, tm, FN_COLS), lambda b, i: (b, 0, i, 0)),
        out_shape=jax.ShapeDtypeStruct((nb, 2, n, FN_COLS), F32),
        compiler_params=_params("parallel", "parallel"),
        name="fourier_chan",
    )(fcat, wc)
    bb = min(8, bn)
    tc = bb * FN_COLS
    t2 = pl.pallas_call(
        functools.partial(_fourier_stage1_kernel, a=a, bb=bb),
        grid=(bn // bb, nb),
        in_specs=[pl.BlockSpec((1, 2 * a, tc), lambda j, b: (b, 0, j)),
                  _full(m1.shape),
                  pl.BlockSpec((a, tc), lambda j, b: (0, j)),
                  pl.BlockSpec((a, tc), lambda j, b: (0, j))],
        out_specs=pl.BlockSpec((1, 2, bb, a, FN_COLS), lambda j, b: (b, 0, j, 0, 0)),
        out_shape=jax.ShapeDtypeStruct((nb, 2, bn, a, FN_COLS), F32),
        compiler_params=_params("parallel", "parallel"),
        name="fourier_stage1",
    )(g.reshape(nb, 2 * a, bn * FN_COLS), m1, tw_c, tw_s)
    assert ntot % a == 0 and (row0 // a) % bn == 0
    tc2 = min(8, a) * FN_COLS
    out_rows = ntot // a
    in_specs = [pl.BlockSpec((1, 2 * bn, tc2), lambda b, j: (b, 0, j)), _full(m3.shape)]
    args = [t2.reshape(nb, 2 * bn, a * FN_COLS), m3]
    aliases = {}
    if out_buf is not None:
        in_specs.append(pl.BlockSpec(memory_space=pl.ANY))
        args.append(out_buf.reshape(nb, out_rows, a * FN_COLS))
        aliases = {2: 0}
    rblk = (row0 // a) // bn
    out = pl.pallas_call(
        _fourier_stage2_kernel,
        grid=(nb, a * FN_COLS // tc2),
        in_specs=in_specs,
        out_specs=pl.BlockSpec((1, bn, tc2), lambda b, j: (b, rblk, j)),
        out_shape=jax.ShapeDtypeStruct((nb, out_rows, a * FN_COLS), F32),
        input_output_aliases=aliases,
        compiler_params=_params("parallel", "parallel"),
        name="fourier_stage2",
    )(*args)
    return out.reshape(nb, ntot, FN_COLS)


def _post_norm(h, y, g, b):
    return _ln(ALPHA * h + y) * g + b


def _out_even_kernel(a_ref, f_ref, h_ref, mod_ref, w_ref, g_ref, b_ref, o_ref):
    na = a_ref.shape[-1]
    y = _dot(a_ref[0], w_ref[:na, :]) + _dot(f_ref[0].astype(BF16), w_ref[na:, :])
    o_ref[0] = _post_norm(h_ref[0], mod_ref[0, 2:3, :] * y, g_ref[...], b_ref[...])


def _out_even(a_n, fm, hcat, mod, w_out, ln_g, ln_b, n_lat):
    nb, ntot, d = hcat.shape
    tm = TOKEN_TILE
    na = a_n.shape[-1]
    tok = lambda c: pl.BlockSpec((1, tm, c), lambda b, i: (b, i, 0))
    return pl.pallas_call(
        _out_even_kernel,
        grid=(nb, ntot // tm),
        in_specs=[tok(na), tok(FN_COLS), tok(d), _mod_spec(nb, n_lat // tm, d),
                  _full(w_out.shape), _full((1, d)), _full((1, d))],
        out_specs=tok(d),
        out_shape=jax.ShapeDtypeStruct((nb, ntot, d), F32),
        compiler_params=_params("parallel", "parallel"),
        name="out_even",
    )(a_n, fm, hcat, mod, w_out, ln_g, ln_b)


def _ffn_kernel(h_ref, mod_ref, w1_ref, w3_ref, w2_ref, g_ref, b_ref, o_ref):
    h = h_ref[0]
    u = (_ln(h) * (1.0 + mod_ref[0, 4:5, :]) + mod_ref[0, 3:4, :]).astype(BF16)
    hid = (_silu(_dot(u, w1_ref[...])) * _dot(u, w3_ref[...])).astype(BF16)
    y = _dot(hid, w2_ref[...])
    o_ref[0] = _post_norm(h, mod_ref[0, 5:6, :] * y, g_ref[...], b_ref[...])


def _ffn(h, n_rows, mod, w1, w3, w2, ln_g, ln_b, n_lat):
    nb, _, d = h.shape
    tm = TOKEN_TILE
    tok = pl.BlockSpec((1, tm, d), lambda b, i: (b, i, 0))
    resident = lambda s: pl.BlockSpec(s, lambda b, i: (0, 0), pipeline_mode=pl.Buffered(1))
    return pl.pallas_call(
        _ffn_kernel,
        grid=(nb, n_rows // tm),
        in_specs=[tok, _mod_spec(nb, n_lat // tm, d),
                  resident(w1.shape), resident(w3.shape), resident(w2.shape),
                  _full((1, d)), _full((1, d))],
        out_specs=tok,
        out_shape=jax.ShapeDtypeStruct((nb, n_rows, d), F32),
        compiler_params=_params("parallel", "parallel"),
        name="ffn",
    )(h, mod, w1, w3, w2, ln_g, ln_b)


def _proj_odd_kernel(h_ref, mod_ref, w_ref, xm_ref, z_ref):
    x = h_ref[0]
    u = (_ln(x) * (1.0 + mod_ref[0, 1:2, :]) + mod_ref[0, 0:1, :]).astype(BF16)
    p = _dot(u, w_ref[...])
    inner = xm_ref.shape[-1]
    xm_ref[0] = p[:, :inner]
    z_ref[0] = p[:, inner:]


def _proj_odd(hcat, mod, w, n_lat):
    nb, ntot, d = hcat.shape
    tm = TOKEN_TILE
    inner = w.shape[1] // 2
    out_spec = pl.BlockSpec((1, tm, inner), lambda b, i: (b, i, 0))
    out_shape = jax.ShapeDtypeStruct((nb, ntot, inner), F32)
    return pl.pallas_call(
        _proj_odd_kernel,
        grid=(nb, ntot // tm),
        in_specs=[pl.BlockSpec((1, tm, d), lambda b, i: (b, i, 0)),
                  _mod_spec(nb, n_lat // tm, d),
                  _full(w.shape)],
        out_specs=[out_spec, out_spec],
        out_shape=[out_shape, out_shape],
        compiler_params=_params("parallel", "parallel"),
        name="proj_odd",
    )(hcat, mod, w)


def _mfeat_kernel(xm_ref, cw_ref, cb_ref, wq_ref, wk_ref, wqt_ref, wvt_ref, wv_ref, wg_ref,
                  xc_ref, k_ref, qt_ref, vt_ref, gr_ref, *, n_lat, chunk, row_group):
    ntot = xm_ref.shape[1]
    ct = CONV_TILE
    half = ML_CONV_W // 2
    cw = cw_ref[...]
    cb = cb_ref[...]

    def conv_tile(t, carry):
        r0 = pl.multiple_of(t * ct, ct)
        at_start = jnp.logical_or(r0 == 0, r0 == n_lat)
        at_end = jnp.logical_or(r0 + ct == n_lat, r0 + ct == ntot)
        p0 = pl.multiple_of(jnp.maximum(r0 - SUBLANES, 0), SUBLANES)
        n0 = pl.multiple_of(jnp.minimum(r0 + ct, ntot - SUBLANES), SUBLANES)
        prev = xm_ref[0, pl.ds(p0, SUBLANES), :]
        nxt = xm_ref[0, pl.ds(n0, SUBLANES), :]
        prev = jnp.where(at_start, 0.0, prev)
        nxt = jnp.where(at_end, 0.0, nxt)
        xe = jnp.concatenate([prev, xm_ref[0, pl.ds(r0, ct), :], nxt], axis=0)
        acc = cb
        for dd in range(ML_CONV_W):
            lo = SUBLANES - half + dd
            acc = acc + cw[dd:dd + 1, :] * xe[lo:lo + ct, :]
        xc_ref[0, pl.ds(r0, ct), :] = _silu(acc)
        return carry

    lax.fori_loop(0, ntot // ct, conv_tile, 0)

    for g in range(ntot // row_group):
        rows = pl.ds(g * row_group, row_group)
        xcb = xc_ref[0, rows, :].astype(BF16)
        xmb = xm_ref[0, rows, :].astype(BF16)
        qb = _dot(xcb, wq_ref[0]).astype(BF16)
        kb = _dot(xcb, wk_ref[0]).astype(BF16)
        vb = _dot(xmb, wv_ref[0]).astype(BF16)
        k_ref[0, rows, :] = kb
        gr_ref[0, 0, :, rows] = _dot_nt(wg_ref[0], jnp.concatenate([qb, kb, vb], axis=1))

    def t_chunk(c, carry):
        r0 = pl.multiple_of(c * chunk, chunk)
        qt_ref[0, 0, c] = _dot_nt(wqt_ref[0], xc_ref[0, pl.ds(r0, chunk), :].astype(BF16)).astype(BF16)
        vt_ref[0, 0, c] = _dot_nt(wvt_ref[0], xm_ref[0, pl.ds(r0, chunk), :].astype(BF16)).astype(BF16)
        return carry

    lax.fori_loop(0, ntot // chunk, t_chunk, 0)


def _mfeat(xm, conv_w, conv_b, wq, wk, wqt, wvt, wv, wg, n_lat, chunk):
    nb, ntot, inner = xm.shape
    nh = ML_HEADS
    dh = inner // nh
    ng = 2 if ntot % (2 * LANES) == 0 else 1
    kern = functools.partial(_mfeat_kernel, n_lat=n_lat, chunk=chunk, row_group=ntot // ng)
    seq = pl.BlockSpec((1, ntot, dh), lambda b, h: (b, 0, h))
    wblk = pl.BlockSpec((1, dh, dh), lambda b, h: (h, 0, 0))
    tspec = pl.BlockSpec((1, 1, ntot // chunk, dh, chunk), lambda b, h: (b, h, 0, 0, 0))
    tshape = jax.ShapeDtypeStruct((nb, nh, ntot // chunk, dh, chunk), BF16)
    n_gates = wg.shape[1]
    return pl.pallas_call(
        kern,
        grid=(nb, nh),
        in_specs=[seq,
                  pl.BlockSpec((ML_CONV_W, dh), lambda b, h: (0, h)),
                  pl.BlockSpec((1, dh), lambda b, h: (0, h)),
                  wblk, wblk, wblk, wblk, wblk,
                  pl.BlockSpec((1, n_gates, 3 * dh), lambda b, h: (h, 0, 0))],
        out_specs=[seq, seq, tspec, tspec,
                   pl.BlockSpec((1, 1, n_gates, ntot), lambda b, h: (b, h, 0, 0))],
        out_shape=[jax.ShapeDtypeStruct((nb, ntot, inner), F32),
                   jax.ShapeDtypeStruct((nb, ntot, inner), BF16), tshape, tshape,
                   jax.ShapeDtypeStruct((nb, nh, n_gates, ntot), F32)],
        compiler_params=_params("parallel", "parallel"),
        name="mlstm_features",
    )(xm, conv_w, conv_b, wq, wk, wqt, wvt, wv, wg)


def _split3(x):
    hi = x.astype(BF16)
    r1 = x - hi.astype(F32)
    mid = r1.astype(BF16)
    lo = (r1 - mid.astype(F32)).astype(BF16)
    return hi, mid, lo


def _running_max(x, reverse):
    n = x.shape[1]
    lane = lax.broadcasted_iota(jnp.int32, x.shape, 1)
    k = 1
    while k < n:
        if reverse:
            shifted = jnp.where(lane < n - k, pltpu.roll(x, n - k, 1), -jnp.inf)
        else:
            shifted = jnp.where(lane >= k, pltpu.roll(x, k, 1), -jnp.inf)
        x = jnp.maximum(x, shifted)
        k *= 2
    return x


def _gates_kernel(g_ref, b_ref, rows_ref, cols_ref):
    nh = ML_HEADS
    chunk = g_ref.shape[-1]
    pre = jnp.sum(g_ref[0], axis=0) + b_ref[...]
    log_i = pre[:2 * nh]
    xf = pre[2 * nh:]
    log_f = jnp.minimum(xf, 0.0) - jnp.log1p(jnp.exp(-jnp.abs(xf)))
    s_idx = lax.broadcasted_iota(jnp.int32, (chunk, chunk), 0)
    t_idx = lax.broadcasted_iota(jnp.int32, (chunk, chunk), 1)
    tri_f = jnp.where(s_idx <= t_idx, 1.0, 0.0).astype(BF16)
    tri_b = jnp.where(s_idx >= t_idx, 1.0, 0.0).astype(BF16)
    cum_f = sum(_dot(p, tri_f) for p in _split3(log_f[:nh]))
    cum_b = sum(_dot(p, tri_b) for p in _split3(log_f[nh:]))
    gap_f = log_i[:nh] - cum_f
    gap_b = log_i[nh:] - cum_b
    gap = jnp.concatenate([gap_f, gap_b], axis=0)
    top = jnp.concatenate([_running_max(gap_f, False), _running_max(gap_b, True)], axis=0)
    rows_ref[0, 0] = jnp.concatenate([gap, top, cum_f, cum_b], axis=0)
    pad = jnp.zeros((LANES - 2 * nh, chunk), F32)
    cols_ref[0] = jnp.transpose(jnp.concatenate([gap, pad], axis=0))[:, :2 * nh]


def _gates(g_part, bias, chunk):
    nb, nh, n_gates, ntot = g_part.shape
    nct = ntot // chunk
    return pl.pallas_call(
        _gates_kernel,
        grid=(nb, nct),
        in_specs=[pl.BlockSpec((1, nh, n_gates, chunk), lambda b, c: (b, 0, 0, c)),
                  _full(bias.shape)],
        out_specs=[pl.BlockSpec((1, 1, 6 * nh, chunk), lambda b, c: (b, c, 0, 0)),
                   pl.BlockSpec((1, chunk, 2 * nh), lambda b, c: (b, c, 0))],
        out_shape=[jax.ShapeDtypeStruct((nb, nct, 6 * nh, chunk), F32),
                   jax.ShapeDtypeStruct((nb, ntot, 2 * nh), F32)],
        compiler_params=_params("parallel", "parallel"),
        name="mlstm_gates",
    )(g_part, bias)


STATE_PAD_ROWS = 16


def _scan_kernel(k_ref, qt_ref, vt_ref, rows_ref, cols_ref, hs_ref, c_sc, n_sc, m_sc,
                 *, n_lat_chunks, n_ctx_chunks, chunk):
    nh = ML_HEADS
    h = pl.program_id(1)
    dh = c_sc.shape[-1]
    c_sc[...] = jnp.zeros(c_sc.shape, F32)
    n_sc[...] = jnp.zeros(n_sc.shape, F32)
    m_sc[...] = jnp.zeros(m_sc.shape, F32)
    s_idx = lax.broadcasted_iota(jnp.int32, (chunk, chunk), 0)
    t_idx = lax.broadcasted_iota(jnp.int32, (chunk, chunk), 1)
    gate_lane = lax.broadcasted_iota(jnp.int32, (1, 2 * nh), 1)

    def step(d, cidx, with_out):
        off = pl.multiple_of(cidx * chunk, chunk)
        kc = k_ref[0, pl.ds(off, chunk), :]
        qt = qt_ref[0, 0, cidx]
        vt = vt_ref[0, 0, cidx]
        gap_r = rows_ref[0, cidx, pl.ds(d * nh + h, 1), :]
        top_r = rows_ref[0, cidx, pl.ds((2 + d) * nh + h, 1), :]
        cum_r = rows_ref[0, cidx, pl.ds((4 + d) * nh + h, 1), :]
        m = m_sc[d]
        ct = c_sc[d]
        nrow = n_sc[d]
        last = chunk - 1 if d == 0 else 0
        b_end = cum_r[:, last:last + 1]
        top_end = top_r[:, last:last + 1]
        if with_out:
            ctile = cols_ref[0, pl.ds(off, chunk), :]
            gap_c = jnp.sum(jnp.where(gate_lane == d * nh + h, ctile, 0.0), axis=1, keepdims=True)
            lift = jnp.maximum(top_r, m)
            ordered = (s_idx <= t_idx) if d == 0 else (s_idx >= t_idx)
            dw_t = jnp.exp(jnp.where(ordered, gap_c - lift, -jnp.inf))
            iw = jnp.exp(m - lift)
            nb16 = jnp.broadcast_to(nrow.astype(BF16), (STATE_PAD_ROWS, dh))
            a = _dot(jnp.concatenate([kc, ct.astype(BF16), nb16], axis=0), qt)
            s_t = a[:chunk] * dw_t
            num_t = iw * a[chunk:chunk + dh] + _dot(vt, s_t.astype(BF16))
            den = iw * a[chunk + dh:chunk + dh + 1] + jnp.sum(s_t, axis=0, keepdims=True)
            floor = jnp.exp(-(cum_r + lift))
            hs_ref[d, 0, 0, cidx] = num_t * (1.0 / jnp.maximum(jnp.abs(den), floor))
        mx = jnp.maximum(m, top_end)
        keep = jnp.exp(m - mx)
        w_r = jnp.exp(gap_r - mx)
        vw = (vt.astype(F32) * w_r).astype(BF16)
        wb16 = jnp.broadcast_to(w_r.astype(BF16), (STATE_PAD_ROWS, chunk))
        upd = _dot(jnp.concatenate([vw, wb16], axis=0), kc)
        c_sc[d] = keep * ct + upd[:dh]
        n_sc[d] = keep * nrow + upd[dh:dh + 1]
        m_sc[d] = b_end + mx

    def ctx_body(c, carry):
        step(0, n_lat_chunks + c, False)
        step(1, n_lat_chunks + n_ctx_chunks - 1 - c, False)
        return carry

    def lat_body(c, carry):
        step(0, c, True)
        step(1, n_lat_chunks - 1 - c, True)
        return carry

    lax.fori_loop(0, n_ctx_chunks, ctx_body, 0)
    lax.fori_loop(0, n_lat_chunks, lat_body, 0, unroll=2)


def _scan(k, qt, vt, rows, cols, n_lat, chunk):
    nb, ntot, inner = k.shape
    nh = ML_HEADS
    dh = inner // nh
    nct = ntot // chunk
    ncl = n_lat // chunk
    kern = functools.partial(_scan_kernel, n_lat_chunks=ncl, n_ctx_chunks=nct - ncl, chunk=chunk)
    tspec = pl.BlockSpec((1, 1, nct, dh, chunk), lambda b, h: (b, h, 0, 0, 0))
    return pl.pallas_call(
        kern,
        grid=(nb, nh),
        in_specs=[pl.BlockSpec((1, ntot, dh), lambda b, h: (b, 0, h)), tspec, tspec,
                  pl.BlockSpec((1, nct, rows.shape[2], chunk), lambda b, h: (b, 0, 0, 0)),
                  pl.BlockSpec((1, ntot, cols.shape[2]), lambda b, h: (b, 0, 0))],
        out_specs=pl.BlockSpec((2, 1, 1, ncl, dh, chunk), lambda b, h: (0, b, h, 0, 0, 0)),
        out_shape=jax.ShapeDtypeStruct((2, nb, nh, ncl, dh, chunk), F32),
        scratch_shapes=[pltpu.VMEM((2, dh, dh), F32), pltpu.VMEM((2, 1, dh), F32),
                        pltpu.VMEM((2, 1, 1), F32)],
        compiler_params=_params("parallel", "parallel"),
        name="mlstm_scan",
    )(k, qt, vt, rows, cols)


def _out_odd_kernel(hs_ref, xc_ref, z_ref, h_ref, mod_ref, skip_ref, hg_ref, w_ref, g_ref, b_ref, o_ref):
    parts = []
    for hd in range(ML_HEADS):
        ht = hs_ref[0, 0, hd, 0] + hs_ref[1, 0, hd, 0]
        mu = jnp.mean(ht, axis=0, keepdims=True)
        hc = ht - mu
        var = jnp.mean(hc * hc, axis=0, keepdims=True)
        parts.append(jnp.transpose(hc * lax.rsqrt(var + LN_EPS)))
    hn = jnp.concatenate(parts, axis=1)
    y = (hn * hg_ref[...] + skip_ref[...] * xc_ref[0]) * _silu(z_ref[0])
    yo = _dot(y.astype(BF16), w_ref[...])
    o_ref[0] = _post_norm(h_ref[0], mod_ref[0, 2:3, :] * yo, g_ref[...], b_ref[...])


def _out_odd(hs, xc, z, hcat, mod, skip, head_g, w_out, ln_g, ln_b, n_lat):
    nb, _, d = hcat.shape
    inner = xc.shape[-1]
    _, _, nh, _, dh, tm = hs.shape
    tok = lambda c: pl.BlockSpec((1, tm, c), lambda b, i: (b, i, 0))
    return pl.pallas_call(
        _out_odd_kernel,
        grid=(nb, n_lat // tm),
        in_specs=[pl.BlockSpec((2, 1, nh, 1, dh, tm), lambda b, i: (0, b, 0, i, 0, 0)),
                  tok(inner), tok(inner), tok(d),
                  pl.BlockSpec((1, 6, d), lambda b, i: (b, 0, 0)),
                  _full((1, inner)), _full((1, inner)), _full(w_out.shape),
                  _full((1, d)), _full((1, d))],
        out_specs=tok(d),
        out_shape=jax.ShapeDtypeStruct((nb, n_lat, d), F32),
        compiler_params=_params("parallel", "parallel"),
        name="out_odd",
    )(hs, xc, z, hcat, mod, skip, head_g, w_out, ln_g, ln_b)


def _even_in_columns():
    cols, scale = [], []
    quarter = DA_DK // 2
    for hd in range(DA_HEADS):
        base = hd * DA_HEAD_COLS
        for blk in range(2):
            b0 = base + blk * 2 * DA_DK
            for half in range(2):
                for m in range(2):
                    cols += [b0 + m * DA_DK + half * quarter + j for j in range(quarter)]
            scale += [DA_DK ** -0.5 if blk == 0 else 1.0] * (2 * DA_DK)
        cols += list(range(base + 4 * DA_DK, base + DA_HEAD_COLS))
        scale += [1.0] * DA_DV
    cols += list(range(DA_HEADS * DA_HEAD_COLS, DA_HEADS * DA_HEAD_COLS + FN_COLS))
    scale += [1.0] * FN_COLS
    return np.asarray(cols, np.int32), np.asarray(scale, np.float32)


def _rope_tables(n_lat, ntot):
    rows = n_lat // GRID_W
    row = jnp.repeat(jnp.arange(rows, dtype=F32), GRID_W)
    col = jnp.tile(jnp.arange(GRID_W, dtype=F32), rows)
    n_freq = DA_DK // 4
    inv_freq = ROPE_BASE ** (-jnp.arange(n_freq, dtype=F32) / n_freq)
    ang = jnp.concatenate([row[:, None] * inv_freq, col[:, None] * inv_freq], -1)
    cos, sin = jnp.cos(ang), jnp.sin(ang)
    c = jnp.concatenate([cos, cos, cos, cos], axis=1)
    s = jnp.concatenate([-sin, -sin, sin, sin], axis=1)
    n_ctx = ntot - n_lat
    c = jnp.concatenate([c, jnp.ones((n_ctx, LANES), F32)], axis=0)
    s = jnp.concatenate([s, jnp.zeros((n_ctx, LANES), F32)], axis=0)
    return c, s


def _block_diag_heads(w, dh):
    nblk = w.shape[0]
    per_head = dh // ML_BLOCK
    wh = w.reshape(nblk // per_head, per_head, ML_BLOCK, ML_BLOCK)
    eye = jnp.eye(per_head, dtype=w.dtype)
    dense = jnp.einsum('hgij,gk->hgikj', wh, eye)
    return dense.reshape(nblk // per_head, dh, dh)


def kernel(x, c, ctx, c_ctx, w_mod, b_mod, ln_g, ln_b, w_ff1, w_ff3, w_ff2, a_w_in, a_w_out, da_lq1, da_lk1, da_lq2, da_lk2, da_head_g, m_w_in, m_w_out, m_conv_w, m_conv_b, m_wq, m_wk, m_wv, m_w_ig, m_b_ig, m_w_fg, m_b_fg, m_skip, m_head_g):
    nb, n_lat, d = x.shape
    n_ctx = ctx.shape[1]
    ntot = n_lat + n_ctx
    assert w_mod.shape[0] == DEPTH == 2
    assert n_lat % TOKEN_TILE == 0 and n_ctx % TOKEN_TILE == 0 and n_lat % n_ctx == 0 and n_lat % GRID_W == 0

    r = -(-(nb + 1) // MOD_ROWS_PAD) * MOD_ROWS_PAD
    cvec = jnp.concatenate([c, c_ctx[None, :], jnp.zeros((r - nb - 1, d), F32)], axis=0)
    mod_all = _modulation(cvec, w_mod, b_mod).reshape(DEPTH, r, 6, d)

    hcat = jnp.concatenate([x, ctx], axis=1)

    lam_init0 = 0.8 - 0.6 * math.exp(-0.3 * 0)
    cols, colscale = _even_in_columns()
    w_in = (a_w_in[0][:, cols] * colscale[None, :]).astype(BF16)
    rope_c, rope_s = _rope_tables(n_lat, ntot)
    q, k, v, f = _proj_even(hcat, mod_all[0], w_in, rope_c, rope_s, n_lat)
    lam_params = jnp.stack([da_lq1[0], da_lk1[0], da_lq2[0], da_lk2[0]], axis=0)
    a_n = _attention(q, k, v, lam_params, da_head_g[0][None, :], n_lat, lam_init0)
    fm = _fourier_mix(f, 0, n_lat, None)
    fm = _fourier_mix(f, n_lat, n_ctx, fm)
    h1 = _out_even(a_n, fm, hcat, mod_all[0], a_w_out[0].astype(BF16),
                   ln_g[0, 0][None, :], ln_b[0, 0][None, :], n_lat)
    hcat = _ffn(h1, ntot, mod_all[0], w_ff1[0].astype(BF16), w_ff3[0].astype(BF16), w_ff2[0].astype(BF16),
                ln_g[0, 1][None, :], ln_b[0, 1][None, :], n_lat)

    inner = m_w_in.shape[2] // 2
    dh = inner // ML_HEADS
    chunk = MXU_DIM if (n_lat % MXU_DIM == 0 and n_ctx % MXU_DIM == 0) else LANES
    xm, z = _proj_odd(hcat, mod_all[1], m_w_in[0].astype(BF16), n_lat)
    wq_f = _block_diag_heads(m_wq[0], dh)
    wk_f = _block_diag_heads(m_wk[0], dh) * (dh ** -0.5)
    wv_f = _block_diag_heads(m_wv[0], dh)
    wq, wk, wv = wq_f.astype(BF16), wk_f.astype(BF16), wv_f.astype(BF16)
    wqt = jnp.swapaxes(wq_f, 1, 2).astype(BF16)
    wvt = jnp.swapaxes(wv_f, 1, 2).astype(BF16)
    wg_all = jnp.concatenate([m_w_ig[0, 0], m_w_ig[0, 1], m_w_fg[0, 0], m_w_fg[0, 1]], axis=1)
    wg_all = wg_all.reshape(3, ML_HEADS, dh, 4 * ML_HEADS) * jnp.asarray([1.0, dh ** 0.5, 1.0], F32)[:, None, None, None]
    wg = jnp.transpose(wg_all, (1, 3, 0, 2)).reshape(ML_HEADS, 4 * ML_HEADS, 3 * dh).astype(BF16)
    g_bias = jnp.concatenate([m_b_ig[0, 0], m_b_ig[0, 1], m_b_fg[0, 0], m_b_fg[0, 1]])[:, None]
    xc, km, qt, vt, g_part = _mfeat(xm, m_conv_w[0], m_conv_b[0][None, :], wq, wk, wqt, wvt, wv, wg, n_lat, chunk)
    rows, colsg = _gates(g_part, g_bias, chunk)
    hs = _scan(km, qt, vt, rows, colsg, n_lat, chunk)
    h1 = _out_odd(hs, xc, z, hcat, mod_all[1], m_skip[0][None, :], m_head_g[0][None, :],
                  m_w_out[0].astype(BF16), ln_g[1, 0][None, :], ln_b[1, 0][None, :], n_lat)
    return _ffn(h1, n_lat, mod_all[1], w_ff1[1].astype(BF16), w_ff3[1].astype(BF16), w_ff2[1].astype(BF16),
                ln_g[1, 1][None, :], ln_b[1, 1][None, :], n_lat)
```

```python
import functools
import math

import numpy as np
import jax
import jax.numpy as jnp
from jax import lax
from jax.experimental import pallas as pl
from jax.experimental.pallas import tpu as pltpu

F32 = jnp.float32
BF16 = jnp.bfloat16

DA_HEADS = 6
DA_DK = 64
DA_DV = 2 * DA_DK
DA_HEAD_COLS = 4 * DA_DK + DA_DV
FN_GROUPS = 4
FN_CH = 64
FN_COLS = FN_GROUPS * FN_CH
ML_HEADS = 8
ML_BLOCK = 4
ML_CONV_W = 5
GRID_W = 64
ROPE_BASE = 10000.0
LN_EPS = 1e-5
DEPTH = 2
ALPHA = (2 * DEPTH) ** 0.25

LANES = 128
SUBLANES = 8
MXU_DIM = 256
VMEM_LIMIT_BYTES = 60 * 1024 * 1024

TOKEN_TILE = 256
CONV_TILE = 128
MOD_ROWS_PAD = 8
ATTN_Q_TILE = 512
SWEEP_UNROLL = 4


def _params(*sem):
    return pltpu.CompilerParams(dimension_semantics=sem, vmem_limit_bytes=VMEM_LIMIT_BYTES)


def _ln(x):
    mu = jnp.mean(x, axis=-1, keepdims=True)
    xc = x - mu
    var = jnp.mean(xc * xc, axis=-1, keepdims=True)
    return xc * lax.rsqrt(var + LN_EPS)


def _silu(x):
    return x * jax.nn.sigmoid(x)


def _dot(a, b):
    return jnp.dot(a, b, preferred_element_type=F32)


def _dot_nt(a, b):
    return lax.dot_general(a, b, (((1,), (1,)), ((), ())), preferred_element_type=F32)


def _full(shape):
    n = len(shape)
    return pl.BlockSpec(shape, lambda *_: (0,) * n)


def _mod_kernel(c_ref, w_ref, b_ref, o_ref):
    s = _silu(c_ref[...]).astype(BF16)
    o_ref[0] = _dot(s, w_ref[0].astype(BF16)) + b_ref[0]


def _modulation(cvec, w_mod, b_mod):
    depth, d, d6 = w_mod.shape
    r = cvec.shape[0]
    tn = d6 // 4
    return pl.pallas_call(
        _mod_kernel,
        grid=(depth, d6 // tn),
        in_specs=[pl.BlockSpec((r, d), lambda l, j: (0, 0)),
                  pl.BlockSpec((1, d, tn), lambda l, j: (l, 0, j)),
                  pl.BlockSpec((1, 1, tn), lambda l, j: (l, 0, j))],
        out_specs=pl.BlockSpec((1, r, tn), lambda l, j: (l, 0, j)),
        out_shape=jax.ShapeDtypeStruct((depth, r, d6), F32),
        compiler_params=_params("parallel", "parallel"),
        name="modulation",
    )(cvec, w_mod, b_mod.reshape(depth, 1, d6))


def _mod_spec(nb, n_lat_tiles, d):
    return pl.BlockSpec((1, 6, d), lambda b, i: (jnp.where(i < n_lat_tiles, b, nb), 0, 0))


def _proj_even_kernel(h_ref, mod_ref, w_ref, c_ref, s_ref, q_ref, k_ref, v_ref, f_ref):
    x = h_ref[0]
    u = (_ln(x) * (1.0 + mod_ref[0, 1:2, :]) + mod_ref[0, 0:1, :]).astype(BF16)
    p = _dot(u, w_ref[...])
    c = c_ref[...]
    s = s_ref[...]
    blk = DA_HEAD_COLS
    for hd in range(DA_HEADS):
        pq = p[:, hd * blk: hd * blk + LANES]
        pk = p[:, hd * blk + LANES: hd * blk + 2 * LANES]
        q_ref[0, hd] = (pq * c + pltpu.roll(pq, LANES // 2, 1) * s).astype(BF16)
        k_ref[0, hd] = (pk * c + pltpu.roll(pk, LANES // 2, 1) * s).astype(BF16)
        v_ref[0, hd] = p[:, hd * blk + 2 * LANES: (hd + 1) * blk].astype(BF16)
    f_ref[0] = p[:, DA_HEADS * blk:]


def _proj_even(hcat, mod, w, rope_c, rope_s, n_lat):
    nb, ntot, d = hcat.shape
    tm = TOKEN_TILE
    nt = ntot // tm
    ncols = w.shape[1]
    qkv_shape = jax.ShapeDtypeStruct((nb, DA_HEADS, ntot, LANES), BF16)
    qkv_spec = pl.BlockSpec((1, DA_HEADS, tm, LANES), lambda b, i: (b, 0, i, 0))
    return pl.pallas_call(
        _proj_even_kernel,
        grid=(nb, nt),
        in_specs=[pl.BlockSpec((1, tm, d), lambda b, i: (b, i, 0)),
                  _mod_spec(nb, n_lat // tm, d),
                  _full((d, ncols)),
                  pl.BlockSpec((tm, LANES), lambda b, i: (i, 0)),
                  pl.BlockSpec((tm, LANES), lambda b, i: (i, 0))],
        out_specs=[qkv_spec, qkv_spec, qkv_spec,
                   pl.BlockSpec((1, tm, FN_COLS), lambda b, i: (b, i, 0))],
        out_shape=[qkv_shape, qkv_shape, qkv_shape,
                   jax.ShapeDtypeStruct((nb, ntot, FN_COLS), F32)],
        compiler_params=_params("parallel", "parallel"),
        name="proj_even",
    )(hcat, mod, w, rope_c, rope_s)


def _stack_maps(q):
    qf = q.astype(F32)
    lane = lax.broadcasted_iota(jnp.int32, (1, LANES), 1)
    map1 = (lane % (LANES // 2)) < (LANES // 4)
    return jnp.concatenate([jnp.where(map1, qf, 0.0), jnp.where(map1, 0.0, qf)], axis=0).astype(BF16)


def _lane_block_max(s):
    blk = s[:, :LANES]
    for c in range(1, s.shape[1] // LANES):
        blk = jnp.maximum(blk, s[:, c * LANES:(c + 1) * LANES])
    return blk


def _exp_blocks(s, mb):
    return [jnp.exp(s[:, c * LANES:(c + 1) * LANES] - mb) for c in range(s.shape[1] // LANES)]


def _diff_merge(o, lp, head_g, lam_init):
    tq = o.shape[0] // 2
    lam = (jnp.exp(jnp.sum(lp[0:1] * lp[1:2], keepdims=True))
           - jnp.exp(jnp.sum(lp[2:3] * lp[3:4], keepdims=True)) + lam_init)
    a = o[:tq] - lam * o[tq:]
    a_n = a * lax.rsqrt(jnp.mean(a * a, axis=-1, keepdims=True) + LN_EPS) * (head_g * (1.0 - lam_init))
    return a_n.astype(BF16)


def _attn_kernel(lp_ref, g_ref, qc_ref, qn_ref, k_ref, v_ref, o_ref, sa_sc, sb_sc, mcur_sc, mnxt_sc, l_sc, acc_sc,
                 *, tk, nkt, lam_init):
    i = pl.program_id(2)
    rows = mcur_sc.shape[0]

    def score_chunk(qq, s_ref, j):
        off = pl.multiple_of(j * tk, tk)
        s = _dot_nt(qq, k_ref[0, 0, pl.ds(off, tk), :])
        s_ref[j] = s
        mnxt_sc[...] = jnp.maximum(mnxt_sc[...], _lane_block_max(s))

    def row_max():
        return jnp.broadcast_to(jnp.max(mnxt_sc[...], axis=1, keepdims=True), (rows, LANES))

    @pl.when(i == 0)
    def _():
        qq0 = _stack_maps(qc_ref[0, 0])
        mnxt_sc[...] = jnp.full(mnxt_sc.shape, -jnp.inf, F32)

        def first(j, carry):
            score_chunk(qq0, sa_sc, j)
            return carry

        lax.fori_loop(0, nkt, first, 0)
        mcur_sc[...] = row_max()

    qqn = _stack_maps(qn_ref[0, 0])
    mnxt_sc[...] = jnp.full(mnxt_sc.shape, -jnp.inf, F32)
    l_sc[...] = jnp.zeros(l_sc.shape, F32)
    acc_sc[...] = jnp.zeros(acc_sc.shape, F32)

    def sweep(cur_ref, nxt_ref):
        def body(j, carry):
            score_chunk(qqn, nxt_ref, j)
            off = pl.multiple_of(j * tk, tk)
            ps = _exp_blocks(cur_ref[j], mcur_sc[...])
            part = ps[0]
            for pc in ps[1:]:
                part = part + pc
            l_sc[...] += part
            acc_sc[...] += _dot(jnp.concatenate(ps, axis=1).astype(BF16), v_ref[0, 0, pl.ds(off, tk), :])
            return carry

        lax.fori_loop(0, nkt, body, 0, unroll=SWEEP_UNROLL)

    @pl.when(i % 2 == 0)
    def _():
        sweep(sa_sc, sb_sc)

    @pl.when(i % 2 == 1)
    def _():
        sweep(sb_sc, sa_sc)

    o = acc_sc[...] / jnp.sum(l_sc[...], axis=1, keepdims=True)
    o_ref[0] = _diff_merge(o, lp_ref[...], g_ref[...], lam_init)
    mcur_sc[...] = row_max()


def _attn_ctx_kernel(lp_ref, g_ref, q_ref, k_ref, v_ref, buf_ref, o_ref, *, lam_init):
    del buf_ref
    s = _dot_nt(_stack_maps(q_ref[0, 0]), k_ref[0, 0])
    p = jnp.exp(s - jnp.max(s, axis=1, keepdims=True))
    o = _dot(p.astype(BF16), v_ref[0, 0]) / jnp.sum(p, axis=1, keepdims=True)
    o_ref[0] = _diff_merge(o, lp_ref[...], g_ref[...], lam_init)


def _attention(q, k, v, lam_params, head_g, n_lat, lam_init):
    nb, nh, ntot, _ = q.shape
    n_ctx = ntot - n_lat
    tq = min(ATTN_Q_TILE, n_lat)
    tk = TOKEN_TILE
    nq = n_lat // tq
    nkt = ntot // tk
    rows = 2 * tq
    kv_spec = pl.BlockSpec((1, 1, ntot, LANES), lambda b, h, i: (b, h, 0, 0))
    a_lat = pl.pallas_call(
        functools.partial(_attn_kernel, tk=tk, nkt=nkt, lam_init=lam_init),
        grid=(nb, nh, nq),
        in_specs=[_full(lam_params.shape), _full(head_g.shape),
                  pl.BlockSpec((1, 1, tq, LANES), lambda b, h, i: (b, h, i, 0)),
                  pl.BlockSpec((1, 1, tq, LANES), lambda b, h, i: (b, h, jnp.minimum(i + 1, nq - 1), 0)),
                  kv_spec, kv_spec],
        out_specs=pl.BlockSpec((1, tq, LANES), lambda b, h, i: (b, i, h)),
        out_shape=jax.ShapeDtypeStruct((nb, ntot, nh * LANES), BF16),
        scratch_shapes=[pltpu.VMEM((nkt, rows, tk), F32)] * 2 + [pltpu.VMEM((rows, LANES), F32)] * 4,
        compiler_params=_params("parallel", "parallel", "arbitrary"),
        name="diff_attention",
    )(lam_params, head_g, q, q, k, v)
    cblk = n_lat // n_ctx
    ctx_spec = pl.BlockSpec((1, 1, n_ctx, LANES), lambda b, h: (b, h, cblk, 0))
    return pl.pallas_call(
        functools.partial(_attn_ctx_kernel, lam_init=lam_init),
        grid=(nb, nh),
        in_specs=[_full(lam_params.shape), _full(head_g.shape), ctx_spec, ctx_spec, ctx_spec,
                  pl.BlockSpec(memory_space=pl.ANY)],
        out_specs=pl.BlockSpec((1, n_ctx, LANES), lambda b, h: (b, cblk, h)),
        out_shape=jax.ShapeDtypeStruct(a_lat.shape, a_lat.dtype),
        input_output_aliases={5: 0},
        compiler_params=_params("parallel", "parallel"),
        name="diff_attention_ctx",
    )(lam_params, head_g, q, k, v, a_lat)


def _fourier_factor(n):
    a = 1 << (int(math.log2(n)) // 2)
    assert a * (n // a) == n and a % SUBLANES == 0 and (n // a) % SUBLANES == 0
    return a, n // a


def _fourier_tables(n):
    a, bn = _fourier_factor(n)
    j = np.arange(FN_CH)
    ang = 2.0 * np.pi * np.outer(j, j) / FN_CH
    eye = np.eye(FN_GROUPS)
    wc = np.concatenate([np.kron(eye, np.cos(ang)), -np.kron(eye, np.sin(ang))], axis=1)
    ia = np.arange(a)
    ang_a = 2.0 * np.pi * np.outer(ia, ia) / a
    ca, sa = np.cos(ang_a), np.sin(ang_a)
    m1 = np.block([[ca, sa], [-sa, ca]])
    ib = np.arange(bn)
    ang_t = 2.0 * np.pi * np.outer(ia, ib) / n
    ang_b = 2.0 * np.pi * np.outer(ib, ib) / bn
    m3 = np.concatenate([np.cos(ang_b), np.sin(ang_b)], axis=1) / math.sqrt(n * FN_CH)
    tw_c = jnp.repeat(jnp.asarray(np.cos(ang_t), F32), FN_COLS, axis=1)
    tw_s = jnp.repeat(jnp.asarray(np.sin(ang_t), F32), FN_COLS, axis=1)
    return (jnp.asarray(wc, BF16), jnp.asarray(m1, BF16), tw_c, tw_s, jnp.asarray(m3, BF16))


def _fourier_chan_kernel(x_ref, w_ref, o_ref):
    g = _dot(x_ref[0].astype(BF16), w_ref[...])
    o_ref[0, 0] = g[:, :FN_COLS]
    o_ref[0, 1] = g[:, FN_COLS:]


def _fourier_stage1_kernel(g_ref, m_ref, c_ref, s_ref, o_ref, *, a, bb):
    t1 = _dot(m_ref[...], g_ref[0].astype(BF16))
    tr, ti = t1[:a], t1[a:]
    c = c_ref[...]
    s = s_ref[...]
    t2r = tr * c + ti * s
    t2i = ti * c - tr * s
    for b in range(bb):
        o_ref[0, 0, b] = t2r[:, b * FN_COLS:(b + 1) * FN_COLS]
        o_ref[0, 1, b] = t2i[:, b * FN_COLS:(b + 1) * FN_COLS]


def _fourier_stage2_kernel(t_ref, m_ref, *rest):
    o_ref = rest[-1]
    o_ref[0] = _dot(m_ref[...], t_ref[0].astype(BF16))


def _fourier_mix(fcat, row0, n, out_buf):
    nb, ntot, _ = fcat.shape
    a, bn = _fourier_factor(n)
    wc, m1, tw_c, tw_s, m3 = _fourier_tables(n)
    tm = min(TOKEN_TILE, n)
    blk0 = row0 // tm
    g = pl.pallas_call(
        _fourier_chan_kernel,
        grid=(nb, n // tm),
        in_specs=[pl.BlockSpec((1, tm, FN_COLS), lambda b, i: (b, blk0 + i, 0)),
                  _full(wc.shape)],
        out_specs=pl.BlockSpec((1, 2, tm, FN_COLS), lambda b, i: (b, 0, i, 0)),
        out_shape=jax.ShapeDtypeStruct((nb, 2, n, FN_COLS), F32),
        compiler_params=_params("parallel", "parallel"),
        name="fourier_chan",
    )(fcat, wc)
    bb = min(8, bn)
    tc = bb * FN_COLS
    t2 = pl.pallas_call(
        functools.partial(_fourier_stage1_kernel, a=a, bb=bb),
        grid=(bn // bb, nb),
        in_specs=[pl.BlockSpec((1, 2 * a, tc), lambda j, b: (b, 0, j)),
                  _full(m1.shape),
                  pl.BlockSpec((a, tc), lambda j, b: (0, j)),
                  pl.BlockSpec((a, tc), lambda j, b: (0, j))],
        out_specs=pl.BlockSpec((1, 2, bb, a, FN_COLS), lambda j, b: (b, 0, j, 0, 0)),
        out_shape=jax.ShapeDtypeStruct((nb, 2, bn, a, FN_COLS), F32),
        compiler_params=_params("parallel", "parallel"),
        name="fourier_stage1",
    )(g.reshape(nb, 2 * a, bn * FN_COLS), m1, tw_c, tw_s)
    assert ntot % a == 0 and (row0 // a) % bn == 0
    tc2 = min(8, a) * FN_COLS
    out_rows = ntot // a
    in_specs = [pl.BlockSpec((1, 2 * bn, tc2), lambda b, j: (b, 0, j)), _full(m3.shape)]
    args = [t2.reshape(nb, 2 * bn, a * FN_COLS), m3]
    aliases = {}
    if out_buf is not None:
        in_specs.append(pl.BlockSpec(memory_space=pl.ANY))
        args.append(out_buf.reshape(nb, out_rows, a * FN_COLS))
        aliases = {2: 0}
    rblk = (row0 // a) // bn
    out = pl.pallas_call(
        _fourier_stage2_kernel,
        grid=(nb, a * FN_COLS // tc2),
        in_specs=in_specs,
        out_specs=pl.BlockSpec((1, bn, tc2), lambda b, j: (b, rblk, j)),
        out_shape=jax.ShapeDtypeStruct((nb, out_rows, a * FN_COLS), F32),
        input_output_aliases=aliases,
        compiler_params=_params("parallel", "parallel"),
        name="fourier_stage2",
    )(*args)
    return out.reshape(nb, ntot, FN_COLS)


def _post_norm(h, y, g, b):
    return _ln(ALPHA * h + y) * g + b


def _out_even_kernel(a_ref, f_ref, h_ref, mod_ref, w_ref, g_ref, b_ref, o_ref):
    na = a_ref.shape[-1]
    y = _dot(a_ref[0], w_ref[:na, :]) + _dot(f_ref[0].astype(BF16), w_ref[na:, :])
    o_ref[0] = _post_norm(h_ref[0], mod_ref[0, 2:3, :] * y, g_ref[...], b_ref[...])


def _out_even(a_n, fm, hcat, mod, w_out, ln_g, ln_b, n_lat):
    nb, ntot, d = hcat.shape
    tm = TOKEN_TILE
    na = a_n.shape[-1]
    tok = lambda c: pl.BlockSpec((1, tm, c), lambda b, i: (b, i, 0))
    return pl.pallas_call(
        _out_even_kernel,
        grid=(nb, ntot // tm),
        in_specs=[tok(na), tok(FN_COLS), tok(d), _mod_spec(nb, n_lat // tm, d),
                  _full(w_out.shape), _full((1, d)), _full((1, d))],
        out_specs=tok(d),
        out_shape=jax.ShapeDtypeStruct((nb, ntot, d), F32),
        compiler_params=_params("parallel", "parallel"),
        name="out_even",
    )(a_n, fm, hcat, mod, w_out, ln_g, ln_b)


def _ffn_kernel(h_ref, mod_ref, w1_ref, w3_ref, w2_ref, g_ref, b_ref, o_ref):
    h = h_ref[0]
    u = (_ln(h) * (1.0 + mod_ref[0, 4:5, :]) + mod_ref[0, 3:4, :]).astype(BF16)
    hid = (_silu(_dot(u, w1_ref[...])) * _dot(u, w3_ref[...])).astype(BF16)
    y = _dot(hid, w2_ref[...])
    o_ref[0] = _post_norm(h, mod_ref[0, 5:6, :] * y, g_ref[...], b_ref[...])


def _ffn(h, n_rows, mod, w1, w3, w2, ln_g, ln_b, n_lat):
    nb, _, d = h.shape
    tm = TOKEN_TILE
    tok = pl.BlockSpec((1, tm, d), lambda b, i: (b, i, 0))
    resident = lambda s: pl.BlockSpec(s, lambda b, i: (0, 0), pipeline_mode=pl.Buffered(1))
    return pl.pallas_call(
        _ffn_kernel,
        grid=(nb, n_rows // tm),
        in_specs=[tok, _mod_spec(nb, n_lat // tm, d),
                  resident(w1.shape), resident(w3.shape), resident(w2.shape),
                  _full((1, d)), _full((1, d))],
        out_specs=tok,
        out_shape=jax.ShapeDtypeStruct((nb, n_rows, d), F32),
        compiler_params=_params("parallel", "parallel"),
        name="ffn",
    )(h, mod, w1, w3, w2, ln_g, ln_b)


def _proj_odd_kernel(h_ref, mod_ref, w_ref, xm_ref, z_ref):
    x = h_ref[0]
    u = (_ln(x) * (1.0 + mod_ref[0, 1:2, :]) + mod_ref[0, 0:1, :]).astype(BF16)
    p = _dot(u, w_ref[...])
    inner = xm_ref.shape[-1]
    xm_ref[0] = p[:, :inner]
    z_ref[0] = p[:, inner:]


def _proj_odd(hcat, mod, w, n_lat):
    nb, ntot, d = hcat.shape
    tm = TOKEN_TILE
    inner = w.shape[1] // 2
    out_spec = pl.BlockSpec((1, tm, inner), lambda b, i: (b, i, 0))
    out_shape = jax.ShapeDtypeStruct((nb, ntot, inner), F32)
    return pl.pallas_call(
        _proj_odd_kernel,
        grid=(nb, ntot // tm),
        in_specs=[pl.BlockSpec((1, tm, d), lambda b, i: (b, i, 0)),
                  _mod_spec(nb, n_lat // tm, d),
                  _full(w.shape)],
        out_specs=[out_spec, out_spec],
        out_shape=[out_shape, out_shape],
        compiler_params=_params("parallel", "parallel"),
        name="proj_odd",
    )(hcat, mod, w)


def _gate_fold_kernel(wg_ref, wq_ref, wk_ref, wv_ref, gc_ref, gm_ref):
    dh = wq_ref.shape[-1]
    wg = wg_ref[0]
    gc_ref[0] = (_dot_nt(wg[:, :dh], wq_ref[0]) + _dot_nt(wg[:, dh:2 * dh], wk_ref[0])).astype(BF16)
    gm_ref[0] = _dot_nt(wg[:, 2 * dh:], wv_ref[0]).astype(BF16)


def _gate_fold(wg, wq, wk, wv):
    nh, n_gates, _ = wg.shape
    dh = wq.shape[-1]
    wblk = pl.BlockSpec((1, dh, dh), lambda h: (h, 0, 0))
    gspec = pl.BlockSpec((1, n_gates, dh), lambda h: (h, 0, 0))
    gshape = jax.ShapeDtypeStruct((nh, n_gates, dh), BF16)
    return pl.pallas_call(
        _gate_fold_kernel,
        grid=(nh,),
        in_specs=[pl.BlockSpec((1, n_gates, 3 * dh), lambda h: (h, 0, 0)), wblk, wblk, wblk],
        out_specs=[gspec, gspec],
        out_shape=[gshape, gshape],
        compiler_params=_params("parallel"),
        name="mlstm_gate_fold",
    )(wg, wq, wk, wv)


def _mfeat_kernel(xm_ref, cw_ref, cb_ref, wk_ref, wqt_ref, wvt_ref,
                  xc_ref, k_ref, qt_ref, vt_ref, gr_ref, *, n_lat, chunk, row_group):
    ntot = xm_ref.shape[1]
    dh = wk_ref.shape[-1]
    ct = CONV_TILE
    half = ML_CONV_W // 2
    cw = cw_ref[...]
    cb = cb_ref[...]

    def conv_tile(t, carry):
        r0 = pl.multiple_of(t * ct, ct)
        at_start = jnp.logical_or(r0 == 0, r0 == n_lat)
        at_end = jnp.logical_or(r0 + ct == n_lat, r0 + ct == ntot)
        p0 = pl.multiple_of(jnp.maximum(r0 - SUBLANES, 0), SUBLANES)
        n0 = pl.multiple_of(jnp.minimum(r0 + ct, ntot - SUBLANES), SUBLANES)
        prev = xm_ref[0, pl.ds(p0, SUBLANES), :]
        nxt = xm_ref[0, pl.ds(n0, SUBLANES), :]
        prev = jnp.where(at_start, 0.0, prev)
        nxt = jnp.where(at_end, 0.0, nxt)
        xe = jnp.concatenate([prev, xm_ref[0, pl.ds(r0, ct), :], nxt], axis=0)
        acc = cb
        for dd in range(ML_CONV_W):
            lo = SUBLANES - half + dd
            acc = acc + cw[dd:dd + 1, :] * xe[lo:lo + ct, :]
        xc_ref[0, pl.ds(r0, ct), :] = _silu(acc)
        return carry

    lax.fori_loop(0, ntot // ct, conv_tile, 0)

    for g in range(ntot // row_group):
        rows = pl.ds(g * row_group, row_group)
        k_ref[0, rows, :] = _dot(xc_ref[0, rows, :].astype(BF16), wk_ref[0]).astype(BF16)

    def t_chunk(c, carry):
        r0 = pl.multiple_of(c * chunk, chunk)
        tq = _dot_nt(wqt_ref[0], xc_ref[0, pl.ds(r0, chunk), :].astype(BF16))
        tv = _dot_nt(wvt_ref[0], xm_ref[0, pl.ds(r0, chunk), :].astype(BF16))
        qt_ref[0, 0, c] = tq[:dh].astype(BF16)
        vt_ref[0, 0, c] = tv[:dh].astype(BF16)
        gr_ref[0, 0, c] = tq[dh:] + tv[dh:]
        return carry

    lax.fori_loop(0, ntot // chunk, t_chunk, 0, unroll=2)


def _mfeat(xm, conv_w, conv_b, wk, wqt_g, wvt_g, n_lat, chunk):
    nb, ntot, inner = xm.shape
    nh = ML_HEADS
    dh = inner // nh
    nct = ntot // chunk
    n_gates = wqt_g.shape[1] - dh
    ng = 2 if ntot % (2 * SUBLANES) == 0 else 1
    kern = functools.partial(_mfeat_kernel, n_lat=n_lat, chunk=chunk, row_group=ntot // ng)
    seq = pl.BlockSpec((1, ntot, dh), lambda b, h: (b, 0, h))
    wstack = pl.BlockSpec((1, dh + n_gates, dh), lambda b, h: (h, 0, 0))
    tspec = pl.BlockSpec((1, 1, nct, dh, chunk), lambda b, h: (b, h, 0, 0, 0))
    tshape = jax.ShapeDtypeStruct((nb, nh, nct, dh, chunk), BF16)
    return pl.pallas_call(
        kern,
        grid=(nb, nh),
        in_specs=[seq,
                  pl.BlockSpec((ML_CONV_W, dh), lambda b, h: (0, h)),
                  pl.BlockSpec((1, dh), lambda b, h: (0, h)),
                  pl.BlockSpec((1, dh, dh), lambda b, h: (h, 0, 0)), wstack, wstack],
        out_specs=[seq, seq, tspec, tspec,
                   pl.BlockSpec((1, 1, nct, n_gates, chunk), lambda b, h: (b, h, 0, 0, 0))],
        out_shape=[jax.ShapeDtypeStruct((nb, ntot, inner), F32),
                   jax.ShapeDtypeStruct((nb, ntot, inner), BF16), tshape, tshape,
                   jax.ShapeDtypeStruct((nb, nh, nct, n_gates, chunk), F32)],
        compiler_params=_params("parallel", "parallel"),
        name="mlstm_features",
    )(xm, conv_w, conv_b, wk, wqt_g, wvt_g)


def _split3(x):
    hi = x.astype(BF16)
    r1 = x - hi.astype(F32)
    mid = r1.astype(BF16)
    lo = (r1 - mid.astype(F32)).astype(BF16)
    return hi, mid, lo


def _running_max(x, reverse):
    n = x.shape[1]
    lane = lax.broadcasted_iota(jnp.int32, x.shape, 1)
    k = 1
    while k < n:
        if reverse:
            shifted = jnp.where(lane < n - k, pltpu.roll(x, n - k, 1), -jnp.inf)
        else:
            shifted = jnp.where(lane >= k, pltpu.roll(x, k, 1), -jnp.inf)
        x = jnp.maximum(x, shifted)
        k *= 2
    return x


def _gates_kernel(g_ref, b_ref, rows_ref, cols_ref):
    nh = ML_HEADS
    chunk = g_ref.shape[-1]
    pre = jnp.sum(g_ref[0, :, 0], axis=0) + b_ref[...]
    log_i = pre[:2 * nh]
    xf = pre[2 * nh:]
    log_f = jnp.minimum(xf, 0.0) - jnp.log1p(jnp.exp(-jnp.abs(xf)))
    s_idx = lax.broadcasted_iota(jnp.int32, (chunk, chunk), 0)
    t_idx = lax.broadcasted_iota(jnp.int32, (chunk, chunk), 1)
    tri_f = jnp.where(s_idx <= t_idx, 1.0, 0.0).astype(BF16)
    tri_b = jnp.where(s_idx >= t_idx, 1.0, 0.0).astype(BF16)
    cum_f = sum(_dot(p, tri_f) for p in _split3(log_f[:nh]))
    cum_b = sum(_dot(p, tri_b) for p in _split3(log_f[nh:]))
    gap_f = log_i[:nh] - cum_f
    gap_b = log_i[nh:] - cum_b
    gap = jnp.concatenate([gap_f, gap_b], axis=0)
    top = jnp.concatenate([_running_max(gap_f, False), _running_max(gap_b, True)], axis=0)
    rows_ref[0, 0] = jnp.concatenate([gap, top, cum_f, cum_b], axis=0)
    pad = jnp.zeros((LANES - 2 * nh, chunk), F32)
    cols_ref[0] = jnp.transpose(jnp.concatenate([gap, pad], axis=0))[:, :2 * nh]


def _gates(g_part, bias, chunk):
    nb, nh, nct, n_gates, _ = g_part.shape
    ntot = nct * chunk
    return pl.pallas_call(
        _gates_kernel,
        grid=(nb, nct),
        in_specs=[pl.BlockSpec((1, nh, 1, n_gates, chunk), lambda b, c: (b, 0, c, 0, 0)),
                  _full(bias.shape)],
        out_specs=[pl.BlockSpec((1, 1, 6 * nh, chunk), lambda b, c: (b, c, 0, 0)),
                   pl.BlockSpec((1, chunk, 2 * nh), lambda b, c: (b, c, 0))],
        out_shape=[jax.ShapeDtypeStruct((nb, nct, 6 * nh, chunk), F32),
                   jax.ShapeDtypeStruct((nb, ntot, 2 * nh), F32)],
        compiler_params=_params("parallel", "parallel"),
        name="mlstm_gates",
    )(g_part, bias)


STATE_PAD_ROWS = 16


def _scan_kernel(k_ref, qt_ref, vt_ref, rows_ref, cols_ref, hs_ref, c_sc, n_sc, m_sc,
                 *, n_lat_chunks, n_ctx_chunks, chunk):
    nh = ML_HEADS
    h = pl.program_id(1)
    dh = c_sc.shape[-1]
    c_sc[...] = jnp.zeros(c_sc.shape, F32)
    n_sc[...] = jnp.zeros(n_sc.shape, F32)
    m_sc[...] = jnp.zeros(m_sc.shape, F32)
    s_idx = lax.broadcasted_iota(jnp.int32, (chunk, chunk), 0)
    t_idx = lax.broadcasted_iota(jnp.int32, (chunk, chunk), 1)
    gate_lane = lax.broadcasted_iota(jnp.int32, (1, 2 * nh), 1)

    def step(d, cidx, with_out):
        off = pl.multiple_of(cidx * chunk, chunk)
        kc = k_ref[0, pl.ds(off, chunk), :]
        qt = qt_ref[0, 0, cidx]
        vt = vt_ref[0, 0, cidx]
        gap_r = rows_ref[0, cidx, pl.ds(d * nh + h, 1), :]
        top_r = rows_ref[0, cidx, pl.ds((2 + d) * nh + h, 1), :]
        cum_r = rows_ref[0, cidx, pl.ds((4 + d) * nh + h, 1), :]
        m = m_sc[d]
        ct = c_sc[d]
        nrow = n_sc[d]
        last = chunk - 1 if d == 0 else 0
        b_end = cum_r[:, last:last + 1]
        top_end = top_r[:, last:last + 1]
        if with_out is not None:
            ctile = cols_ref[0, pl.ds(off, chunk), :]
            gap_c = jnp.sum(jnp.where(gate_lane == d * nh + h, ctile, 0.0), axis=1, keepdims=True)
            lift = jnp.maximum(top_r, m)
            ordered = (s_idx <= t_idx) if d == 0 else (s_idx >= t_idx)
            dw_t = jnp.exp(jnp.where(ordered, gap_c - lift, -jnp.inf))
            iw = jnp.exp(m - lift)
            nb16 = jnp.broadcast_to(nrow.astype(BF16), (STATE_PAD_ROWS, dh))
            a = _dot(jnp.concatenate([kc, ct.astype(BF16), nb16], axis=0), qt)
            s_t = a[:chunk] * dw_t
            num_t = iw * a[chunk:chunk + dh] + _dot(vt, s_t.astype(BF16))
            den = iw * a[chunk + dh:chunk + dh + 1] + jnp.sum(s_t, axis=0, keepdims=True)
            floor = jnp.exp(-(cum_r + lift))
            h_t = num_t * (1.0 / jnp.maximum(jnp.abs(den), floor))
            hs_ref[0, 0, cidx] = h_t if with_out == "assign" else hs_ref[0, 0, cidx] + h_t
        mx = jnp.maximum(m, top_end)
        keep = jnp.exp(m - mx)
        w_r = jnp.exp(gap_r - mx)
        vw = (vt.astype(F32) * w_r).astype(BF16)
        wb16 = jnp.broadcast_to(w_r.astype(BF16), (STATE_PAD_ROWS, chunk))
        upd = _dot(jnp.concatenate([vw, wb16], axis=0), kc)
        c_sc[d] = keep * ct + upd[:dh]
        n_sc[d] = keep * nrow + upd[dh:dh + 1]
        m_sc[d] = b_end + mx

    def ctx_body(c, carry):
        step(0, n_lat_chunks + c, None)
        step(1, n_lat_chunks + n_ctx_chunks - 1 - c, None)
        return carry

    def lat_body(mode, c, carry):
        step(0, c, mode)
        step(1, n_lat_chunks - 1 - c, mode)
        return carry

    lax.fori_loop(0, n_ctx_chunks, ctx_body, 0)
    half = n_lat_chunks // 2
    lax.fori_loop(0, half, functools.partial(lat_body, "assign"), 0, unroll=2)
    lax.fori_loop(half, n_lat_chunks, functools.partial(lat_body, "add"), 0, unroll=2)


def _scan(k, qt, vt, rows, cols, n_lat, chunk):
    nb, ntot, inner = k.shape
    nh = ML_HEADS
    dh = inner // nh
    nct = ntot // chunk
    ncl = n_lat // chunk
    assert ncl % 2 == 0
    kern = functools.partial(_scan_kernel, n_lat_chunks=ncl, n_ctx_chunks=nct - ncl, chunk=chunk)
    tspec = pl.BlockSpec((1, 1, nct, dh, chunk), lambda b, h: (b, h, 0, 0, 0))
    return pl.pallas_call(
        kern,
        grid=(nb, nh),
        in_specs=[pl.BlockSpec((1, ntot, dh), lambda b, h: (b, 0, h)), tspec, tspec,
                  pl.BlockSpec((1, nct, rows.shape[2], chunk), lambda b, h: (b, 0, 0, 0)),
                  pl.BlockSpec((1, ntot, cols.shape[2]), lambda b, h: (b, 0, 0))],
        out_specs=pl.BlockSpec((1, 1, ncl, dh, chunk), lambda b, h: (b, h, 0, 0, 0)),
        out_shape=jax.ShapeDtypeStruct((nb, nh, ncl, dh, chunk), F32),
        scratch_shapes=[pltpu.VMEM((2, dh, dh), F32), pltpu.VMEM((2, 1, dh), F32),
                        pltpu.VMEM((2, 1, 1), F32)],
        compiler_params=_params("parallel", "parallel"),
        name="mlstm_scan",
    )(k, qt, vt, rows, cols)


def _out_odd_kernel(hs_ref, xc_ref, z_ref, h_ref, mod_ref, skip_ref, hg_ref, w_ref, g_ref, b_ref, o_ref):
    parts = []
    for hd in range(ML_HEADS):
        ht = hs_ref[0, hd, 0]
        mu = jnp.mean(ht, axis=0, keepdims=True)
        hc = ht - mu
        var = jnp.mean(hc * hc, axis=0, keepdims=True)
        parts.append(jnp.transpose(hc * lax.rsqrt(var + LN_EPS)))
    hn = jnp.concatenate(parts, axis=1)
    y = (hn * hg_ref[...] + skip_ref[...] * xc_ref[0]) * _silu(z_ref[0])
    yo = _dot(y.astype(BF16), w_ref[...])
    o_ref[0] = _post_norm(h_ref[0], mod_ref[0, 2:3, :] * yo, g_ref[...], b_ref[...])


def _out_odd(hs, xc, z, hcat, mod, skip, head_g, w_out, ln_g, ln_b, n_lat):
    nb, _, d = hcat.shape
    inner = xc.shape[-1]
    _, nh, _, dh, tm = hs.shape
    tok = lambda c: pl.BlockSpec((1, tm, c), lambda b, i: (b, i, 0))
    return pl.pallas_call(
        _out_odd_kernel,
        grid=(nb, n_lat // tm),
        in_specs=[pl.BlockSpec((1, nh, 1, dh, tm), lambda b, i: (b, 0, i, 0, 0)),
                  tok(inner), tok(inner), tok(d),
                  pl.BlockSpec((1, 6, d), lambda b, i: (b, 0, 0)),
                  _full((1, inner)), _full((1, inner)), _full(w_out.shape),
                  _full((1, d)), _full((1, d))],
        out_specs=tok(d),
        out_shape=jax.ShapeDtypeStruct((nb, n_lat, d), F32),
        compiler_params=_params("parallel", "parallel"),
        name="out_odd",
    )(hs, xc, z, hcat, mod, skip, head_g, w_out, ln_g, ln_b)


def _even_in_columns():
    cols, scale = [], []
    quarter = DA_DK // 2
    for hd in range(DA_HEADS):
        base = hd * DA_HEAD_COLS
        for blk in range(2):
            b0 = base + blk * 2 * DA_DK
            for half in range(2):
                for m in range(2):
                    cols += [b0 + m * DA_DK + half * quarter + j for j in range(quarter)]
            scale += [DA_DK ** -0.5 if blk == 0 else 1.0] * (2 * DA_DK)
        cols += list(range(base + 4 * DA_DK, base + DA_HEAD_COLS))
        scale += [1.0] * DA_DV
    cols += list(range(DA_HEADS * DA_HEAD_COLS, DA_HEADS * DA_HEAD_COLS + FN_COLS))
    scale += [1.0] * FN_COLS
    return np.asarray(cols, np.int32), np.asarray(scale, np.float32)


def _rope_tables(n_lat, ntot):
    rows = n_lat // GRID_W
    row = jnp.repeat(jnp.arange(rows, dtype=F32), GRID_W)
    col = jnp.tile(jnp.arange(GRID_W, dtype=F32), rows)
    n_freq = DA_DK // 4
    inv_freq = ROPE_BASE ** (-jnp.arange(n_freq, dtype=F32) / n_freq)
    ang = jnp.concatenate([row[:, None] * inv_freq, col[:, None] * inv_freq], -1)
    cos, sin = jnp.cos(ang), jnp.sin(ang)
    c = jnp.concatenate([cos, cos, cos, cos], axis=1)
    s = jnp.concatenate([-sin, -sin, sin, sin], axis=1)
    n_ctx = ntot - n_lat
    c = jnp.concatenate([c, jnp.ones((n_ctx, LANES), F32)], axis=0)
    s = jnp.concatenate([s, jnp.zeros((n_ctx, LANES), F32)], axis=0)
    return c, s


def _block_diag_heads(w, dh):
    nblk = w.shape[0]
    per_head = dh // ML_BLOCK
    wh = w.reshape(nblk // per_head, per_head, ML_BLOCK, ML_BLOCK)
    eye = jnp.eye(per_head, dtype=w.dtype)
    dense = jnp.einsum('hgij,gk->hgikj', wh, eye)
    return dense.reshape(nblk // per_head, dh, dh)


def kernel(x, c, ctx, c_ctx, w_mod, b_mod, ln_g, ln_b, w_ff1, w_ff3, w_ff2, a_w_in, a_w_out, da_lq1, da_lk1, da_lq2, da_lk2, da_head_g, m_w_in, m_w_out, m_conv_w, m_conv_b, m_wq, m_wk, m_wv, m_w_ig, m_b_ig, m_w_fg, m_b_fg, m_skip, m_head_g):
    nb, n_lat, d = x.shape
    n_ctx = ctx.shape[1]
    ntot = n_lat + n_ctx
    assert w_mod.shape[0] == DEPTH == 2
    assert n_lat % TOKEN_TILE == 0 and n_ctx % TOKEN_TILE == 0 and n_lat % n_ctx == 0 and n_lat % GRID_W == 0

    r = -(-(nb + 1) // MOD_ROWS_PAD) * MOD_ROWS_PAD
    cvec = jnp.concatenate([c, c_ctx[None, :], jnp.zeros((r - nb - 1, d), F32)], axis=0)
    mod_all = _modulation(cvec, w_mod, b_mod).reshape(DEPTH, r, 6, d)

    hcat = jnp.concatenate([x, ctx], axis=1)

    lam_init0 = 0.8 - 0.6 * math.exp(-0.3 * 0)
    cols, colscale = _even_in_columns()
    w_in = (a_w_in[0][:, cols] * colscale[None, :]).astype(BF16)
    rope_c, rope_s = _rope_tables(n_lat, ntot)
    q, k, v, f = _proj_even(hcat, mod_all[0], w_in, rope_c, rope_s, n_lat)
    lam_params = jnp.stack([da_lq1[0], da_lk1[0], da_lq2[0], da_lk2[0]], axis=0)
    a_n = _attention(q, k, v, lam_params, da_head_g[0][None, :], n_lat, lam_init0)
    fm = _fourier_mix(f, 0, n_lat, None)
    fm = _fourier_mix(f, n_lat, n_ctx, fm)
    h1 = _out_even(a_n, fm, hcat, mod_all[0], a_w_out[0].astype(BF16),
                   ln_g[0, 0][None, :], ln_b[0, 0][None, :], n_lat)
    hcat = _ffn(h1, ntot, mod_all[0], w_ff1[0].astype(BF16), w_ff3[0].astype(BF16), w_ff2[0].astype(BF16),
                ln_g[0, 1][None, :], ln_b[0, 1][None, :], n_lat)

    inner = m_w_in.shape[2] // 2
    dh = inner // ML_HEADS
    chunk = MXU_DIM if (n_lat % MXU_DIM == 0 and n_ctx % MXU_DIM == 0) else LANES
    xm, z = _proj_odd(hcat, mod_all[1], m_w_in[0].astype(BF16), n_lat)
    wq_f = _block_diag_heads(m_wq[0], dh)
    wk_f = _block_diag_heads(m_wk[0], dh) * (dh ** -0.5)
    wv_f = _block_diag_heads(m_wv[0], dh)
    wq, wk, wv = wq_f.astype(BF16), wk_f.astype(BF16), wv_f.astype(BF16)
    wg_all = jnp.concatenate([m_w_ig[0, 0], m_w_ig[0, 1], m_w_fg[0, 0], m_w_fg[0, 1]], axis=1)
    wg_all = wg_all.reshape(3, ML_HEADS, dh, 4 * ML_HEADS) * jnp.asarray([1.0, dh ** 0.5, 1.0], F32)[:, None, None, None]
    wg = jnp.transpose(wg_all, (1, 3, 0, 2)).reshape(ML_HEADS, 4 * ML_HEADS, 3 * dh).astype(BF16)
    g_bias = jnp.concatenate([m_b_ig[0, 0], m_b_ig[0, 1], m_b_fg[0, 0], m_b_fg[0, 1]])[:, None]
    gc, gm = _gate_fold(wg, wq, wk, wv)
    wqt_g = jnp.concatenate([jnp.swapaxes(wq_f, 1, 2).astype(BF16), gc], axis=1)
    wvt_g = jnp.concatenate([jnp.swapaxes(wv_f, 1, 2).astype(BF16), gm], axis=1)
    xc, km, qt, vt, g_part = _mfeat(xm, m_conv_w[0], m_conv_b[0][None, :], wk, wqt_g, wvt_g, n_lat, chunk)
    rows, colsg = _gates(g_part, g_bias, chunk)
    hs = _scan(km, qt, vt, rows, colsg, n_lat, chunk)
    h1 = _out_odd(hs, xc, z, hcat, mod_all[1], m_skip[0][None, :], m_head_g[0][None, :],
                  m_w_out[0].astype(BF16), ln_g[1, 0][None, :], ln_b[1, 0][None, :], n_lat)
    return _ffn(h1, n_lat, mod_all[1], w_ff1[1].astype(BF16), w_ff3[1].astype(BF16), w_ff2[1].astype(BF16),
                ln_g[1, 1][None, :], ln_b[1, 1][None, :], n_lat)
```

```python
import functools
import math

import numpy as np
import jax
import jax.numpy as jnp
from jax import lax
from jax.experimental import pallas as pl
from jax.experimental.pallas import tpu as pltpu

F32 = jnp.float32
BF16 = jnp.bfloat16

DA_HEADS = 6
DA_DK = 64
DA_DV = 2 * DA_DK
DA_HEAD_COLS = 4 * DA_DK + DA_DV
FN_GROUPS = 4
FN_CH = 64
FN_COLS = FN_GROUPS * FN_CH
ML_HEADS = 8
ML_BLOCK = 4
ML_CONV_W = 5
GRID_W = 64
ROPE_BASE = 10000.0
LN_EPS = 1e-5
DEPTH = 2
ALPHA = (2 * DEPTH) ** 0.25

LANES = 128
SUBLANES = 8
MXU_DIM = 256
VMEM_LIMIT_BYTES = 60 * 1024 * 1024

TOKEN_TILE = 256
CONV_TILE = 128
MOD_ROWS_PAD = 8
ATTN_Q_TILE = 512
SWEEP_UNROLL = 4


def _params(*sem):
    return pltpu.CompilerParams(dimension_semantics=sem, vmem_limit_bytes=VMEM_LIMIT_BYTES)


def _ln(x):
    mu = jnp.mean(x, axis=-1, keepdims=True)
    xc = x - mu
    var = jnp.mean(xc * xc, axis=-1, keepdims=True)
    return xc * lax.rsqrt(var + LN_EPS)


def _silu(x):
    return x * jax.nn.sigmoid(x)


def _dot(a, b):
    return jnp.dot(a, b, preferred_element_type=F32)


def _dot_nt(a, b):
    return lax.dot_general(a, b, (((1,), (1,)), ((), ())), preferred_element_type=F32)


def _full(shape):
    n = len(shape)
    return pl.BlockSpec(shape, lambda *_: (0,) * n)


def _mod_kernel(c_ref, w_ref, b_ref, o_ref):
    s = _silu(c_ref[...]).astype(BF16)
    o_ref[0] = _dot(s, w_ref[0].astype(BF16)) + b_ref[0]


def _modulation(cvec, w_mod, b_mod):
    depth, d, d6 = w_mod.shape
    r = cvec.shape[0]
    tn = d6 // 4
    return pl.pallas_call(
        _mod_kernel,
        grid=(depth, d6 // tn),
        in_specs=[pl.BlockSpec((r, d), lambda l, j: (0, 0)),
                  pl.BlockSpec((1, d, tn), lambda l, j: (l, 0, j)),
                  pl.BlockSpec((1, 1, tn), lambda l, j: (l, 0, j))],
        out_specs=pl.BlockSpec((1, r, tn), lambda l, j: (l, 0, j)),
        out_shape=jax.ShapeDtypeStruct((depth, r, d6), F32),
        compiler_params=_params("parallel", "parallel"),
        name="modulation",
    )(cvec, w_mod, b_mod.reshape(depth, 1, d6))


def _mod_spec(nb, n_lat_tiles, d):
    return pl.BlockSpec((1, 6, d), lambda b, i: (jnp.where(i < n_lat_tiles, b, nb), 0, 0))


def _proj_even_kernel(h_ref, mod_ref, w_ref, c_ref, s_ref, q_ref, k_ref, v_ref, f_ref):
    x = h_ref[0]
    u = (_ln(x) * (1.0 + mod_ref[0, 1:2, :]) + mod_ref[0, 0:1, :]).astype(BF16)
    p = _dot(u, w_ref[...])
    c = c_ref[...]
    s = s_ref[...]
    blk = DA_HEAD_COLS
    for hd in range(DA_HEADS):
        pq = p[:, hd * blk: hd * blk + LANES]
        pk = p[:, hd * blk + LANES: hd * blk + 2 * LANES]
        q_ref[0, hd] = (pq * c + pltpu.roll(pq, LANES // 2, 1) * s).astype(BF16)
        k_ref[0, hd] = (pk * c + pltpu.roll(pk, LANES // 2, 1) * s).astype(BF16)
        v_ref[0, hd] = p[:, hd * blk + 2 * LANES: (hd + 1) * blk].astype(BF16)
    f_ref[0] = p[:, DA_HEADS * blk:]


def _proj_even(hcat, mod, w, rope_c, rope_s, n_lat):
    nb, ntot, d = hcat.shape
    tm = TOKEN_TILE
    nt = ntot // tm
    ncols = w.shape[1]
    qkv_shape = jax.ShapeDtypeStruct((nb, DA_HEADS, ntot, LANES), BF16)
    qkv_spec = pl.BlockSpec((1, DA_HEADS, tm, LANES), lambda b, i: (b, 0, i, 0))
    return pl.pallas_call(
        _proj_even_kernel,
        grid=(nb, nt),
        in_specs=[pl.BlockSpec((1, tm, d), lambda b, i: (b, i, 0)),
                  _mod_spec(nb, n_lat // tm, d),
                  _full((d, ncols)),
                  pl.BlockSpec((tm, LANES), lambda b, i: (i, 0)),
                  pl.BlockSpec((tm, LANES), lambda b, i: (i, 0))],
        out_specs=[qkv_spec, qkv_spec, qkv_spec,
                   pl.BlockSpec((1, tm, FN_COLS), lambda b, i: (b, i, 0))],
        out_shape=[qkv_shape, qkv_shape, qkv_shape,
                   jax.ShapeDtypeStruct((nb, ntot, FN_COLS), F32)],
        compiler_params=_params("parallel", "parallel"),
        name="proj_even",
    )(hcat, mod, w, rope_c, rope_s)


def _stack_maps(q):
    qf = q.astype(F32)
    lane = lax.broadcasted_iota(jnp.int32, (1, LANES), 1)
    map1 = (lane % (LANES // 2)) < (LANES // 4)
    return jnp.concatenate([jnp.where(map1, qf, 0.0), jnp.where(map1, 0.0, qf)], axis=0).astype(BF16)


def _lane_block_max(s):
    blk = s[:, :LANES]
    for c in range(1, s.shape[1] // LANES):
        blk = jnp.maximum(blk, s[:, c * LANES:(c + 1) * LANES])
    return blk


def _exp_blocks(s, mb):
    return [jnp.exp(s[:, c * LANES:(c + 1) * LANES] - mb) for c in range(s.shape[1] // LANES)]


def _diff_merge(o, lp, head_g, lam_init):
    tq = o.shape[0] // 2
    lam = (jnp.exp(jnp.sum(lp[0:1] * lp[1:2], keepdims=True))
           - jnp.exp(jnp.sum(lp[2:3] * lp[3:4], keepdims=True)) + lam_init)
    a = o[:tq] - lam * o[tq:]
    a_n = a * lax.rsqrt(jnp.mean(a * a, axis=-1, keepdims=True) + LN_EPS) * (head_g * (1.0 - lam_init))
    return a_n.astype(BF16)


def _attn_kernel(lp_ref, g_ref, qc_ref, qn_ref, kc_ref, kn_ref, v_ref, o_ref,
                 sa_sc, sb_sc, mcur_sc, mnxt_sc, l_sc, acc_sc, *, tk, nkt, lam_init):
    step = (pl.program_id(0) * pl.num_programs(1) + pl.program_id(1)) * pl.num_programs(2) + pl.program_id(2)
    rows = mcur_sc.shape[0]

    def score_chunk(qq, k_ref, s_ref, j):
        off = pl.multiple_of(j * tk, tk)
        s = _dot_nt(qq, k_ref[0, 0, pl.ds(off, tk), :])
        s_ref[j] = s
        mnxt_sc[...] = jnp.maximum(mnxt_sc[...], _lane_block_max(s))

    def row_max():
        return jnp.broadcast_to(jnp.max(mnxt_sc[...], axis=1, keepdims=True), (rows, LANES))

    @pl.when(step == 0)
    def _():
        qq0 = _stack_maps(qc_ref[0, 0])
        mnxt_sc[...] = jnp.full(mnxt_sc.shape, -jnp.inf, F32)

        def first(j, carry):
            score_chunk(qq0, kc_ref, sa_sc, j)
            return carry

        lax.fori_loop(0, nkt, first, 0)
        mcur_sc[...] = row_max()

    qqn = _stack_maps(qn_ref[0, 0])
    mnxt_sc[...] = jnp.full(mnxt_sc.shape, -jnp.inf, F32)
    l_sc[...] = jnp.zeros(l_sc.shape, F32)
    acc_sc[...] = jnp.zeros(acc_sc.shape, F32)

    def sweep(cur_ref, nxt_ref):
        def body(j, carry):
            score_chunk(qqn, kn_ref, nxt_ref, j)
            off = pl.multiple_of(j * tk, tk)
            ps = _exp_blocks(cur_ref[j], mcur_sc[...])
            part = ps[0]
            for pc in ps[1:]:
                part = part + pc
            l_sc[...] += part
            acc_sc[...] += _dot(jnp.concatenate(ps, axis=1).astype(BF16), v_ref[0, 0, pl.ds(off, tk), :])
            return carry

        lax.fori_loop(0, nkt, body, 0, unroll=SWEEP_UNROLL)

    @pl.when(step % 2 == 0)
    def _():
        sweep(sa_sc, sb_sc)

    @pl.when(step % 2 == 1)
    def _():
        sweep(sb_sc, sa_sc)

    o = acc_sc[...] / jnp.sum(l_sc[...], axis=1, keepdims=True)
    o_ref[0] = _diff_merge(o, lp_ref[...], g_ref[...], lam_init)
    mcur_sc[...] = row_max()


def _attn_ctx_kernel(lp_ref, g_ref, q_ref, k_ref, v_ref, buf_ref, o_ref, *, lam_init):
    del buf_ref
    s = _dot_nt(_stack_maps(q_ref[0, 0]), k_ref[0, 0])
    p = jnp.exp(s - jnp.max(s, axis=1, keepdims=True))
    o = _dot(p.astype(BF16), v_ref[0, 0]) / jnp.sum(p, axis=1, keepdims=True)
    o_ref[0] = _diff_merge(o, lp_ref[...], g_ref[...], lam_init)


def _attention(q, k, v, lam_params, head_g, n_lat, lam_init):
    nb, nh, ntot, _ = q.shape
    n_ctx = ntot - n_lat
    tq = min(ATTN_Q_TILE, n_lat)
    tk = TOKEN_TILE
    nq = n_lat // tq
    nkt = ntot // tk
    rows = 2 * tq
    kv_spec = pl.BlockSpec((1, 1, ntot, LANES), lambda b, h, i: (b, h, 0, 0))

    def following(b, h, i):
        s = jnp.minimum((b * nh + h) * nq + i + 1, nb * nh * nq - 1)
        return s // (nh * nq), (s // nq) % nh, s % nq

    def qn_map(b, h, i):
        b2, h2, i2 = following(b, h, i)
        return b2, h2, i2, 0

    def kn_map(b, h, i):
        b2, h2, _ = following(b, h, i)
        return b2, h2, 0, 0

    a_lat = pl.pallas_call(
        functools.partial(_attn_kernel, tk=tk, nkt=nkt, lam_init=lam_init),
        grid=(nb, nh, nq),
        in_specs=[_full(lam_params.shape), _full(head_g.shape),
                  pl.BlockSpec((1, 1, tq, LANES), lambda b, h, i: (b, h, i, 0)),
                  pl.BlockSpec((1, 1, tq, LANES), qn_map),
                  kv_spec, pl.BlockSpec((1, 1, ntot, LANES), kn_map), kv_spec],
        out_specs=pl.BlockSpec((1, tq, LANES), lambda b, h, i: (b, i, h)),
        out_shape=jax.ShapeDtypeStruct((nb, ntot, nh * LANES), BF16),
        scratch_shapes=[pltpu.VMEM((nkt, rows, tk), F32)] * 2 + [pltpu.VMEM((rows, LANES), F32)] * 4,
        compiler_params=_params("arbitrary", "arbitrary", "arbitrary"),
        name="diff_attention",
    )(lam_params, head_g, q, q, k, k, v)
    cblk = n_lat // n_ctx
    ctx_spec = pl.BlockSpec((1, 1, n_ctx, LANES), lambda b, h: (b, h, cblk, 0))
    return pl.pallas_call(
        functools.partial(_attn_ctx_kernel, lam_init=lam_init),
        grid=(nb, nh),
        in_specs=[_full(lam_params.shape), _full(head_g.shape), ctx_spec, ctx_spec, ctx_spec,
                  pl.BlockSpec(memory_space=pl.ANY)],
        out_specs=pl.BlockSpec((1, n_ctx, LANES), lambda b, h: (b, cblk, h)),
        out_shape=jax.ShapeDtypeStruct(a_lat.shape, a_lat.dtype),
        input_output_aliases={5: 0},
        compiler_params=_params("parallel", "parallel"),
        name="diff_attention_ctx",
    )(lam_params, head_g, q, k, v, a_lat)


def _fourier_factor(n):
    a = 1 << (int(math.log2(n)) // 2)
    assert a * (n // a) == n and a % SUBLANES == 0 and (n // a) % SUBLANES == 0
    return a, n // a


def _fourier_tables(n):
    a, bn = _fourier_factor(n)
    j = np.arange(FN_CH)
    ang = 2.0 * np.pi * np.outer(j, j) / FN_CH
    eye = np.eye(FN_GROUPS)
    wc = np.concatenate([np.kron(eye, np.cos(ang)), -np.kron(eye, np.sin(ang))], axis=1)
    ia = np.arange(a)
    ang_a = 2.0 * np.pi * np.outer(ia, ia) / a
    ca, sa = np.cos(ang_a), np.sin(ang_a)
    m1 = np.block([[ca, sa], [-sa, ca]])
    ib = np.arange(bn)
    ang_t = 2.0 * np.pi * np.outer(ia, ib) / n
    ang_b = 2.0 * np.pi * np.outer(ib, ib) / bn
    m3 = np.concatenate([np.cos(ang_b), np.sin(ang_b)], axis=1) / math.sqrt(n * FN_CH)
    tw_c = jnp.repeat(jnp.asarray(np.cos(ang_t), F32), FN_COLS, axis=1)
    tw_s = jnp.repeat(jnp.asarray(np.sin(ang_t), F32), FN_COLS, axis=1)
    return (jnp.asarray(wc, BF16), jnp.asarray(m1, BF16), tw_c, tw_s, jnp.asarray(m3, BF16))


def _fourier_chan_kernel(x_ref, w_ref, o_ref):
    g = _dot(x_ref[0].astype(BF16), w_ref[...])
    o_ref[0, 0] = g[:, :FN_COLS]
    o_ref[0, 1] = g[:, FN_COLS:]


def _fourier_stage1_kernel(g_ref, m_ref, c_ref, s_ref, o_ref, *, a, bb):
    t1 = _dot(m_ref[...], g_ref[0].astype(BF16))
    tr, ti = t1[:a], t1[a:]
    c = c_ref[...]
    s = s_ref[...]
    t2r = tr * c + ti * s
    t2i = ti * c - tr * s
    for b in range(bb):
        o_ref[0, 0, b] = t2r[:, b * FN_COLS:(b + 1) * FN_COLS]
        o_ref[0, 1, b] = t2i[:, b * FN_COLS:(b + 1) * FN_COLS]


def _fourier_stage2_kernel(t_ref, m_ref, *rest):
    o_ref = rest[-1]
    o_ref[0] = _dot(m_ref[...], t_ref[0].astype(BF16))


def _fourier_mix(fcat, row0, n, out_buf):
    nb, ntot, _ = fcat.shape
    a, bn = _fourier_factor(n)
    wc, m1, tw_c, tw_s, m3 = _fourier_tables(n)
    tm = min(TOKEN_TILE, n)
    blk0 = row0 // tm
    g = pl.pallas_call(
        _fourier_chan_kernel,
        grid=(nb, n // tm),
        in_specs=[pl.BlockSpec((1, tm, FN_COLS), lambda b, i: (b, blk0 + i, 0)),
                  _full(wc.shape)],
        out_specs=pl.BlockSpec((1, 2, tm, FN_COLS), lambda b, i: (b, 0, i, 0)),
        out_shape=jax.ShapeDtypeStruct((nb, 2, n, FN_COLS), F32),
        compiler_params=_params("parallel", "parallel"),
        name="fourier_chan",
    )(fcat, wc)
    bb = min(8, bn)
    tc = bb * FN_COLS
    t2 = pl.pallas_call(
        functools.partial(_fourier_stage1_kernel, a=a, bb=bb),
        grid=(bn // bb, nb),
        in_specs=[pl.BlockSpec((1, 2 * a, tc), lambda j, b: (b, 0, j)),
                  _full(m1.shape),
                  pl.BlockSpec((a, tc), lambda j, b: (0, j)),
                  pl.BlockSpec((a, tc), lambda j, b: (0, j))],
        out_specs=pl.BlockSpec((1, 2, bb, a, FN_COLS), lambda j, b: (b, 0, j, 0, 0)),
        out_shape=jax.ShapeDtypeStruct((nb, 2, bn, a, FN_COLS), F32),
        compiler_params=_params("parallel", "parallel"),
        name="fourier_stage1",
    )(g.reshape(nb, 2 * a, bn * FN_COLS), m1, tw_c, tw_s)
    assert ntot % a == 0 and (row0 // a) % bn == 0
    tc2 = min(8, a) * FN_COLS
    out_rows = ntot // a
    in_specs = [pl.BlockSpec((1, 2 * bn, tc2), lambda b, j: (b, 0, j)), _full(m3.shape)]
    args = [t2.reshape(nb, 2 * bn, a * FN_COLS), m3]
    aliases = {}
    if out_buf is not None:
        in_specs.append(pl.BlockSpec(memory_space=pl.ANY))
        args.append(out_buf.reshape(nb, out_rows, a * FN_COLS))
        aliases = {2: 0}
    rblk = (row0 // a) // bn
    out = pl.pallas_call(
        _fourier_stage2_kernel,
        grid=(nb, a * FN_COLS // tc2),
        in_specs=in_specs,
        out_specs=pl.BlockSpec((1, bn, tc2), lambda b, j: (b, rblk, j)),
        out_shape=jax.ShapeDtypeStruct((nb, out_rows, a * FN_COLS), F32),
        input_output_aliases=aliases,
        compiler_params=_params("parallel", "parallel"),
        name="fourier_stage2",
    )(*args)
    return out.reshape(nb, ntot, FN_COLS)


def _post_norm(h, y, g, b):
    return _ln(ALPHA * h + y) * g + b


def _out_even_kernel(a_ref, f_ref, h_ref, mod_ref, w_ref, g_ref, b_ref, o_ref):
    na = a_ref.shape[-1]
    y = _dot(a_ref[0], w_ref[:na, :]) + _dot(f_ref[0].astype(BF16), w_ref[na:, :])
    o_ref[0] = _post_norm(h_ref[0], mod_ref[0, 2:3, :] * y, g_ref[...], b_ref[...])


def _out_even(a_n, fm, hcat, mod, w_out, ln_g, ln_b, n_lat):
    nb, ntot, d = hcat.shape
    tm = TOKEN_TILE
    na = a_n.shape[-1]
    tok = lambda c: pl.BlockSpec((1, tm, c), lambda b, i: (b, i, 0))
    return pl.pallas_call(
        _out_even_kernel,
        grid=(nb, ntot // tm),
        in_specs=[tok(na), tok(FN_COLS), tok(d), _mod_spec(nb, n_lat // tm, d),
                  _full(w_out.shape), _full((1, d)), _full((1, d))],
        out_specs=tok(d),
        out_shape=jax.ShapeDtypeStruct((nb, ntot, d), F32),
        compiler_params=_params("parallel", "parallel"),
        name="out_even",
    )(a_n, fm, hcat, mod, w_out, ln_g, ln_b)


def _ffn_kernel(h_ref, mod_ref, w1_ref, w3_ref, w2_ref, g_ref, b_ref, o_ref):
    h = h_ref[0]
    u = (_ln(h) * (1.0 + mod_ref[0, 4:5, :]) + mod_ref[0, 3:4, :]).astype(BF16)
    hid = (_silu(_dot(u, w1_ref[...])) * _dot(u, w3_ref[...])).astype(BF16)
    y = _dot(hid, w2_ref[...])
    o_ref[0] = _post_norm(h, mod_ref[0, 5:6, :] * y, g_ref[...], b_ref[...])


def _ffn(h, n_rows, mod, w1, w3, w2, ln_g, ln_b, n_lat):
    nb, _, d = h.shape
    tm = TOKEN_TILE
    tok = pl.BlockSpec((1, tm, d), lambda b, i: (b, i, 0))
    resident = lambda s: pl.BlockSpec(s, lambda b, i: (0, 0), pipeline_mode=pl.Buffered(1))
    return pl.pallas_call(
        _ffn_kernel,
        grid=(nb, n_rows // tm),
        in_specs=[tok, _mod_spec(nb, n_lat // tm, d),
                  resident(w1.shape), resident(w3.shape), resident(w2.shape),
                  _full((1, d)), _full((1, d))],
        out_specs=tok,
        out_shape=jax.ShapeDtypeStruct((nb, n_rows, d), F32),
        compiler_params=_params("parallel", "parallel"),
        name="ffn",
    )(h, mod, w1, w3, w2, ln_g, ln_b)


def _proj_odd_kernel(h_ref, mod_ref, w_ref, xm_ref, z_ref):
    x = h_ref[0]
    u = (_ln(x) * (1.0 + mod_ref[0, 1:2, :]) + mod_ref[0, 0:1, :]).astype(BF16)
    p = _dot(u, w_ref[...])
    inner = xm_ref.shape[-1]
    xm_ref[0] = p[:, :inner]
    z_ref[0] = p[:, inner:]


def _proj_odd(hcat, mod, w, n_lat):
    nb, ntot, d = hcat.shape
    tm = TOKEN_TILE
    inner = w.shape[1] // 2
    out_spec = pl.BlockSpec((1, tm, inner), lambda b, i: (b, i, 0))
    out_shape = jax.ShapeDtypeStruct((nb, ntot, inner), F32)
    return pl.pallas_call(
        _proj_odd_kernel,
        grid=(nb, ntot // tm),
        in_specs=[pl.BlockSpec((1, tm, d), lambda b, i: (b, i, 0)),
                  _mod_spec(nb, n_lat // tm, d),
                  _full(w.shape)],
        out_specs=[out_spec, out_spec],
        out_shape=[out_shape, out_shape],
        compiler_params=_params("parallel", "parallel"),
        name="proj_odd",
    )(hcat, mod, w)


def _gate_fold_kernel(wg_ref, wq_ref, wk_ref, wv_ref, gc_ref, gm_ref):
    dh = wq_ref.shape[-1]
    wg = wg_ref[0]
    gc_ref[0] = (_dot_nt(wg[:, :dh], wq_ref[0]) + _dot_nt(wg[:, dh:2 * dh], wk_ref[0])).astype(BF16)
    gm_ref[0] = _dot_nt(wg[:, 2 * dh:], wv_ref[0]).astype(BF16)


def _gate_fold(wg, wq, wk, wv):
    nh, n_gates, _ = wg.shape
    dh = wq.shape[-1]
    wblk = pl.BlockSpec((1, dh, dh), lambda h: (h, 0, 0))
    gspec = pl.BlockSpec((1, n_gates, dh), lambda h: (h, 0, 0))
    gshape = jax.ShapeDtypeStruct((nh, n_gates, dh), BF16)
    return pl.pallas_call(
        _gate_fold_kernel,
        grid=(nh,),
        in_specs=[pl.BlockSpec((1, n_gates, 3 * dh), lambda h: (h, 0, 0)), wblk, wblk, wblk],
        out_specs=[gspec, gspec],
        out_shape=[gshape, gshape],
        compiler_params=_params("parallel"),
        name="mlstm_gate_fold",
    )(wg, wq, wk, wv)


def _mfeat_kernel(xm_ref, cw_ref, cb_ref, wk_ref, wqt_ref, wvt_ref,
                  xc_ref, k_ref, qt_ref, vt_ref, gr_ref, *, n_lat, chunk, row_group):
    ntot = xm_ref.shape[1]
    dh = wk_ref.shape[-1]
    ct = CONV_TILE
    half = ML_CONV_W // 2
    cw = cw_ref[...]
    cb = cb_ref[...]

    def conv_tile(t, carry):
        r0 = pl.multiple_of(t * ct, ct)
        at_start = jnp.logical_or(r0 == 0, r0 == n_lat)
        at_end = jnp.logical_or(r0 + ct == n_lat, r0 + ct == ntot)
        p0 = pl.multiple_of(jnp.maximum(r0 - SUBLANES, 0), SUBLANES)
        n0 = pl.multiple_of(jnp.minimum(r0 + ct, ntot - SUBLANES), SUBLANES)
        prev = xm_ref[0, pl.ds(p0, SUBLANES), :]
        nxt = xm_ref[0, pl.ds(n0, SUBLANES), :]
        prev = jnp.where(at_start, 0.0, prev)
        nxt = jnp.where(at_end, 0.0, nxt)
        xe = jnp.concatenate([prev, xm_ref[0, pl.ds(r0, ct), :], nxt], axis=0)
        acc = cb
        for dd in range(ML_CONV_W):
            lo = SUBLANES - half + dd
            acc = acc + cw[dd:dd + 1, :] * xe[lo:lo + ct, :]
        xc_ref[0, pl.ds(r0, ct), :] = _silu(acc)
        return carry

    lax.fori_loop(0, ntot // ct, conv_tile, 0)

    for g in range(ntot // row_group):
        rows = pl.ds(g * row_group, row_group)
        k_ref[0, rows, :] = _dot(xc_ref[0, rows, :].astype(BF16), wk_ref[0]).astype(BF16)

    def t_chunk(c, carry):
        r0 = pl.multiple_of(c * chunk, chunk)
        tq = _dot_nt(wqt_ref[0], xc_ref[0, pl.ds(r0, chunk), :].astype(BF16))
        tv = _dot_nt(wvt_ref[0], xm_ref[0, pl.ds(r0, chunk), :].astype(BF16))
        qt_ref[0, 0, c] = tq[:dh].astype(BF16)
        vt_ref[0, 0, c] = tv[:dh].astype(BF16)
        gr_ref[0, 0, c] = tq[dh:] + tv[dh:]
        return carry

    lax.fori_loop(0, ntot // chunk, t_chunk, 0, unroll=2)


def _mfeat(xm, conv_w, conv_b, wk, wqt_g, wvt_g, n_lat, chunk):
    nb, ntot, inner = xm.shape
    nh = ML_HEADS
    dh = inner // nh
    nct = ntot // chunk
    n_gates = wqt_g.shape[1] - dh
    ng = 2 if ntot % (2 * SUBLANES) == 0 else 1
    kern = functools.partial(_mfeat_kernel, n_lat=n_lat, chunk=chunk, row_group=ntot // ng)
    seq = pl.BlockSpec((1, ntot, dh), lambda b, h: (b, 0, h))
    wstack = pl.BlockSpec((1, dh + n_gates, dh), lambda b, h: (h, 0, 0))
    tspec = pl.BlockSpec((1, 1, nct, dh, chunk), lambda b, h: (b, h, 0, 0, 0))
    tshape = jax.ShapeDtypeStruct((nb, nh, nct, dh, chunk), BF16)
    return pl.pallas_call(
        kern,
        grid=(nb, nh),
        in_specs=[seq,
                  pl.BlockSpec((ML_CONV_W, dh), lambda b, h: (0, h)),
                  pl.BlockSpec((1, dh), lambda b, h: (0, h)),
                  pl.BlockSpec((1, dh, dh), lambda b, h: (h, 0, 0)), wstack, wstack],
        out_specs=[seq, seq, tspec, tspec,
                   pl.BlockSpec((1, 1, nct, n_gates, chunk), lambda b, h: (b, h, 0, 0, 0))],
        out_shape=[jax.ShapeDtypeStruct((nb, ntot, inner), F32),
                   jax.ShapeDtypeStruct((nb, ntot, inner), BF16), tshape, tshape,
                   jax.ShapeDtypeStruct((nb, nh, nct, n_gates, chunk), F32)],
        compiler_params=_params("parallel", "parallel"),
        name="mlstm_features",
    )(xm, conv_w, conv_b, wk, wqt_g, wvt_g)


def _split3(x):
    hi = x.astype(BF16)
    r1 = x - hi.astype(F32)
    mid = r1.astype(BF16)
    lo = (r1 - mid.astype(F32)).astype(BF16)
    return hi, mid, lo


def _running_max(x, reverse):
    n = x.shape[1]
    lane = lax.broadcasted_iota(jnp.int32, x.shape, 1)
    k = 1
    while k < n:
        if reverse:
            shifted = jnp.where(lane < n - k, pltpu.roll(x, n - k, 1), -jnp.inf)
        else:
            shifted = jnp.where(lane >= k, pltpu.roll(x, k, 1), -jnp.inf)
        x = jnp.maximum(x, shifted)
        k *= 2
    return x


def _gates_kernel(g_ref, b_ref, rows_ref, cols_ref):
    nh = ML_HEADS
    nct, chunk = g_ref.shape[2], g_ref.shape[-1]
    bias = b_ref[...]
    li_f, li_b, lf_f, lf_b = [], [], [], []
    for c in range(nct):
        pre = jnp.sum(g_ref[0, :, c], axis=0) + bias
        xf = pre[2 * nh:]
        log_f = jnp.minimum(xf, 0.0) - jnp.log1p(jnp.exp(-jnp.abs(xf)))
        li_f.append(pre[:nh])
        li_b.append(pre[nh:2 * nh])
        lf_f.append(log_f[:nh])
        lf_b.append(log_f[nh:])
    s_idx = lax.broadcasted_iota(jnp.int32, (chunk, chunk), 0)
    t_idx = lax.broadcasted_iota(jnp.int32, (chunk, chunk), 1)
    tri_f = jnp.where(s_idx <= t_idx, 1.0, 0.0).astype(BF16)
    tri_b = jnp.where(s_idx >= t_idx, 1.0, 0.0).astype(BF16)
    cum_f = sum(_dot(p, tri_f) for p in _split3(jnp.concatenate(lf_f, axis=0)))
    cum_b = sum(_dot(p, tri_b) for p in _split3(jnp.concatenate(lf_b, axis=0)))
    gap_f = jnp.concatenate(li_f, axis=0) - cum_f
    gap_b = jnp.concatenate(li_b, axis=0) - cum_b
    top_f = _running_max(gap_f, False)
    top_b = _running_max(gap_b, True)
    pad = jnp.zeros((LANES - 2 * nh, chunk), F32)
    for c in range(nct):
        r = slice(c * nh, (c + 1) * nh)
        rows_ref[0, c] = jnp.concatenate([gap_f[r], gap_b[r], top_f[r], top_b[r], cum_f[r], cum_b[r]], axis=0)
        gap_t = jnp.transpose(jnp.concatenate([gap_f[r], gap_b[r], pad], axis=0))
        cols_ref[0, c * chunk:(c + 1) * chunk, :] = gap_t[:, :2 * nh]


def _gates(g_part, bias, chunk):
    nb, nh, nct, n_gates, _ = g_part.shape
    ntot = nct * chunk
    return pl.pallas_call(
        _gates_kernel,
        grid=(nb,),
        in_specs=[pl.BlockSpec((1, nh, nct, n_gates, chunk), lambda b: (b, 0, 0, 0, 0)),
                  _full(bias.shape)],
        out_specs=[pl.BlockSpec((1, nct, 6 * nh, chunk), lambda b: (b, 0, 0, 0)),
                   pl.BlockSpec((1, ntot, 2 * nh), lambda b: (b, 0, 0))],
        out_shape=[jax.ShapeDtypeStruct((nb, nct, 6 * nh, chunk), F32),
                   jax.ShapeDtypeStruct((nb, ntot, 2 * nh), F32)],
        compiler_params=_params("parallel"),
        name="mlstm_gates",
    )(g_part, bias)


STATE_PAD_ROWS = 16


def _scan_kernel(k_ref, qt_ref, vt_ref, rows_ref, cols_ref, hs_ref, c_sc, n_sc, m_sc,
                 *, n_lat_chunks, n_ctx_chunks, chunk):
    nh = ML_HEADS
    h = pl.program_id(1)
    dh = c_sc.shape[-1]
    c_sc[...] = jnp.zeros(c_sc.shape, F32)
    n_sc[...] = jnp.zeros(n_sc.shape, F32)
    m_sc[...] = jnp.zeros(m_sc.shape, F32)
    s_idx = lax.broadcasted_iota(jnp.int32, (chunk, chunk), 0)
    t_idx = lax.broadcasted_iota(jnp.int32, (chunk, chunk), 1)
    gate_lane = lax.broadcasted_iota(jnp.int32, (1, 2 * nh), 1)

    def step(d, cidx, with_out):
        off = pl.multiple_of(cidx * chunk, chunk)
        kc = k_ref[0, pl.ds(off, chunk), :]
        qt = qt_ref[0, 0, cidx]
        vt = vt_ref[0, 0, cidx]
        gap_r = rows_ref[0, cidx, pl.ds(d * nh + h, 1), :]
        top_r = rows_ref[0, cidx, pl.ds((2 + d) * nh + h, 1), :]
        cum_r = rows_ref[0, cidx, pl.ds((4 + d) * nh + h, 1), :]
        m = m_sc[d]
        ct = c_sc[d]
        nrow = n_sc[d]
        last = chunk - 1 if d == 0 else 0
        b_end = cum_r[:, last:last + 1]
        top_end = top_r[:, last:last + 1]
        if with_out is not None:
            ctile = cols_ref[0, pl.ds(off, chunk), :]
            gap_c = jnp.sum(jnp.where(gate_lane == d * nh + h, ctile, 0.0), axis=1, keepdims=True)
            lift = jnp.maximum(top_r, m)
            ordered = (s_idx <= t_idx) if d == 0 else (s_idx >= t_idx)
            dw_t = jnp.exp(jnp.where(ordered, gap_c - lift, -jnp.inf))
            iw = jnp.exp(m - lift)
            nb16 = jnp.broadcast_to(nrow.astype(BF16), (STATE_PAD_ROWS, dh))
            a = _dot(jnp.concatenate([kc, ct.astype(BF16), nb16], axis=0), qt)
            s_t = a[:chunk] * dw_t
            num_t = iw * a[chunk:chunk + dh] + _dot(vt, s_t.astype(BF16))
            den = iw * a[chunk + dh:chunk + dh + 1] + jnp.sum(s_t, axis=0, keepdims=True)
            floor = jnp.exp(-(cum_r + lift))
            h_t = num_t * (1.0 / jnp.maximum(jnp.abs(den), floor))
            hs_ref[0, 0, cidx] = h_t if with_out == "assign" else hs_ref[0, 0, cidx] + h_t
        mx = jnp.maximum(m, top_end)
        keep = jnp.exp(m - mx)
        w_r = jnp.exp(gap_r - mx)
        vw = (vt.astype(F32) * w_r).astype(BF16)
        wb16 = jnp.broadcast_to(w_r.astype(BF16), (STATE_PAD_ROWS, chunk))
        upd = _dot(jnp.concatenate([vw, wb16], axis=0), kc)
        c_sc[d] = keep * ct + upd[:dh]
        n_sc[d] = keep * nrow + upd[dh:dh + 1]
        m_sc[d] = b_end + mx

    def ctx_body(c, carry):
        step(0, n_lat_chunks + c, None)
        step(1, n_lat_chunks + n_ctx_chunks - 1 - c, None)
        return carry

    def lat_body(mode, c, carry):
        step(0, c, mode)
        step(1, n_lat_chunks - 1 - c, mode)
        return carry

    lax.fori_loop(0, n_ctx_chunks, ctx_body, 0)
    half = n_lat_chunks // 2
    lax.fori_loop(0, half, functools.partial(lat_body, "assign"), 0, unroll=2)
    lax.fori_loop(half, n_lat_chunks, functools.partial(lat_body, "add"), 0, unroll=2)


def _scan(k, qt, vt, rows, cols, n_lat, chunk):
    nb, ntot, inner = k.shape
    nh = ML_HEADS
    dh = inner // nh
    nct = ntot // chunk
    ncl = n_lat // chunk
    assert ncl % 2 == 0
    kern = functools.partial(_scan_kernel, n_lat_chunks=ncl, n_ctx_chunks=nct - ncl, chunk=chunk)
    tspec = pl.BlockSpec((1, 1, nct, dh, chunk), lambda b, h: (b, h, 0, 0, 0))
    return pl.pallas_call(
        kern,
        grid=(nb, nh),
        in_specs=[pl.BlockSpec((1, ntot, dh), lambda b, h: (b, 0, h)), tspec, tspec,
                  pl.BlockSpec((1, nct, rows.shape[2], chunk), lambda b, h: (b, 0, 0, 0)),
                  pl.BlockSpec((1, ntot, cols.shape[2]), lambda b, h: (b, 0, 0))],
        out_specs=pl.BlockSpec((1, 1, ncl, dh, chunk), lambda b, h: (b, h, 0, 0, 0)),
        out_shape=jax.ShapeDtypeStruct((nb, nh, ncl, dh, chunk), F32),
        scratch_shapes=[pltpu.VMEM((2, dh, dh), F32), pltpu.VMEM((2, 1, dh), F32),
                        pltpu.VMEM((2, 1, 1), F32)],
        compiler_params=_params("parallel", "parallel"),
        name="mlstm_scan",
    )(k, qt, vt, rows, cols)


def _out_odd_kernel(hs_ref, xc_ref, z_ref, h_ref, mod_ref, skip_ref, hg_ref, w_ref, g_ref, b_ref, o_ref):
    parts = []
    for hd in range(ML_HEADS):
        ht = hs_ref[0, hd, 0]
        mu = jnp.mean(ht, axis=0, keepdims=True)
        hc = ht - mu
        var = jnp.mean(hc * hc, axis=0, keepdims=True)
        parts.append(jnp.transpose(hc * lax.rsqrt(var + LN_EPS)))
    hn = jnp.concatenate(parts, axis=1)
    y = (hn * hg_ref[...] + skip_ref[...] * xc_ref[0]) * _silu(z_ref[0])
    yo = _dot(y.astype(BF16), w_ref[...])
    o_ref[0] = _post_norm(h_ref[0], mod_ref[0, 2:3, :] * yo, g_ref[...], b_ref[...])


def _out_odd(hs, xc, z, hcat, mod, skip, head_g, w_out, ln_g, ln_b, n_lat):
    nb, _, d = hcat.shape
    inner = xc.shape[-1]
    _, nh, _, dh, tm = hs.shape
    tok = lambda c: pl.BlockSpec((1, tm, c), lambda b, i: (b, i, 0))
    return pl.pallas_call(
        _out_odd_kernel,
        grid=(nb, n_lat // tm),
        in_specs=[pl.BlockSpec((1, nh, 1, dh, tm), lambda b, i: (b, 0, i, 0, 0)),
                  tok(inner), tok(inner), tok(d),
                  pl.BlockSpec((1, 6, d), lambda b, i: (b, 0, 0)),
                  _full((1, inner)), _full((1, inner)), _full(w_out.shape),
                  _full((1, d)), _full((1, d))],
        out_specs=tok(d),
        out_shape=jax.ShapeDtypeStruct((nb, n_lat, d), F32),
        compiler_params=_params("parallel", "parallel"),
        name="out_odd",
    )(hs, xc, z, hcat, mod, skip, head_g, w_out, ln_g, ln_b)


def _even_in_columns():
    cols, scale = [], []
    quarter = DA_DK // 2
    for hd in range(DA_HEADS):
        base = hd * DA_HEAD_COLS
        for blk in range(2):
            b0 = base + blk * 2 * DA_DK
            for half in range(2):
                for m in range(2):
                    cols += [b0 + m * DA_DK + half * quarter + j for j in range(quarter)]
            scale += [DA_DK ** -0.5 if blk == 0 else 1.0] * (2 * DA_DK)
        cols += list(range(base + 4 * DA_DK, base + DA_HEAD_COLS))
        scale += [1.0] * DA_DV
    cols += list(range(DA_HEADS * DA_HEAD_COLS, DA_HEADS * DA_HEAD_COLS + FN_COLS))
    scale += [1.0] * FN_COLS
    return np.asarray(cols, np.int32), np.asarray(scale, np.float32)


def _rope_tables(n_lat, ntot):
    rows = n_lat // GRID_W
    row = jnp.repeat(jnp.arange(rows, dtype=F32), GRID_W)
    col = jnp.tile(jnp.arange(GRID_W, dtype=F32), rows)
    n_freq = DA_DK // 4
    inv_freq = ROPE_BASE ** (-jnp.arange(n_freq, dtype=F32) / n_freq)
    ang = jnp.concatenate([row[:, None] * inv_freq, col[:, None] * inv_freq], -1)
    cos, sin = jnp.cos(ang), jnp.sin(ang)
    c = jnp.concatenate([cos, cos, cos, cos], axis=1)
    s = jnp.concatenate([-sin, -sin, sin, sin], axis=1)
    n_ctx = ntot - n_lat
    c = jnp.concatenate([c, jnp.ones((n_ctx, LANES), F32)], axis=0)
    s = jnp.concatenate([s, jnp.zeros((n_ctx, LANES), F32)], axis=0)
    return c, s


def _block_diag_heads(w, dh):
    nblk = w.shape[0]
    per_head = dh // ML_BLOCK
    wh = w.reshape(nblk // per_head, per_head, ML_BLOCK, ML_BLOCK)
    eye = jnp.eye(per_head, dtype=w.dtype)
    dense = jnp.einsum('hgij,gk->hgikj', wh, eye)
    return dense.reshape(nblk // per_head, dh, dh)


def kernel(x, c, ctx, c_ctx, w_mod, b_mod, ln_g, ln_b, w_ff1, w_ff3, w_ff2, a_w_in, a_w_out, da_lq1, da_lk1, da_lq2, da_lk2, da_head_g, m_w_in, m_w_out, m_conv_w, m_conv_b, m_wq, m_wk, m_wv, m_w_ig, m_b_ig, m_w_fg, m_b_fg, m_skip, m_head_g):
    nb, n_lat, d = x.shape
    n_ctx = ctx.shape[1]
    ntot = n_lat + n_ctx
    assert w_mod.shape[0] == DEPTH == 2
    assert n_lat % TOKEN_TILE == 0 and n_ctx % TOKEN_TILE == 0 and n_lat % n_ctx == 0 and n_lat % GRID_W == 0

    r = -(-(nb + 1) // MOD_ROWS_PAD) * MOD_ROWS_PAD
    cvec = jnp.concatenate([c, c_ctx[None, :], jnp.zeros((r - nb - 1, d), F32)], axis=0)
    mod_all = _modulation(cvec, w_mod, b_mod).reshape(DEPTH, r, 6, d)

    hcat = jnp.concatenate([x, ctx], axis=1)

    lam_init0 = 0.8 - 0.6 * math.exp(-0.3 * 0)
    cols, colscale = _even_in_columns()
    w_in = (a_w_in[0][:, cols] * colscale[None, :]).astype(BF16)
    rope_c, rope_s = _rope_tables(n_lat, ntot)
    q, k, v, f = _proj_even(hcat, mod_all[0], w_in, rope_c, rope_s, n_lat)
    lam_params = jnp.stack([da_lq1[0], da_lk1[0], da_lq2[0], da_lk2[0]], axis=0)
    a_n = _attention(q, k, v, lam_params, da_head_g[0][None, :], n_lat, lam_init0)
    fm = _fourier_mix(f, 0, n_lat, None)
    fm = _fourier_mix(f, n_lat, n_ctx, fm)
    h1 = _out_even(a_n, fm, hcat, mod_all[0], a_w_out[0].astype(BF16),
                   ln_g[0, 0][None, :], ln_b[0, 0][None, :], n_lat)
    hcat = _ffn(h1, ntot, mod_all[0], w_ff1[0].astype(BF16), w_ff3[0].astype(BF16), w_ff2[0].astype(BF16),
                ln_g[0, 1][None, :], ln_b[0, 1][None, :], n_lat)

    inner = m_w_in.shape[2] // 2
    dh = inner // ML_HEADS
    chunk = MXU_DIM if (n_lat % MXU_DIM == 0 and n_ctx % MXU_DIM == 0) else LANES
    xm, z = _proj_odd(hcat, mod_all[1], m_w_in[0].astype(BF16), n_lat)
    wq_f = _block_diag_heads(m_wq[0], dh)
    wk_f = _block_diag_heads(m_wk[0], dh) * (dh ** -0.5)
    wv_f = _block_diag_heads(m_wv[0], dh)
    wq, wk, wv = wq_f.astype(BF16), wk_f.astype(BF16), wv_f.astype(BF16)
    wg_all = jnp.concatenate([m_w_ig[0, 0], m_w_ig[0, 1], m_w_fg[0, 0], m_w_fg[0, 1]], axis=1)
    wg_all = wg_all.reshape(3, ML_HEADS, dh, 4 * ML_HEADS) * jnp.asarray([1.0, dh ** 0.5, 1.0], F32)[:, None, None, None]
    wg = jnp.transpose(wg_all, (1, 3, 0, 2)).reshape(ML_HEADS, 4 * ML_HEADS, 3 * dh).astype(BF16)
    g_bias = jnp.concatenate([m_b_ig[0, 0], m_b_ig[0, 1], m_b_fg[0, 0], m_b_fg[0, 1]])[:, None]
    gc, gm = _gate_fold(wg, wq, wk, wv)
    wqt_g = jnp.concatenate([jnp.swapaxes(wq_f, 1, 2).astype(BF16), gc], axis=1)
    wvt_g = jnp.concatenate([jnp.swapaxes(wv_f, 1, 2).astype(BF16), gm], axis=1)
    xc, km, qt, vt, g_part = _mfeat(xm, m_conv_w[0], m_conv_b[0][None, :], wk, wqt_g, wvt_g, n_lat, chunk)
    rows, colsg = _gates(g_part, g_bias, chunk)
    hs = _scan(km, qt, vt, rows, colsg, n_lat, chunk)
    h1 = _out_odd(hs, xc, z, hcat, mod_all[1], m_skip[0][None, :], m_head_g[0][None, :],
                  m_w_out[0].astype(BF16), ln_g[1, 0][None, :], ln_b[1, 0][None, :], n_lat)
    return _ffn(h1, n_lat, mod_all[1], w_ff1[1].astype(BF16), w_ff3[1].astype(BF16), w_ff2[1].astype(BF16),
                ln_g[1, 1][None, :], ln_b[1, 1][None, :], n_lat)
```

```python
import functools
import math

import numpy as np
import jax
import jax.numpy as jnp
from jax import lax
from jax.experimental import pallas as pl
from jax.experimental.pallas import tpu as pltpu

F32 = jnp.float32
BF16 = jnp.bfloat16

DA_HEADS = 6
DA_DK = 64
DA_DV = 2 * DA_DK
DA_HEAD_COLS = 4 * DA_DK + DA_DV
FN_GROUPS = 4
FN_CH = 64
FN_COLS = FN_GROUPS * FN_CH
ML_HEADS = 8
ML_BLOCK = 4
ML_CONV_W = 5
GRID_W = 64
ROPE_BASE = 10000.0
LN_EPS = 1e-5
DEPTH = 2
ALPHA = (2 * DEPTH) ** 0.25

LANES = 128
SUBLANES = 8
MXU_DIM = 256
VMEM_LIMIT_BYTES = 60 * 1024 * 1024

TOKEN_TILE = 256
CONV_TILE = 128
MOD_ROWS_PAD = 8
ATTN_Q_TILE = 512
SWEEP_UNROLL = 4


def _params(*sem):
    return pltpu.CompilerParams(dimension_semantics=sem, vmem_limit_bytes=VMEM_LIMIT_BYTES)


def _ln(x):
    mu = jnp.mean(x, axis=-1, keepdims=True)
    xc = x - mu
    var = jnp.mean(xc * xc, axis=-1, keepdims=True)
    return xc * lax.rsqrt(var + LN_EPS)


def _silu(x):
    return x * jax.nn.sigmoid(x)


def _dot(a, b):
    return jnp.dot(a, b, preferred_element_type=F32)


def _dot_nt(a, b):
    return lax.dot_general(a, b, (((1,), (1,)), ((), ())), preferred_element_type=F32)


def _full(shape):
    n = len(shape)
    return pl.BlockSpec(shape, lambda *_: (0,) * n)


def _mod_kernel(c_ref, w_ref, b_ref, o_ref):
    s = _silu(c_ref[...]).astype(BF16)
    o_ref[0] = _dot(s, w_ref[0].astype(BF16)) + b_ref[0]


def _modulation(cvec, w_mod, b_mod):
    depth, d, d6 = w_mod.shape
    r = cvec.shape[0]
    tn = d6 // 4
    return pl.pallas_call(
        _mod_kernel,
        grid=(depth, d6 // tn),
        in_specs=[pl.BlockSpec((r, d), lambda l, j: (0, 0)),
                  pl.BlockSpec((1, d, tn), lambda l, j: (l, 0, j)),
                  pl.BlockSpec((1, 1, tn), lambda l, j: (l, 0, j))],
        out_specs=pl.BlockSpec((1, r, tn), lambda l, j: (l, 0, j)),
        out_shape=jax.ShapeDtypeStruct((depth, r, d6), F32),
        compiler_params=_params("parallel", "parallel"),
        name="modulation",
    )(cvec, w_mod, b_mod.reshape(depth, 1, d6))


def _mod_spec(nb, n_lat_tiles, d):
    return pl.BlockSpec((1, 6, d), lambda b, i: (jnp.where(i < n_lat_tiles, b, nb), 0, 0))


def _stream_tile(x_ref, ctx_ref, n_lat_tiles):
    return jnp.where(pl.program_id(1) < n_lat_tiles, x_ref[0], ctx_ref[0])


def _stream_specs(tm, d, n_lat_tiles):
    return [pl.BlockSpec((1, tm, d), lambda b, i: (b, jnp.minimum(i, n_lat_tiles - 1), 0)),
            pl.BlockSpec((1, tm, d), lambda b, i: (b, jnp.maximum(i - n_lat_tiles, 0), 0))]


def _proj_even_kernel(x_ref, ctx_ref, mod_ref, w_ref, c_ref, s_ref, q_ref, k_ref, v_ref, f_ref, *, n_lat_tiles):
    x = _stream_tile(x_ref, ctx_ref, n_lat_tiles)
    u = (_ln(x) * (1.0 + mod_ref[0, 1:2, :]) + mod_ref[0, 0:1, :]).astype(BF16)
    p = _dot(u, w_ref[...])
    c = c_ref[...]
    s = s_ref[...]
    blk = DA_HEAD_COLS
    for hd in range(DA_HEADS):
        pq = p[:, hd * blk: hd * blk + LANES]
        pk = p[:, hd * blk + LANES: hd * blk + 2 * LANES]
        q_ref[0, hd] = (pq * c + pltpu.roll(pq, LANES // 2, 1) * s).astype(BF16)
        k_ref[0, hd] = (pk * c + pltpu.roll(pk, LANES // 2, 1) * s).astype(BF16)
        v_ref[0, hd] = p[:, hd * blk + 2 * LANES: (hd + 1) * blk].astype(BF16)
    f_ref[0] = p[:, DA_HEADS * blk:]


def _proj_even(x, ctx, mod, w, rope_c, rope_s):
    nb, n_lat, d = x.shape
    ntot = n_lat + ctx.shape[1]
    tm = TOKEN_TILE
    nt = ntot // tm
    ncols = w.shape[1]
    qkv_shape = jax.ShapeDtypeStruct((nb, DA_HEADS, ntot, LANES), BF16)
    qkv_spec = pl.BlockSpec((1, DA_HEADS, tm, LANES), lambda b, i: (b, 0, i, 0))
    return pl.pallas_call(
        functools.partial(_proj_even_kernel, n_lat_tiles=n_lat // tm),
        grid=(nb, nt),
        in_specs=_stream_specs(tm, d, n_lat // tm) + [
                  _mod_spec(nb, n_lat // tm, d),
                  _full((d, ncols)),
                  pl.BlockSpec((tm, LANES), lambda b, i: (i, 0)),
                  pl.BlockSpec((tm, LANES), lambda b, i: (i, 0))],
        out_specs=[qkv_spec, qkv_spec, qkv_spec,
                   pl.BlockSpec((1, tm, FN_COLS), lambda b, i: (b, i, 0))],
        out_shape=[qkv_shape, qkv_shape, qkv_shape,
                   jax.ShapeDtypeStruct((nb, ntot, FN_COLS), F32)],
        compiler_params=_params("parallel", "parallel"),
        name="proj_even",
    )(x, ctx, mod, w, rope_c, rope_s)


def _stack_maps(q):
    qf = q.astype(F32)
    lane = lax.broadcasted_iota(jnp.int32, (1, LANES), 1)
    map1 = (lane % (LANES // 2)) < (LANES // 4)
    return jnp.concatenate([jnp.where(map1, qf, 0.0), jnp.where(map1, 0.0, qf)], axis=0).astype(BF16)


def _lane_block_max(s):
    blk = s[:, :LANES]
    for c in range(1, s.shape[1] // LANES):
        blk = jnp.maximum(blk, s[:, c * LANES:(c + 1) * LANES])
    return blk


def _exp_blocks(s, mb):
    return [jnp.exp(s[:, c * LANES:(c + 1) * LANES] - mb) for c in range(s.shape[1] // LANES)]


def _diff_merge(o, lp, head_g, lam_init):
    tq = o.shape[0] // 2
    lam = (jnp.exp(jnp.sum(lp[0:1] * lp[1:2], keepdims=True))
           - jnp.exp(jnp.sum(lp[2:3] * lp[3:4], keepdims=True)) + lam_init)
    a = o[:tq] - lam * o[tq:]
    a_n = a * lax.rsqrt(jnp.mean(a * a, axis=-1, keepdims=True) + LN_EPS) * (head_g * (1.0 - lam_init))
    return a_n.astype(BF16)


def _attn_kernel(lp_ref, g_ref, qc_ref, qn_ref, kc_ref, kn_ref, v_ref, o_ref,
                 sa_sc, sb_sc, mcur_sc, mnxt_sc, l_sc, acc_sc, *, tk, nkt, lam_init):
    step = (pl.program_id(0) * pl.num_programs(1) + pl.program_id(1)) * pl.num_programs(2) + pl.program_id(2)
    rows = mcur_sc.shape[0]

    def chunk_off(j):
        return j * tk if isinstance(j, int) else pl.multiple_of(j * tk, tk)

    def score_chunk(qq, k_ref, s_ref, j):
        off = chunk_off(j)
        s = _dot_nt(qq, k_ref[0, 0, pl.ds(off, tk), :])
        s_ref[j] = s
        mnxt_sc[...] = jnp.maximum(mnxt_sc[...], _lane_block_max(s))

    def row_max():
        return jnp.broadcast_to(jnp.max(mnxt_sc[...], axis=1, keepdims=True), (rows, LANES))

    @pl.when(step == 0)
    def _():
        qq0 = _stack_maps(qc_ref[0, 0])
        mnxt_sc[...] = jnp.full(mnxt_sc.shape, -jnp.inf, F32)

        def first(j, carry):
            score_chunk(qq0, kc_ref, sa_sc, j)
            return carry

        lax.fori_loop(0, nkt, first, 0)
        mcur_sc[...] = row_max()

    qqn = _stack_maps(qn_ref[0, 0])
    mnxt_sc[...] = jnp.full(mnxt_sc.shape, -jnp.inf, F32)
    l_sc[...] = jnp.zeros(l_sc.shape, F32)
    acc_sc[...] = jnp.zeros(acc_sc.shape, F32)

    def sweep(cur_ref, nxt_ref):
        def body(j, carry):
            score_chunk(qqn, kn_ref, nxt_ref, j)
            off = chunk_off(j)
            ps = _exp_blocks(cur_ref[j], mcur_sc[...])
            part = ps[0]
            for pc in ps[1:]:
                part = part + pc
            l_sc[...] += part
            acc_sc[...] += _dot(jnp.concatenate(ps, axis=1).astype(BF16), v_ref[0, 0, pl.ds(off, tk), :])
            return carry

        tail = min(nkt, SWEEP_UNROLL + nkt % SWEEP_UNROLL)
        lax.fori_loop(0, nkt - tail, body, 0, unroll=SWEEP_UNROLL)
        for j in range(nkt - tail, nkt):
            body(j, 0)

    @pl.when(step % 2 == 0)
    def _():
        sweep(sa_sc, sb_sc)

    @pl.when(step % 2 == 1)
    def _():
        sweep(sb_sc, sa_sc)

    o = acc_sc[...] / jnp.sum(l_sc[...], axis=1, keepdims=True)
    o_ref[0] = _diff_merge(o, lp_ref[...], g_ref[...], lam_init)
    mcur_sc[...] = row_max()


def _attn_ctx_kernel(lp_ref, g_ref, q_ref, k_ref, v_ref, buf_ref, o_ref, *, lam_init):
    del buf_ref
    s = _dot_nt(_stack_maps(q_ref[0, 0]), k_ref[0, 0])
    p = jnp.exp(s - jnp.max(s, axis=1, keepdims=True))
    o = _dot(p.astype(BF16), v_ref[0, 0]) / jnp.sum(p, axis=1, keepdims=True)
    o_ref[0] = _diff_merge(o, lp_ref[...], g_ref[...], lam_init)


def _attention(q, k, v, lam_params, head_g, n_lat, lam_init):
    nb, nh, ntot, _ = q.shape
    n_ctx = ntot - n_lat
    tq = min(ATTN_Q_TILE, n_lat)
    tk = TOKEN_TILE
    nq = n_lat // tq
    nkt = ntot // tk
    rows = 2 * tq
    kv_spec = pl.BlockSpec((1, 1, ntot, LANES), lambda b, h, i: (b, h, 0, 0))

    def following(b, h, i):
        s = jnp.minimum((b * nh + h) * nq + i + 1, nb * nh * nq - 1)
        return s // (nh * nq), (s // nq) % nh, s % nq

    def qn_map(b, h, i):
        b2, h2, i2 = following(b, h, i)
        return b2, h2, i2, 0

    def kn_map(b, h, i):
        b2, h2, _ = following(b, h, i)
        return b2, h2, 0, 0

    a_lat = pl.pallas_call(
        functools.partial(_attn_kernel, tk=tk, nkt=nkt, lam_init=lam_init),
        grid=(nb, nh, nq),
        in_specs=[_full(lam_params.shape), _full(head_g.shape),
                  pl.BlockSpec((1, 1, tq, LANES), lambda b, h, i: (b, h, i, 0)),
                  pl.BlockSpec((1, 1, tq, LANES), qn_map),
                  kv_spec, pl.BlockSpec((1, 1, ntot, LANES), kn_map), kv_spec],
        out_specs=pl.BlockSpec((1, tq, LANES), lambda b, h, i: (b, i, h)),
        out_shape=jax.ShapeDtypeStruct((nb, ntot, nh * LANES), BF16),
        scratch_shapes=[pltpu.VMEM((nkt, rows, tk), F32)] * 2 + [pltpu.VMEM((rows, LANES), F32)] * 4,
        compiler_params=_params("arbitrary", "arbitrary", "arbitrary"),
        name="diff_attention",
    )(lam_params, head_g, q, q, k, k, v)
    cblk = n_lat // n_ctx
    ctx_spec = pl.BlockSpec((1, 1, n_ctx, LANES), lambda b, h: (b, h, cblk, 0))
    return pl.pallas_call(
        functools.partial(_attn_ctx_kernel, lam_init=lam_init),
        grid=(nb, nh),
        in_specs=[_full(lam_params.shape), _full(head_g.shape), ctx_spec, ctx_spec, ctx_spec,
                  pl.BlockSpec(memory_space=pl.ANY)],
        out_specs=pl.BlockSpec((1, n_ctx, LANES), lambda b, h: (b, cblk, h)),
        out_shape=jax.ShapeDtypeStruct(a_lat.shape, a_lat.dtype),
        input_output_aliases={5: 0},
        compiler_params=_params("parallel", "parallel"),
        name="diff_attention_ctx",
    )(lam_params, head_g, q, k, v, a_lat)


def _fourier_factor(n):
    a = 1 << (int(math.log2(n)) // 2)
    assert a * (n // a) == n and a % SUBLANES == 0 and (n // a) % SUBLANES == 0
    return a, n // a


def _fourier_tables(n):
    a, bn = _fourier_factor(n)
    j = np.arange(FN_CH)
    ang = 2.0 * np.pi * np.outer(j, j) / FN_CH
    eye = np.eye(FN_GROUPS)
    wc = np.concatenate([np.kron(eye, np.cos(ang)), -np.kron(eye, np.sin(ang))], axis=1)
    ia = np.arange(a)
    ang_a = 2.0 * np.pi * np.outer(ia, ia) / a
    ca, sa = np.cos(ang_a), np.sin(ang_a)
    m1 = np.block([[ca, sa], [-sa, ca]])
    ib = np.arange(bn)
    ang_t = 2.0 * np.pi * np.outer(ia, ib) / n
    ang_b = 2.0 * np.pi * np.outer(ib, ib) / bn
    m3 = np.concatenate([np.cos(ang_b), np.sin(ang_b)], axis=1) / math.sqrt(n * FN_CH)
    tw_c = jnp.repeat(jnp.asarray(np.cos(ang_t), F32), FN_COLS, axis=1)
    tw_s = jnp.repeat(jnp.asarray(np.sin(ang_t), F32), FN_COLS, axis=1)
    return (jnp.asarray(wc, BF16), jnp.asarray(m1, BF16), tw_c, tw_s, jnp.asarray(m3, BF16))


def _fourier_chan_kernel(x_ref, w_ref, o_ref):
    g = _dot(x_ref[0].astype(BF16), w_ref[...])
    o_ref[0, 0] = g[:, :FN_COLS].astype(BF16)
    o_ref[0, 1] = g[:, FN_COLS:].astype(BF16)


def _fourier_stage1_kernel(g_ref, m_ref, c_ref, s_ref, o_ref, *, a, bb):
    t1 = _dot(m_ref[...], g_ref[0])
    tr, ti = t1[:a], t1[a:]
    c = c_ref[...]
    s = s_ref[...]
    t2r = tr * c + ti * s
    t2i = ti * c - tr * s
    for b in range(bb):
        o_ref[0, 0, b] = t2r[:, b * FN_COLS:(b + 1) * FN_COLS].astype(BF16)
        o_ref[0, 1, b] = t2i[:, b * FN_COLS:(b + 1) * FN_COLS].astype(BF16)


def _fourier_stage2_kernel(t_ref, m_ref, *rest):
    o_ref = rest[-1]
    o_ref[0] = _dot(m_ref[...], t_ref[0])


def _fourier_mix(fcat, row0, n, out_buf):
    nb, ntot, _ = fcat.shape
    a, bn = _fourier_factor(n)
    wc, m1, tw_c, tw_s, m3 = _fourier_tables(n)
    tm = min(TOKEN_TILE, n)
    blk0 = row0 // tm
    g = pl.pallas_call(
        _fourier_chan_kernel,
        grid=(nb, n // tm),
        in_specs=[pl.BlockSpec((1, tm, FN_COLS), lambda b, i: (b, blk0 + i, 0)),
                  _full(wc.shape)],
        out_specs=pl.BlockSpec((1, 2, tm, FN_COLS), lambda b, i: (b, 0, i, 0)),
        out_shape=jax.ShapeDtypeStruct((nb, 2, n, FN_COLS), BF16),
        compiler_params=_params("parallel", "parallel"),
        name="fourier_chan",
    )(fcat, wc)
    bb = min(8, bn)
    tc = bb * FN_COLS
    t2 = pl.pallas_call(
        functools.partial(_fourier_stage1_kernel, a=a, bb=bb),
        grid=(bn // bb, nb),
        in_specs=[pl.BlockSpec((1, 2 * a, tc), lambda j, b: (b, 0, j)),
                  _full(m1.shape),
                  pl.BlockSpec((a, tc), lambda j, b: (0, j)),
                  pl.BlockSpec((a, tc), lambda j, b: (0, j))],
        out_specs=pl.BlockSpec((1, 2, bb, a, FN_COLS), lambda j, b: (b, 0, j, 0, 0)),
        out_shape=jax.ShapeDtypeStruct((nb, 2, bn, a, FN_COLS), BF16),
        compiler_params=_params("parallel", "parallel"),
        name="fourier_stage1",
    )(g.reshape(nb, 2 * a, bn * FN_COLS), m1, tw_c, tw_s)
    assert ntot % a == 0 and (row0 // a) % bn == 0
    tc2 = min(8, a) * FN_COLS
    out_rows = ntot // a
    in_specs = [pl.BlockSpec((1, 2 * bn, tc2), lambda b, j: (b, 0, j)), _full(m3.shape)]
    args = [t2.reshape(nb, 2 * bn, a * FN_COLS), m3]
    aliases = {}
    if out_buf is not None:
        in_specs.append(pl.BlockSpec(memory_space=pl.ANY))
        args.append(out_buf.reshape(nb, out_rows, a * FN_COLS))
        aliases = {2: 0}
    rblk = (row0 // a) // bn
    out = pl.pallas_call(
        _fourier_stage2_kernel,
        grid=(nb, a * FN_COLS // tc2),
        in_specs=in_specs,
        out_specs=pl.BlockSpec((1, bn, tc2), lambda b, j: (b, rblk, j)),
        out_shape=jax.ShapeDtypeStruct((nb, out_rows, a * FN_COLS), F32),
        input_output_aliases=aliases,
        compiler_params=_params("parallel", "parallel"),
        name="fourier_stage2",
    )(*args)
    return out.reshape(nb, ntot, FN_COLS)


def _post_norm(h, y, g, b):
    return _ln(ALPHA * h + y) * g + b


def _out_even_kernel(a_ref, f_ref, x_ref, ctx_ref, mod_ref, w_ref, g_ref, b_ref, o_ref, *, n_lat_tiles):
    na = a_ref.shape[-1]
    y = _dot(a_ref[0], w_ref[:na, :]) + _dot(f_ref[0].astype(BF16), w_ref[na:, :])
    h = _stream_tile(x_ref, ctx_ref, n_lat_tiles)
    o_ref[0] = _post_norm(h, mod_ref[0, 2:3, :] * y, g_ref[...], b_ref[...])


def _out_even(a_n, fm, x, ctx, mod, w_out, ln_g, ln_b):
    nb, n_lat, d = x.shape
    ntot = n_lat + ctx.shape[1]
    tm = TOKEN_TILE
    na = a_n.shape[-1]
    tok = lambda c: pl.BlockSpec((1, tm, c), lambda b, i: (b, i, 0))
    return pl.pallas_call(
        functools.partial(_out_even_kernel, n_lat_tiles=n_lat // tm),
        grid=(nb, ntot // tm),
        in_specs=[tok(na), tok(FN_COLS)] + _stream_specs(tm, d, n_lat // tm) + [
                  _mod_spec(nb, n_lat // tm, d), _full(w_out.shape), _full((1, d)), _full((1, d))],
        out_specs=tok(d),
        out_shape=jax.ShapeDtypeStruct((nb, ntot, d), F32),
        compiler_params=_params("parallel", "parallel"),
        name="out_even",
    )(a_n, fm, x, ctx, mod, w_out, ln_g, ln_b)


def _ffn_kernel(h_ref, mod_ref, w1_ref, w3_ref, w2_ref, g_ref, b_ref, o_ref):
    h = h_ref[0]
    u = (_ln(h) * (1.0 + mod_ref[0, 4:5, :]) + mod_ref[0, 3:4, :]).astype(BF16)
    hid = (_silu(_dot(u, w1_ref[...])) * _dot(u, w3_ref[...])).astype(BF16)
    y = _dot(hid, w2_ref[...])
    o_ref[0] = _post_norm(h, mod_ref[0, 5:6, :] * y, g_ref[...], b_ref[...])


def _ffn(h, n_rows, mod, w1, w3, w2, ln_g, ln_b, n_lat):
    nb, _, d = h.shape
    tm = TOKEN_TILE
    tok = pl.BlockSpec((1, tm, d), lambda b, i: (b, i, 0))
    resident = lambda s: pl.BlockSpec(s, lambda b, i: (0, 0), pipeline_mode=pl.Buffered(1))
    return pl.pallas_call(
        _ffn_kernel,
        grid=(nb, n_rows // tm),
        in_specs=[tok, _mod_spec(nb, n_lat // tm, d),
                  resident(w1.shape), resident(w3.shape), resident(w2.shape),
                  _full((1, d)), _full((1, d))],
        out_specs=tok,
        out_shape=jax.ShapeDtypeStruct((nb, n_rows, d), F32),
        compiler_params=_params("parallel", "parallel"),
        name="ffn",
    )(h, mod, w1, w3, w2, ln_g, ln_b)


def _proj_odd_kernel(h_ref, mod_ref, w_ref, xm_ref, z_ref):
    x = h_ref[0]
    u = (_ln(x) * (1.0 + mod_ref[0, 1:2, :]) + mod_ref[0, 0:1, :]).astype(BF16)
    p = _dot(u, w_ref[...])
    inner = xm_ref.shape[-1]
    xm_ref[0] = p[:, :inner]
    z_ref[0] = p[:, inner:]


def _proj_odd(hcat, mod, w, n_lat):
    nb, ntot, d = hcat.shape
    tm = TOKEN_TILE
    inner = w.shape[1] // 2
    out_spec = pl.BlockSpec((1, tm, inner), lambda b, i: (b, i, 0))
    out_shape = jax.ShapeDtypeStruct((nb, ntot, inner), F32)
    return pl.pallas_call(
        _proj_odd_kernel,
        grid=(nb, ntot // tm),
        in_specs=[pl.BlockSpec((1, tm, d), lambda b, i: (b, i, 0)),
                  _mod_spec(nb, n_lat // tm, d),
                  _full(w.shape)],
        out_specs=[out_spec, out_spec],
        out_shape=[out_shape, out_shape],
        compiler_params=_params("parallel", "parallel"),
        name="proj_odd",
    )(hcat, mod, w)


def _gate_fold_kernel(wg_ref, wq_ref, wk_ref, wv_ref, gc_ref, gm_ref):
    dh = wq_ref.shape[-1]
    wg = wg_ref[0]
    gc_ref[0] = (_dot_nt(wg[:, :dh], wq_ref[0]) + _dot_nt(wg[:, dh:2 * dh], wk_ref[0])).astype(BF16)
    gm_ref[0] = _dot_nt(wg[:, 2 * dh:], wv_ref[0]).astype(BF16)


def _gate_fold(wg, wq, wk, wv):
    nh, n_gates, _ = wg.shape
    dh = wq.shape[-1]
    wblk = pl.BlockSpec((1, dh, dh), lambda h: (h, 0, 0))
    gspec = pl.BlockSpec((1, n_gates, dh), lambda h: (h, 0, 0))
    gshape = jax.ShapeDtypeStruct((nh, n_gates, dh), BF16)
    return pl.pallas_call(
        _gate_fold_kernel,
        grid=(nh,),
        in_specs=[pl.BlockSpec((1, n_gates, 3 * dh), lambda h: (h, 0, 0)), wblk, wblk, wblk],
        out_specs=[gspec, gspec],
        out_shape=[gshape, gshape],
        compiler_params=_params("parallel"),
        name="mlstm_gate_fold",
    )(wg, wq, wk, wv)


def _mfeat_kernel(xm_ref, cw_ref, cb_ref, wk_ref, wqt_ref, wvt_ref,
                  xc_ref, k_ref, qt_ref, vt_ref, gr_ref, *, n_lat, chunk, row_group):
    ntot = xm_ref.shape[1]
    dh = wk_ref.shape[-1]
    ct = CONV_TILE
    half = ML_CONV_W // 2
    cw = cw_ref[...]
    cb = cb_ref[...]

    def conv_tile(t, carry):
        r0 = pl.multiple_of(t * ct, ct)
        at_start = jnp.logical_or(r0 == 0, r0 == n_lat)
        at_end = jnp.logical_or(r0 + ct == n_lat, r0 + ct == ntot)
        p0 = pl.multiple_of(jnp.maximum(r0 - SUBLANES, 0), SUBLANES)
        n0 = pl.multiple_of(jnp.minimum(r0 + ct, ntot - SUBLANES), SUBLANES)
        prev = xm_ref[0, pl.ds(p0, SUBLANES), :]
        nxt = xm_ref[0, pl.ds(n0, SUBLANES), :]
        prev = jnp.where(at_start, 0.0, prev)
        nxt = jnp.where(at_end, 0.0, nxt)
        xe = jnp.concatenate([prev, xm_ref[0, pl.ds(r0, ct), :], nxt], axis=0)
        acc = cb
        for dd in range(ML_CONV_W):
            lo = SUBLANES - half + dd
            acc = acc + cw[dd:dd + 1, :] * xe[lo:lo + ct, :]
        xc_ref[0, pl.ds(r0, ct), :] = _silu(acc)
        return carry

    lax.fori_loop(0, ntot // ct, conv_tile, 0)

    for g in range(ntot // row_group):
        rows = pl.ds(g * row_group, row_group)
        k_ref[0, rows, :] = _dot(xc_ref[0, rows, :].astype(BF16), wk_ref[0]).astype(BF16)

    def t_chunk(c, carry):
        r0 = pl.multiple_of(c * chunk, chunk)
        tq = _dot_nt(wqt_ref[0], xc_ref[0, pl.ds(r0, chunk), :].astype(BF16))
        tv = _dot_nt(wvt_ref[0], xm_ref[0, pl.ds(r0, chunk), :].astype(BF16))
        qt_ref[0, 0, c] = tq[:dh].astype(BF16)
        vt_ref[0, 0, c] = tv[:dh].astype(BF16)
        gr_ref[0, 0, c] = tq[dh:] + tv[dh:]
        return carry

    lax.fori_loop(0, ntot // chunk, t_chunk, 0, unroll=2)


def _mfeat(xm, conv_w, conv_b, wk, wqt_g, wvt_g, n_lat, chunk):
    nb, ntot, inner = xm.shape
    nh = ML_HEADS
    dh = inner // nh
    nct = ntot // chunk
    n_gates = wqt_g.shape[1] - dh
    ng = 2 if ntot % (2 * SUBLANES) == 0 else 1
    kern = functools.partial(_mfeat_kernel, n_lat=n_lat, chunk=chunk, row_group=ntot // ng)
    seq = pl.BlockSpec((1, ntot, dh), lambda b, h: (b, 0, h))
    wstack = pl.BlockSpec((1, dh + n_gates, dh), lambda b, h: (h, 0, 0))
    tspec = pl.BlockSpec((1, 1, nct, dh, chunk), lambda b, h: (b, h, 0, 0, 0))
    tshape = jax.ShapeDtypeStruct((nb, nh, nct, dh, chunk), BF16)
    return pl.pallas_call(
        kern,
        grid=(nb, nh),
        in_specs=[seq,
                  pl.BlockSpec((ML_CONV_W, dh), lambda b, h: (0, h)),
                  pl.BlockSpec((1, dh), lambda b, h: (0, h)),
                  pl.BlockSpec((1, dh, dh), lambda b, h: (h, 0, 0)), wstack, wstack],
        out_specs=[seq, seq, tspec, tspec,
                   pl.BlockSpec((1, 1, nct, n_gates, chunk), lambda b, h: (b, h, 0, 0, 0))],
        out_shape=[jax.ShapeDtypeStruct((nb, ntot, inner), F32),
                   jax.ShapeDtypeStruct((nb, ntot, inner), BF16), tshape, tshape,
                   jax.ShapeDtypeStruct((nb, nh, nct, n_gates, chunk), F32)],
        compiler_params=_params("parallel", "parallel"),
        name="mlstm_features",
    )(xm, conv_w, conv_b, wk, wqt_g, wvt_g)


def _split3(x):
    hi = x.astype(BF16)
    r1 = x - hi.astype(F32)
    mid = r1.astype(BF16)
    lo = (r1 - mid.astype(F32)).astype(BF16)
    return hi, mid, lo


def _running_max(x, reverse):
    n = x.shape[1]
    lane = lax.broadcasted_iota(jnp.int32, x.shape, 1)
    k = 1
    while k < n:
        if reverse:
            shifted = jnp.where(lane < n - k, pltpu.roll(x, n - k, 1), -jnp.inf)
        else:
            shifted = jnp.where(lane >= k, pltpu.roll(x, k, 1), -jnp.inf)
        x = jnp.maximum(x, shifted)
        k *= 2
    return x


def _gates_kernel(g_ref, b_ref, rows_ref, cols_ref):
    nh = ML_HEADS
    nct, chunk = g_ref.shape[2], g_ref.shape[-1]
    bias = b_ref[...]
    li_f, li_b, lf_f, lf_b = [], [], [], []
    for c in range(nct):
        pre = jnp.sum(g_ref[0, :, c], axis=0) + bias
        xf = pre[2 * nh:]
        log_f = jnp.minimum(xf, 0.0) - jnp.log1p(jnp.exp(-jnp.abs(xf)))
        li_f.append(pre[:nh])
        li_b.append(pre[nh:2 * nh])
        lf_f.append(log_f[:nh])
        lf_b.append(log_f[nh:])
    s_idx = lax.broadcasted_iota(jnp.int32, (chunk, chunk), 0)
    t_idx = lax.broadcasted_iota(jnp.int32, (chunk, chunk), 1)
    tri_f = jnp.where(s_idx <= t_idx, 1.0, 0.0).astype(BF16)
    tri_b = jnp.where(s_idx >= t_idx, 1.0, 0.0).astype(BF16)
    cum_f = sum(_dot(p, tri_f) for p in _split3(jnp.concatenate(lf_f, axis=0)))
    cum_b = sum(_dot(p, tri_b) for p in _split3(jnp.concatenate(lf_b, axis=0)))
    gap_f = jnp.concatenate(li_f, axis=0) - cum_f
    gap_b = jnp.concatenate(li_b, axis=0) - cum_b
    top_f = _running_max(gap_f, False)
    top_b = _running_max(gap_b, True)
    pad = jnp.zeros((LANES - 2 * nh, chunk), F32)
    for c in range(nct):
        r = slice(c * nh, (c + 1) * nh)
        rows_ref[0, c] = jnp.concatenate([gap_f[r], gap_b[r], top_f[r], top_b[r], cum_f[r], cum_b[r]], axis=0)
        gap_t = jnp.transpose(jnp.concatenate([gap_f[r], gap_b[r], pad], axis=0))
        cols_ref[0, c * chunk:(c + 1) * chunk, :] = gap_t[:, :2 * nh]


def _gates(g_part, bias, chunk):
    nb, nh, nct, n_gates, _ = g_part.shape
    ntot = nct * chunk
    return pl.pallas_call(
        _gates_kernel,
        grid=(nb,),
        in_specs=[pl.BlockSpec((1, nh, nct, n_gates, chunk), lambda b: (b, 0, 0, 0, 0)),
                  _full(bias.shape)],
        out_specs=[pl.BlockSpec((1, nct, 6 * nh, chunk), lambda b: (b, 0, 0, 0)),
                   pl.BlockSpec((1, ntot, 2 * nh), lambda b: (b, 0, 0))],
        out_shape=[jax.ShapeDtypeStruct((nb, nct, 6 * nh, chunk), F32),
                   jax.ShapeDtypeStruct((nb, ntot, 2 * nh), F32)],
        compiler_params=_params("parallel"),
        name="mlstm_gates",
    )(g_part, bias)


STATE_PAD_ROWS = 16


def _scan_kernel(k_ref, qt_ref, vt_ref, rows_ref, cols_ref, hs_ref, c_sc, n_sc, m_sc,
                 *, n_lat_chunks, n_ctx_chunks, chunk):
    nh = ML_HEADS
    h = pl.program_id(1)
    dh = c_sc.shape[-1]
    c_sc[...] = jnp.zeros(c_sc.shape, F32)
    n_sc[...] = jnp.zeros(n_sc.shape, F32)
    m_sc[...] = jnp.zeros(m_sc.shape, F32)
    s_idx = lax.broadcasted_iota(jnp.int32, (chunk, chunk), 0)
    t_idx = lax.broadcasted_iota(jnp.int32, (chunk, chunk), 1)
    gate_lane = lax.broadcasted_iota(jnp.int32, (1, 2 * nh), 1)

    def step(d, cidx, with_out):
        off = pl.multiple_of(cidx * chunk, chunk)
        kc = k_ref[0, pl.ds(off, chunk), :]
        qt = qt_ref[0, 0, cidx]
        vt = vt_ref[0, 0, cidx]
        gap_r = rows_ref[0, cidx, pl.ds(d * nh + h, 1), :]
        top_r = rows_ref[0, cidx, pl.ds((2 + d) * nh + h, 1), :]
        cum_r = rows_ref[0, cidx, pl.ds((4 + d) * nh + h, 1), :]
        m = m_sc[d]
        ct = c_sc[d]
        nrow = n_sc[d]
        last = chunk - 1 if d == 0 else 0
        b_end = cum_r[:, last:last + 1]
        top_end = top_r[:, last:last + 1]
        if with_out is not None:
            ctile = cols_ref[0, pl.ds(off, chunk), :]
            gap_c = jnp.sum(jnp.where(gate_lane == d * nh + h, ctile, 0.0), axis=1, keepdims=True)
            lift = jnp.maximum(top_r, m)
            ordered = (s_idx <= t_idx) if d == 0 else (s_idx >= t_idx)
            dw_t = jnp.exp(jnp.where(ordered, gap_c - lift, -jnp.inf))
            iw = jnp.exp(m - lift)
            nb16 = jnp.broadcast_to(nrow.astype(BF16), (STATE_PAD_ROWS, dh))
            a = _dot(jnp.concatenate([kc, ct.astype(BF16), nb16], axis=0), qt)
            s_t = a[:chunk] * dw_t
            num_t = iw * a[chunk:chunk + dh] + _dot(vt, s_t.astype(BF16))
            den = iw * a[chunk + dh:chunk + dh + 1] + jnp.sum(s_t, axis=0, keepdims=True)
            floor = jnp.exp(-(cum_r + lift))
            h_t = num_t * (1.0 / jnp.maximum(jnp.abs(den), floor))
            hs_ref[0, 0, cidx] = h_t if with_out == "assign" else hs_ref[0, 0, cidx] + h_t
        mx = jnp.maximum(m, top_end)
        keep = jnp.exp(m - mx)
        w_r = jnp.exp(gap_r - mx)
        vw = (vt.astype(F32) * w_r).astype(BF16)
        wb16 = jnp.broadcast_to(w_r.astype(BF16), (STATE_PAD_ROWS, chunk))
        upd = _dot(jnp.concatenate([vw, wb16], axis=0), kc)
        c_sc[d] = keep * ct + upd[:dh]
        n_sc[d] = keep * nrow + upd[dh:dh + 1]
        m_sc[d] = b_end + mx

    def ctx_body(c, carry):
        step(0, n_lat_chunks + c, None)
        step(1, n_lat_chunks + n_ctx_chunks - 1 - c, None)
        return carry

    def lat_body(mode, c, carry):
        step(0, c, mode)
        step(1, n_lat_chunks - 1 - c, mode)
        return carry

    lax.fori_loop(0, n_ctx_chunks, ctx_body, 0)
    half = n_lat_chunks // 2
    lax.fori_loop(0, half, functools.partial(lat_body, "assign"), 0, unroll=2)
    lax.fori_loop(half, n_lat_chunks, functools.partial(lat_body, "add"), 0, unroll=2)


def _scan(k, qt, vt, rows, cols, n_lat, chunk):
    nb, ntot, inner = k.shape
    nh = ML_HEADS
    dh = inner // nh
    nct = ntot // chunk
    ncl = n_lat // chunk
    assert ncl % 2 == 0
    kern = functools.partial(_scan_kernel, n_lat_chunks=ncl, n_ctx_chunks=nct - ncl, chunk=chunk)
    tspec = pl.BlockSpec((1, 1, nct, dh, chunk), lambda b, h: (b, h, 0, 0, 0))
    return pl.pallas_call(
        kern,
        grid=(nb, nh),
        in_specs=[pl.BlockSpec((1, ntot, dh), lambda b, h: (b, 0, h)), tspec, tspec,
                  pl.BlockSpec((1, nct, rows.shape[2], chunk), lambda b, h: (b, 0, 0, 0)),
                  pl.BlockSpec((1, ntot, cols.shape[2]), lambda b, h: (b, 0, 0))],
        out_specs=pl.BlockSpec((1, 1, ncl, dh, chunk), lambda b, h: (b, h, 0, 0, 0)),
        out_shape=jax.ShapeDtypeStruct((nb, nh, ncl, dh, chunk), F32),
        scratch_shapes=[pltpu.VMEM((2, dh, dh), F32), pltpu.VMEM((2, 1, dh), F32),
                        pltpu.VMEM((2, 1, 1), F32)],
        compiler_params=_params("parallel", "parallel"),
        name="mlstm_scan",
    )(k, qt, vt, rows, cols)


def _out_odd_kernel(hs_ref, xc_ref, z_ref, h_ref, mod_ref, skip_ref, hg_ref, w_ref, g_ref, b_ref, o_ref):
    parts = []
    for hd in range(ML_HEADS):
        ht = hs_ref[0, hd, 0]
        mu = jnp.mean(ht, axis=0, keepdims=True)
        hc = ht - mu
        var = jnp.mean(hc * hc, axis=0, keepdims=True)
        parts.append(jnp.transpose(hc * lax.rsqrt(var + LN_EPS)))
    hn = jnp.concatenate(parts, axis=1)
    y = (hn * hg_ref[...] + skip_ref[...] * xc_ref[0]) * _silu(z_ref[0])
    yo = _dot(y.astype(BF16), w_ref[...])
    o_ref[0] = _post_norm(h_ref[0], mod_ref[0, 2:3, :] * yo, g_ref[...], b_ref[...])


def _out_odd(hs, xc, z, hcat, mod, skip, head_g, w_out, ln_g, ln_b, n_lat):
    nb, _, d = hcat.shape
    inner = xc.shape[-1]
    _, nh, _, dh, tm = hs.shape
    tok = lambda c: pl.BlockSpec((1, tm, c), lambda b, i: (b, i, 0))
    return pl.pallas_call(
        _out_odd_kernel,
        grid=(nb, n_lat // tm),
        in_specs=[pl.BlockSpec((1, nh, 1, dh, tm), lambda b, i: (b, 0, i, 0, 0)),
                  tok(inner), tok(inner), tok(d),
                  pl.BlockSpec((1, 6, d), lambda b, i: (b, 0, 0)),
                  _full((1, inner)), _full((1, inner)), _full(w_out.shape),
                  _full((1, d)), _full((1, d))],
        out_specs=tok(d),
        out_shape=jax.ShapeDtypeStruct((nb, n_lat, d), F32),
        compiler_params=_params("parallel", "parallel"),
        name="out_odd",
    )(hs, xc, z, hcat, mod, skip, head_g, w_out, ln_g, ln_b)


def _even_in_columns():
    cols, scale = [], []
    quarter = DA_DK // 2
    for hd in range(DA_HEADS):
        base = hd * DA_HEAD_COLS
        for blk in range(2):
            b0 = base + blk * 2 * DA_DK
            for half in range(2):
                for m in range(2):
                    cols += [b0 + m * DA_DK + half * quarter + j for j in range(quarter)]
            scale += [DA_DK ** -0.5 if blk == 0 else 1.0] * (2 * DA_DK)
        cols += list(range(base + 4 * DA_DK, base + DA_HEAD_COLS))
        scale += [1.0] * DA_DV
    cols += list(range(DA_HEADS * DA_HEAD_COLS, DA_HEADS * DA_HEAD_COLS + FN_COLS))
    scale += [1.0] * FN_COLS
    return np.asarray(cols, np.int32), np.asarray(scale, np.float32)


def _rope_tables(n_lat, ntot):
    rows = n_lat // GRID_W
    row = jnp.repeat(jnp.arange(rows, dtype=F32), GRID_W)
    col = jnp.tile(jnp.arange(GRID_W, dtype=F32), rows)
    n_freq = DA_DK // 4
    inv_freq = ROPE_BASE ** (-jnp.arange(n_freq, dtype=F32) / n_freq)
    ang = jnp.concatenate([row[:, None] * inv_freq, col[:, None] * inv_freq], -1)
    cos, sin = jnp.cos(ang), jnp.sin(ang)
    c = jnp.concatenate([cos, cos, cos, cos], axis=1)
    s = jnp.concatenate([-sin, -sin, sin, sin], axis=1)
    n_ctx = ntot - n_lat
    c = jnp.concatenate([c, jnp.ones((n_ctx, LANES), F32)], axis=0)
    s = jnp.concatenate([s, jnp.zeros((n_ctx, LANES), F32)], axis=0)
    return c, s


def _block_diag_heads(w, dh):
    nblk = w.shape[0]
    per_head = dh // ML_BLOCK
    wh = w.reshape(nblk // per_head, per_head, ML_BLOCK, ML_BLOCK)
    eye = jnp.eye(per_head, dtype=w.dtype)
    dense = jnp.einsum('hgij,gk->hgikj', wh, eye)
    return dense.reshape(nblk // per_head, dh, dh)


def kernel(x, c, ctx, c_ctx, w_mod, b_mod, ln_g, ln_b, w_ff1, w_ff3, w_ff2, a_w_in, a_w_out, da_lq1, da_lk1, da_lq2, da_lk2, da_head_g, m_w_in, m_w_out, m_conv_w, m_conv_b, m_wq, m_wk, m_wv, m_w_ig, m_b_ig, m_w_fg, m_b_fg, m_skip, m_head_g):
    nb, n_lat, d = x.shape
    n_ctx = ctx.shape[1]
    ntot = n_lat + n_ctx
    assert w_mod.shape[0] == DEPTH == 2
    assert n_lat % TOKEN_TILE == 0 and n_ctx % TOKEN_TILE == 0 and n_lat % n_ctx == 0 and n_lat % GRID_W == 0

    r = -(-(nb + 1) // MOD_ROWS_PAD) * MOD_ROWS_PAD
    cvec = jnp.concatenate([c, c_ctx[None, :], jnp.zeros((r - nb - 1, d), F32)], axis=0)
    mod_all = _modulation(cvec, w_mod, b_mod).reshape(DEPTH, r, 6, d)

    lam_init0 = 0.8 - 0.6 * math.exp(-0.3 * 0)
    cols, colscale = _even_in_columns()
    w_in = (a_w_in[0][:, cols] * colscale[None, :]).astype(BF16)
    rope_c, rope_s = _rope_tables(n_lat, ntot)
    q, k, v, f = _proj_even(x, ctx, mod_all[0], w_in, rope_c, rope_s)
    lam_params = jnp.stack([da_lq1[0], da_lk1[0], da_lq2[0], da_lk2[0]], axis=0)
    a_n = _attention(q, k, v, lam_params, da_head_g[0][None, :], n_lat, lam_init0)
    fm = _fourier_mix(f, 0, n_lat, None)
    fm = _fourier_mix(f, n_lat, n_ctx, fm)
    h1 = _out_even(a_n, fm, x, ctx, mod_all[0], a_w_out[0].astype(BF16),
                   ln_g[0, 0][None, :], ln_b[0, 0][None, :])
    hcat = _ffn(h1, ntot, mod_all[0], w_ff1[0].astype(BF16), w_ff3[0].astype(BF16), w_ff2[0].astype(BF16),
                ln_g[0, 1][None, :], ln_b[0, 1][None, :], n_lat)

    inner = m_w_in.shape[2] // 2
    dh = inner // ML_HEADS
    chunk = MXU_DIM if (n_lat % MXU_DIM == 0 and n_ctx % MXU_DIM == 0) else LANES
    xm, z = _proj_odd(hcat, mod_all[1], m_w_in[0].astype(BF16), n_lat)
    wq_f = _block_diag_heads(m_wq[0], dh)
    wk_f = _block_diag_heads(m_wk[0], dh) * (dh ** -0.5)
    wv_f = _block_diag_heads(m_wv[0], dh)
    wq, wk, wv = wq_f.astype(BF16), wk_f.astype(BF16), wv_f.astype(BF16)
    wg_all = jnp.concatenate([m_w_ig[0, 0], m_w_ig[0, 1], m_w_fg[0, 0], m_w_fg[0, 1]], axis=1)
    wg_all = wg_all.reshape(3, ML_HEADS, dh, 4 * ML_HEADS) * jnp.asarray([1.0, dh ** 0.5, 1.0], F32)[:, None, None, None]
    wg = jnp.transpose(wg_all, (1, 3, 0, 2)).reshape(ML_HEADS, 4 * ML_HEADS, 3 * dh).astype(BF16)
    g_bias = jnp.concatenate([m_b_ig[0, 0], m_b_ig[0, 1], m_b_fg[0, 0], m_b_fg[0, 1]])[:, None]
    gc, gm = _gate_fold(wg, wq, wk, wv)
    wqt_g = jnp.concatenate([jnp.swapaxes(wq_f, 1, 2).astype(BF16), gc], axis=1)
    wvt_g = jnp.concatenate([jnp.swapaxes(wv_f, 1, 2).astype(BF16), gm], axis=1)
    xc, km, qt, vt, g_part = _mfeat(xm, m_conv_w[0], m_conv_b[0][None, :], wk, wqt_g, wvt_g, n_lat, chunk)
    rows, colsg = _gates(g_part, g_bias, chunk)
    hs = _scan(km, qt, vt, rows, colsg, n_lat, chunk)
    h1 = _out_odd(hs, xc, z, hcat, mod_all[1], m_skip[0][None, :], m_head_g[0][None, :],
                  m_w_out[0].astype(BF16), ln_g[1, 0][None, :], ln_b[1, 0][None, :], n_lat)
    return _ffn(h1, n_lat, mod_all[1], w_ff1[1].astype(BF16), w_ff3[1].astype(BF16), w_ff2[1].astype(BF16),
                ln_g[1, 1][None, :], ln_b[1, 1][None, :], n_lat)
```

```python
import functools
import math

import numpy as np
import jax
import jax.numpy as jnp
from jax import lax
from jax.experimental import pallas as pl
from jax.experimental.pallas import tpu as pltpu

F32 = jnp.float32
BF16 = jnp.bfloat16

DA_HEADS = 6
DA_DK = 64
DA_DV = 2 * DA_DK
DA_HEAD_COLS = 4 * DA_DK + DA_DV
FN_GROUPS = 4
FN_CH = 64
FN_COLS = FN_GROUPS * FN_CH
ML_HEADS = 8
ML_BLOCK = 4
ML_CONV_W = 5
GRID_W = 64
ROPE_BASE = 10000.0
LN_EPS = 1e-5
DEPTH = 2
ALPHA = (2 * DEPTH) ** 0.25

LANES = 128
SUBLANES = 8
MXU_DIM = 256
VMEM_LIMIT_BYTES = 60 * 1024 * 1024

TOKEN_TILE = 256
CONV_TILE = 128
MOD_ROWS_PAD = 8
ATTN_Q_TILE = 512
SWEEP_UNROLL = 4


def _params(*sem):
    return pltpu.CompilerParams(dimension_semantics=sem, vmem_limit_bytes=VMEM_LIMIT_BYTES)


def _ln(x):
    mu = jnp.mean(x, axis=-1, keepdims=True)
    xc = x - mu
    var = jnp.mean(xc * xc, axis=-1, keepdims=True)
    return xc * lax.rsqrt(var + LN_EPS)


def _silu(x):
    return x * jax.nn.sigmoid(x)


def _dot(a, b):
    return jnp.dot(a, b, preferred_element_type=F32)


def _dot_nt(a, b):
    return lax.dot_general(a, b, (((1,), (1,)), ((), ())), preferred_element_type=F32)


def _full(shape):
    n = len(shape)
    return pl.BlockSpec(shape, lambda *_: (0,) * n)


def _mod_kernel(c_ref, w_ref, b_ref, o_ref):
    s = _silu(c_ref[...]).astype(BF16)
    o_ref[0] = _dot(s, w_ref[0].astype(BF16)) + b_ref[0]


def _modulation(cvec, w_mod, b_mod):
    depth, d, d6 = w_mod.shape
    r = cvec.shape[0]
    tn = d6 // 4
    return pl.pallas_call(
        _mod_kernel,
        grid=(depth, d6 // tn),
        in_specs=[pl.BlockSpec((r, d), lambda l, j: (0, 0)),
                  pl.BlockSpec((1, d, tn), lambda l, j: (l, 0, j)),
                  pl.BlockSpec((1, 1, tn), lambda l, j: (l, 0, j))],
        out_specs=pl.BlockSpec((1, r, tn), lambda l, j: (l, 0, j)),
        out_shape=jax.ShapeDtypeStruct((depth, r, d6), F32),
        compiler_params=_params("parallel", "parallel"),
        name="modulation",
    )(cvec, w_mod, b_mod.reshape(depth, 1, d6))


def _mod_spec(nb, n_lat_tiles, d):
    return pl.BlockSpec((1, 6, d), lambda b, i: (jnp.where(i < n_lat_tiles, b, nb), 0, 0))


def _stream_tile(x_ref, ctx_ref, n_lat_tiles):
    return jnp.where(pl.program_id(1) < n_lat_tiles, x_ref[0], ctx_ref[0])


def _stream_specs(tm, d, n_lat_tiles):
    return [pl.BlockSpec((1, tm, d), lambda b, i: (b, jnp.minimum(i, n_lat_tiles - 1), 0)),
            pl.BlockSpec((1, tm, d), lambda b, i: (b, jnp.maximum(i - n_lat_tiles, 0), 0))]


def _proj_even_kernel(x_ref, ctx_ref, mod_ref, w_ref, c_ref, s_ref, q_ref, k_ref, v_ref, f_ref, *, n_lat_tiles):
    x = _stream_tile(x_ref, ctx_ref, n_lat_tiles)
    u = (_ln(x) * (1.0 + mod_ref[0, 1:2, :]) + mod_ref[0, 0:1, :]).astype(BF16)
    p = _dot(u, w_ref[...])
    c = c_ref[...]
    s = s_ref[...]
    blk = DA_HEAD_COLS
    for hd in range(DA_HEADS):
        pq = p[:, hd * blk: hd * blk + LANES]
        pk = p[:, hd * blk + LANES: hd * blk + 2 * LANES]
        q_ref[0, hd] = (pq * c + pltpu.roll(pq, LANES // 2, 1) * s).astype(BF16)
        k_ref[0, hd] = (pk * c + pltpu.roll(pk, LANES // 2, 1) * s).astype(BF16)
        v_ref[0, hd] = p[:, hd * blk + 2 * LANES: (hd + 1) * blk].astype(BF16)
    f_ref[0] = p[:, DA_HEADS * blk:]


def _proj_even(x, ctx, mod, w, rope_c, rope_s):
    nb, n_lat, d = x.shape
    ntot = n_lat + ctx.shape[1]
    tm = TOKEN_TILE
    nt = ntot // tm
    ncols = w.shape[1]
    qkv_shape = jax.ShapeDtypeStruct((nb, DA_HEADS, ntot, LANES), BF16)
    qkv_spec = pl.BlockSpec((1, DA_HEADS, tm, LANES), lambda b, i: (b, 0, i, 0))
    return pl.pallas_call(
        functools.partial(_proj_even_kernel, n_lat_tiles=n_lat // tm),
        grid=(nb, nt),
        in_specs=_stream_specs(tm, d, n_lat // tm) + [
                  _mod_spec(nb, n_lat // tm, d),
                  _full((d, ncols)),
                  pl.BlockSpec((tm, LANES), lambda b, i: (i, 0)),
                  pl.BlockSpec((tm, LANES), lambda b, i: (i, 0))],
        out_specs=[qkv_spec, qkv_spec, qkv_spec,
                   pl.BlockSpec((1, tm, FN_COLS), lambda b, i: (b, i, 0))],
        out_shape=[qkv_shape, qkv_shape, qkv_shape,
                   jax.ShapeDtypeStruct((nb, ntot, FN_COLS), F32)],
        compiler_params=_params("parallel", "parallel"),
        name="proj_even",
    )(x, ctx, mod, w, rope_c, rope_s)


def _stack_maps(q):
    qf = q.astype(F32)
    lane = lax.broadcasted_iota(jnp.int32, (1, LANES), 1)
    map1 = (lane % (LANES // 2)) < (LANES // 4)
    return jnp.concatenate([jnp.where(map1, qf, 0.0), jnp.where(map1, 0.0, qf)], axis=0).astype(BF16)


def _lane_block_max(s):
    blk = s[:, :LANES]
    for c in range(1, s.shape[1] // LANES):
        blk = jnp.maximum(blk, s[:, c * LANES:(c + 1) * LANES])
    return blk


def _exp_blocks(s, mb):
    return [jnp.exp(s[:, c * LANES:(c + 1) * LANES] - mb) for c in range(s.shape[1] // LANES)]


def _diff_merge(o, lp, head_g, lam_init):
    tq = o.shape[0] // 2
    lam = (jnp.exp(jnp.sum(lp[0:1] * lp[1:2], keepdims=True))
           - jnp.exp(jnp.sum(lp[2:3] * lp[3:4], keepdims=True)) + lam_init)
    a = o[:tq] - lam * o[tq:]
    a_n = a * lax.rsqrt(jnp.mean(a * a, axis=-1, keepdims=True) + LN_EPS) * (head_g * (1.0 - lam_init))
    return a_n.astype(BF16)


def _attn_kernel(lp_ref, g_ref, qc_ref, qn_ref, kc_ref, kn_ref, v_ref, o_ref,
                 sa_sc, sb_sc, mcur_sc, mnxt_sc, l_sc, acc_sc, *, tk, nkt, lam_init):
    step = (pl.program_id(0) * pl.num_programs(1) + pl.program_id(1)) * pl.num_programs(2) + pl.program_id(2)
    rows = mcur_sc.shape[0]

    def chunk_off(j):
        return j * tk if isinstance(j, int) else pl.multiple_of(j * tk, tk)

    def score_chunk(qq, k_ref, s_ref, j):
        off = chunk_off(j)
        s = _dot_nt(qq, k_ref[0, 0, pl.ds(off, tk), :])
        s_ref[j] = s
        mnxt_sc[...] = jnp.maximum(mnxt_sc[...], _lane_block_max(s))

    def row_max():
        return jnp.broadcast_to(jnp.max(mnxt_sc[...], axis=1, keepdims=True), (rows, LANES))

    @pl.when(step == 0)
    def _():
        qq0 = _stack_maps(qc_ref[0, 0])
        mnxt_sc[...] = jnp.full(mnxt_sc.shape, -jnp.inf, F32)

        def first(j, carry):
            score_chunk(qq0, kc_ref, sa_sc, j)
            return carry

        lax.fori_loop(0, nkt, first, 0)
        mcur_sc[...] = row_max()

    qqn = _stack_maps(qn_ref[0, 0])
    mnxt_sc[...] = jnp.full(mnxt_sc.shape, -jnp.inf, F32)
    l_sc[...] = jnp.zeros(l_sc.shape, F32)
    acc_sc[...] = jnp.zeros(acc_sc.shape, F32)

    def sweep(cur_ref, nxt_ref):
        def body(j, carry):
            score_chunk(qqn, kn_ref, nxt_ref, j)
            off = chunk_off(j)
            ps = _exp_blocks(cur_ref[j], mcur_sc[...])
            part = ps[0]
            for pc in ps[1:]:
                part = part + pc
            l_sc[...] += part
            acc_sc[...] += _dot(jnp.concatenate(ps, axis=1).astype(BF16), v_ref[0, 0, pl.ds(off, tk), :])
            return carry

        tail = min(nkt, SWEEP_UNROLL + nkt % SWEEP_UNROLL)
        lax.fori_loop(0, nkt - tail, body, 0, unroll=SWEEP_UNROLL)
        for j in range(nkt - tail, nkt):
            body(j, 0)

    @pl.when(step % 2 == 0)
    def _():
        sweep(sa_sc, sb_sc)

    @pl.when(step % 2 == 1)
    def _():
        sweep(sb_sc, sa_sc)

    o = acc_sc[...] / jnp.sum(l_sc[...], axis=1, keepdims=True)
    o_ref[0] = _diff_merge(o, lp_ref[...], g_ref[...], lam_init)
    mcur_sc[...] = row_max()


def _attn_ctx_kernel(lp_ref, g_ref, q_ref, k_ref, v_ref, buf_ref, o_ref, *, lam_init):
    del buf_ref
    s = _dot_nt(_stack_maps(q_ref[0, 0]), k_ref[0, 0])
    p = jnp.exp(s - jnp.max(s, axis=1, keepdims=True))
    o = _dot(p.astype(BF16), v_ref[0, 0]) / jnp.sum(p, axis=1, keepdims=True)
    o_ref[0] = _diff_merge(o, lp_ref[...], g_ref[...], lam_init)


def _attention(q, k, v, lam_params, head_g, n_lat, lam_init):
    nb, nh, ntot, _ = q.shape
    n_ctx = ntot - n_lat
    tq = min(ATTN_Q_TILE, n_lat)
    tk = TOKEN_TILE
    nq = n_lat // tq
    nkt = ntot // tk
    rows = 2 * tq
    kv_spec = pl.BlockSpec((1, 1, ntot, LANES), lambda b, h, i: (b, h, 0, 0))

    def following(b, h, i):
        s = jnp.minimum((b * nh + h) * nq + i + 1, nb * nh * nq - 1)
        return s // (nh * nq), (s // nq) % nh, s % nq

    def qn_map(b, h, i):
        b2, h2, i2 = following(b, h, i)
        return b2, h2, i2, 0

    def kn_map(b, h, i):
        b2, h2, _ = following(b, h, i)
        return b2, h2, 0, 0

    a_lat = pl.pallas_call(
        functools.partial(_attn_kernel, tk=tk, nkt=nkt, lam_init=lam_init),
        grid=(nb, nh, nq),
        in_specs=[_full(lam_params.shape), _full(head_g.shape),
                  pl.BlockSpec((1, 1, tq, LANES), lambda b, h, i: (b, h, i, 0)),
                  pl.BlockSpec((1, 1, tq, LANES), qn_map),
                  kv_spec, pl.BlockSpec((1, 1, ntot, LANES), kn_map), kv_spec],
        out_specs=pl.BlockSpec((1, tq, LANES), lambda b, h, i: (b, i, h)),
        out_shape=jax.ShapeDtypeStruct((nb, ntot, nh * LANES), BF16),
        scratch_shapes=[pltpu.VMEM((nkt, rows, tk), F32)] * 2 + [pltpu.VMEM((rows, LANES), F32)] * 4,
        compiler_params=_params("arbitrary", "arbitrary", "arbitrary"),
        name="diff_attention",
    )(lam_params, head_g, q, q, k, k, v)
    cblk = n_lat // n_ctx
    ctx_spec = pl.BlockSpec((1, 1, n_ctx, LANES), lambda b, h: (b, h, cblk, 0))
    return pl.pallas_call(
        functools.partial(_attn_ctx_kernel, lam_init=lam_init),
        grid=(nb, nh),
        in_specs=[_full(lam_params.shape), _full(head_g.shape), ctx_spec, ctx_spec, ctx_spec,
                  pl.BlockSpec(memory_space=pl.ANY)],
        out_specs=pl.BlockSpec((1, n_ctx, LANES), lambda b, h: (b, cblk, h)),
        out_shape=jax.ShapeDtypeStruct(a_lat.shape, a_lat.dtype),
        input_output_aliases={5: 0},
        compiler_params=_params("parallel", "parallel"),
        name="diff_attention_ctx",
    )(lam_params, head_g, q, k, v, a_lat)


def _fourier_factor(n):
    a = 1 << (int(math.log2(n)) // 2)
    assert a * (n // a) == n and a % SUBLANES == 0 and (n // a) % SUBLANES == 0
    return a, n // a


def _fourier_tables(n):
    a, bn = _fourier_factor(n)
    j = np.arange(FN_CH)
    ang = 2.0 * np.pi * np.outer(j, j) / FN_CH
    eye = np.eye(FN_GROUPS)
    wc = np.concatenate([np.kron(eye, np.cos(ang)), -np.kron(eye, np.sin(ang))], axis=1)
    ia = np.arange(a)
    ang_a = 2.0 * np.pi * np.outer(ia, ia) / a
    ca, sa = np.cos(ang_a), np.sin(ang_a)
    m1 = np.block([[ca, sa], [-sa, ca]])
    ib = np.arange(bn)
    ang_t = 2.0 * np.pi * np.outer(ib, ia) / n
    ang_b = 2.0 * np.pi * np.outer(ib, ib) / bn
    m3 = np.concatenate([np.cos(ang_b), np.sin(ang_b)], axis=1) / math.sqrt(n * FN_CH)
    tw_c = jnp.broadcast_to(jnp.asarray(np.cos(ang_t), F32)[:, :, None], (bn, a, LANES))
    tw_s = jnp.broadcast_to(jnp.asarray(np.sin(ang_t), F32)[:, :, None], (bn, a, LANES))
    return (jnp.asarray(wc, BF16), jnp.asarray(m1, BF16), tw_c, tw_s, jnp.asarray(m3, BF16))


FOURIER_ROW_TILE = 512


def _fourier_kernel(x_ref, wc_ref, m1_ref, twc_ref, tws_ref, m3_ref, *rest, a, bn):
    o_ref, g_sc, t_sc, y_sc = rest[-4:]
    n = a * bn
    nhalf = FN_COLS // LANES
    rt = min(FOURIER_ROW_TILE, n)

    def halves(v):
        return [v[:, h * LANES:(h + 1) * LANES] for h in range(v.shape[1] // LANES)]

    for t in range(n // rt):
        tile = slice(t * rt, (t + 1) * rt)
        g = _dot(x_ref[0, tile, :].astype(BF16), wc_ref[...])
        for i, part in enumerate(halves(g)):
            g_sc[i, tile, :] = part

    def gather(buf, rows):
        parts = [jnp.concatenate([buf[p * nhalf + h, rows, :] for h in range(nhalf)], axis=1) for p in range(2)]
        return jnp.concatenate(parts, axis=0).astype(BF16)

    def stage1(b, carry):
        t1 = _dot(m1_ref[...], gather(g_sc, pl.ds(b, a, stride=bn)))
        tr, ti = t1[:a], t1[a:]
        c = jnp.concatenate([twc_ref[b]] * nhalf, axis=1)
        s = jnp.concatenate([tws_ref[b]] * nhalf, axis=1)
        dst = pl.ds(pl.multiple_of(b * a, a), a)
        for i, part in enumerate(halves(tr * c + ti * s) + halves(ti * c - tr * s)):
            t_sc[i, dst, :] = part
        return carry

    lax.fori_loop(0, bn, stage1, 0, unroll=2)

    def stage2(ka, carry):
        rows = pl.ds(ka, bn, stride=a)
        y = _dot(m3_ref[...], gather(t_sc, rows))
        for i, part in enumerate(halves(y)):
            y_sc[i, rows, :] = part
        return carry

    lax.fori_loop(0, a, stage2, 0, unroll=2)
    for t in range(n // rt):
        tile = slice(t * rt, (t + 1) * rt)
        o_ref[0, tile, :] = jnp.concatenate([y_sc[h, tile, :] for h in range(nhalf)], axis=1)


def _fourier_mix(fcat, row0, n, out_buf):
    nb, ntot, _ = fcat.shape
    a, bn = _fourier_factor(n)
    tables = _fourier_tables(n)
    assert row0 % n == 0
    seq = pl.BlockSpec((1, n, FN_COLS), lambda b: (b, row0 // n, 0))
    in_specs = [seq] + [_full(t.shape) for t in tables]
    args = [fcat, *tables]
    aliases = {}
    if out_buf is not None:
        in_specs.append(pl.BlockSpec(memory_space=pl.ANY))
        args.append(out_buf)
        aliases = {len(args) - 1: 0}
    return pl.pallas_call(
        functools.partial(_fourier_kernel, a=a, bn=bn),
        grid=(nb,),
        in_specs=in_specs,
        out_specs=seq,
        out_shape=jax.ShapeDtypeStruct((nb, ntot, FN_COLS), F32),
        scratch_shapes=[pltpu.VMEM((2 * (FN_COLS // LANES), n, LANES), F32)] * 2
                       + [pltpu.VMEM((FN_COLS // LANES, n, LANES), F32)],
        input_output_aliases=aliases,
        compiler_params=_params("parallel"),
        name="fourier_mix",
    )(*args)


def _post_norm(h, y, g, b):
    return _ln(ALPHA * h + y) * g + b


def _out_even_kernel(a_ref, f_ref, x_ref, ctx_ref, mod_ref, w_ref, g_ref, b_ref, o_ref, *, n_lat_tiles):
    na = a_ref.shape[-1]
    y = _dot(a_ref[0], w_ref[:na, :]) + _dot(f_ref[0].astype(BF16), w_ref[na:, :])
    h = _stream_tile(x_ref, ctx_ref, n_lat_tiles)
    o_ref[0] = _post_norm(h, mod_ref[0, 2:3, :] * y, g_ref[...], b_ref[...])


def _out_even(a_n, fm, x, ctx, mod, w_out, ln_g, ln_b):
    nb, n_lat, d = x.shape
    ntot = n_lat + ctx.shape[1]
    tm = TOKEN_TILE
    na = a_n.shape[-1]
    tok = lambda c: pl.BlockSpec((1, tm, c), lambda b, i: (b, i, 0))
    return pl.pallas_call(
        functools.partial(_out_even_kernel, n_lat_tiles=n_lat // tm),
        grid=(nb, ntot // tm),
        in_specs=[tok(na), tok(FN_COLS)] + _stream_specs(tm, d, n_lat // tm) + [
                  _mod_spec(nb, n_lat // tm, d), _full(w_out.shape), _full((1, d)), _full((1, d))],
        out_specs=tok(d),
        out_shape=jax.ShapeDtypeStruct((nb, ntot, d), F32),
        compiler_params=_params("parallel", "parallel"),
        name="out_even",
    )(a_n, fm, x, ctx, mod, w_out, ln_g, ln_b)


def _ffn_kernel(h_ref, mod_ref, w1_ref, w3_ref, w2_ref, g_ref, b_ref, o_ref):
    h = h_ref[0]
    u = (_ln(h) * (1.0 + mod_ref[0, 4:5, :]) + mod_ref[0, 3:4, :]).astype(BF16)
    hid = (_silu(_dot(u, w1_ref[...])) * _dot(u, w3_ref[...])).astype(BF16)
    y = _dot(hid, w2_ref[...])
    o_ref[0] = _post_norm(h, mod_ref[0, 5:6, :] * y, g_ref[...], b_ref[...])


def _ffn(h, n_rows, mod, w1, w3, w2, ln_g, ln_b, n_lat):
    nb, _, d = h.shape
    tm = TOKEN_TILE
    tok = pl.BlockSpec((1, tm, d), lambda b, i: (b, i, 0))
    resident = lambda s: pl.BlockSpec(s, lambda b, i: (0, 0), pipeline_mode=pl.Buffered(1))
    return pl.pallas_call(
        _ffn_kernel,
        grid=(nb, n_rows // tm),
        in_specs=[tok, _mod_spec(nb, n_lat // tm, d),
                  resident(w1.shape), resident(w3.shape), resident(w2.shape),
                  _full((1, d)), _full((1, d))],
        out_specs=tok,
        out_shape=jax.ShapeDtypeStruct((nb, n_rows, d), F32),
        compiler_params=_params("parallel", "parallel"),
        name="ffn",
    )(h, mod, w1, w3, w2, ln_g, ln_b)


def _proj_odd_kernel(h_ref, mod_ref, w_ref, xm_ref, z_ref):
    x = h_ref[0]
    u = (_ln(x) * (1.0 + mod_ref[0, 1:2, :]) + mod_ref[0, 0:1, :]).astype(BF16)
    p = _dot(u, w_ref[...])
    inner = xm_ref.shape[-1]
    xm_ref[0] = p[:, :inner]
    z_ref[0] = p[:, inner:]


def _proj_odd(hcat, mod, w, n_lat):
    nb, ntot, d = hcat.shape
    tm = TOKEN_TILE
    inner = w.shape[1] // 2
    out_spec = pl.BlockSpec((1, tm, inner), lambda b, i: (b, i, 0))
    out_shape = jax.ShapeDtypeStruct((nb, ntot, inner), F32)
    return pl.pallas_call(
        _proj_odd_kernel,
        grid=(nb, ntot // tm),
        in_specs=[pl.BlockSpec((1, tm, d), lambda b, i: (b, i, 0)),
                  _mod_spec(nb, n_lat // tm, d),
                  _full(w.shape)],
        out_specs=[out_spec, out_spec],
        out_shape=[out_shape, out_shape],
        compiler_params=_params("parallel", "parallel"),
        name="proj_odd",
    )(hcat, mod, w)


def _gate_fold_kernel(wg_ref, wq_ref, wk_ref, wv_ref, gc_ref, gm_ref):
    dh = wq_ref.shape[-1]
    wg = wg_ref[0]
    gc_ref[0] = (_dot_nt(wg[:, :dh], wq_ref[0]) + _dot_nt(wg[:, dh:2 * dh], wk_ref[0])).astype(BF16)
    gm_ref[0] = _dot_nt(wg[:, 2 * dh:], wv_ref[0]).astype(BF16)


def _gate_fold(wg, wq, wk, wv):
    nh, n_gates, _ = wg.shape
    dh = wq.shape[-1]
    wblk = pl.BlockSpec((1, dh, dh), lambda h: (h, 0, 0))
    gspec = pl.BlockSpec((1, n_gates, dh), lambda h: (h, 0, 0))
    gshape = jax.ShapeDtypeStruct((nh, n_gates, dh), BF16)
    return pl.pallas_call(
        _gate_fold_kernel,
        grid=(nh,),
        in_specs=[pl.BlockSpec((1, n_gates, 3 * dh), lambda h: (h, 0, 0)), wblk, wblk, wblk],
        out_specs=[gspec, gspec],
        out_shape=[gshape, gshape],
        compiler_params=_params("parallel"),
        name="mlstm_gate_fold",
    )(wg, wq, wk, wv)


def _mfeat_kernel(xm_ref, cw_ref, cb_ref, wk_ref, wqt_ref, wvt_ref,
                  xc_ref, k_ref, qt_ref, vt_ref, gr_ref, *, n_lat, chunk, row_group):
    ntot = xm_ref.shape[1]
    dh = wk_ref.shape[-1]
    ct = CONV_TILE
    half = ML_CONV_W // 2
    cw = cw_ref[...]
    cb = cb_ref[...]

    def conv_tile(t, carry):
        r0 = pl.multiple_of(t * ct, ct)
        at_start = jnp.logical_or(r0 == 0, r0 == n_lat)
        at_end = jnp.logical_or(r0 + ct == n_lat, r0 + ct == ntot)
        p0 = pl.multiple_of(jnp.maximum(r0 - SUBLANES, 0), SUBLANES)
        n0 = pl.multiple_of(jnp.minimum(r0 + ct, ntot - SUBLANES), SUBLANES)
        prev = xm_ref[0, pl.ds(p0, SUBLANES), :]
        nxt = xm_ref[0, pl.ds(n0, SUBLANES), :]
        prev = jnp.where(at_start, 0.0, prev)
        nxt = jnp.where(at_end, 0.0, nxt)
        xe = jnp.concatenate([prev, xm_ref[0, pl.ds(r0, ct), :], nxt], axis=0)
        acc = cb
        for dd in range(ML_CONV_W):
            lo = SUBLANES - half + dd
            acc = acc + cw[dd:dd + 1, :] * xe[lo:lo + ct, :]
        xc_ref[0, pl.ds(r0, ct), :] = _silu(acc)
        return carry

    lax.fori_loop(0, ntot // ct, conv_tile, 0)

    for g in range(ntot // row_group):
        rows = pl.ds(g * row_group, row_group)
        k_ref[0, rows, :] = _dot(xc_ref[0, rows, :].astype(BF16), wk_ref[0]).astype(BF16)

    def t_chunk(c, carry):
        r0 = pl.multiple_of(c * chunk, chunk)
        tq = _dot_nt(wqt_ref[0], xc_ref[0, pl.ds(r0, chunk), :].astype(BF16))
        tv = _dot_nt(wvt_ref[0], xm_ref[0, pl.ds(r0, chunk), :].astype(BF16))
        qt_ref[0, 0, c] = tq[:dh].astype(BF16)
        vt_ref[0, 0, c] = tv[:dh].astype(BF16)
        gr_ref[0, 0, c] = tq[dh:] + tv[dh:]
        return carry

    lax.fori_loop(0, ntot // chunk, t_chunk, 0, unroll=2)


def _mfeat(xm, conv_w, conv_b, wk, wqt_g, wvt_g, n_lat, chunk):
    nb, ntot, inner = xm.shape
    nh = ML_HEADS
    dh = inner // nh
    nct = ntot // chunk
    n_gates = wqt_g.shape[1] - dh
    ng = 2 if ntot % (2 * SUBLANES) == 0 else 1
    kern = functools.partial(_mfeat_kernel, n_lat=n_lat, chunk=chunk, row_group=ntot // ng)
    seq = pl.BlockSpec((1, ntot, dh), lambda b, h: (b, 0, h))
    wstack = pl.BlockSpec((1, dh + n_gates, dh), lambda b, h: (h, 0, 0))
    tspec = pl.BlockSpec((1, 1, nct, dh, chunk), lambda b, h: (b, h, 0, 0, 0))
    tshape = jax.ShapeDtypeStruct((nb, nh, nct, dh, chunk), BF16)
    return pl.pallas_call(
        kern,
        grid=(nb, nh),
        in_specs=[seq,
                  pl.BlockSpec((ML_CONV_W, dh), lambda b, h: (0, h)),
                  pl.BlockSpec((1, dh), lambda b, h: (0, h)),
                  pl.BlockSpec((1, dh, dh), lambda b, h: (h, 0, 0)), wstack, wstack],
        out_specs=[seq, seq, tspec, tspec,
                   pl.BlockSpec((1, 1, nct, n_gates, chunk), lambda b, h: (b, h, 0, 0, 0))],
        out_shape=[jax.ShapeDtypeStruct((nb, ntot, inner), F32),
                   jax.ShapeDtypeStruct((nb, ntot, inner), BF16), tshape, tshape,
                   jax.ShapeDtypeStruct((nb, nh, nct, n_gates, chunk), F32)],
        compiler_params=_params("parallel", "parallel"),
        name="mlstm_features",
    )(xm, conv_w, conv_b, wk, wqt_g, wvt_g)


def _split3(x):
    hi = x.astype(BF16)
    r1 = x - hi.astype(F32)
    mid = r1.astype(BF16)
    lo = (r1 - mid.astype(F32)).astype(BF16)
    return hi, mid, lo


def _running_max(x, reverse):
    n = x.shape[1]
    lane = lax.broadcasted_iota(jnp.int32, x.shape, 1)
    k = 1
    while k < n:
        if reverse:
            shifted = jnp.where(lane < n - k, pltpu.roll(x, n - k, 1), -jnp.inf)
        else:
            shifted = jnp.where(lane >= k, pltpu.roll(x, k, 1), -jnp.inf)
        x = jnp.maximum(x, shifted)
        k *= 2
    return x


def _gates_kernel(g_ref, b_ref, rows_ref, cols_ref):
    nh = ML_HEADS
    nct, chunk = g_ref.shape[2], g_ref.shape[-1]
    bias = b_ref[...]
    li_f, li_b, lf_f, lf_b = [], [], [], []
    for c in range(nct):
        pre = jnp.sum(g_ref[0, :, c], axis=0) + bias
        xf = pre[2 * nh:]
        log_f = jnp.minimum(xf, 0.0) - jnp.log1p(jnp.exp(-jnp.abs(xf)))
        li_f.append(pre[:nh])
        li_b.append(pre[nh:2 * nh])
        lf_f.append(log_f[:nh])
        lf_b.append(log_f[nh:])
    s_idx = lax.broadcasted_iota(jnp.int32, (chunk, chunk), 0)
    t_idx = lax.broadcasted_iota(jnp.int32, (chunk, chunk), 1)
    tri_f = jnp.where(s_idx <= t_idx, 1.0, 0.0).astype(BF16)
    tri_b = jnp.where(s_idx >= t_idx, 1.0, 0.0).astype(BF16)
    cum_f = sum(_dot(p, tri_f) for p in _split3(jnp.concatenate(lf_f, axis=0)))
    cum_b = sum(_dot(p, tri_b) for p in _split3(jnp.concatenate(lf_b, axis=0)))
    gap_f = jnp.concatenate(li_f, axis=0) - cum_f
    gap_b = jnp.concatenate(li_b, axis=0) - cum_b
    top_f = _running_max(gap_f, False)
    top_b = _running_max(gap_b, True)
    pad = jnp.zeros((LANES - 2 * nh, chunk), F32)
    for c in range(nct):
        r = slice(c * nh, (c + 1) * nh)
        rows_ref[0, c] = jnp.concatenate([gap_f[r], gap_b[r], top_f[r], top_b[r], cum_f[r], cum_b[r]], axis=0)
        gap_t = jnp.transpose(jnp.concatenate([gap_f[r], gap_b[r], pad], axis=0))
        cols_ref[0, c * chunk:(c + 1) * chunk, :] = gap_t[:, :2 * nh]


def _gates(g_part, bias, chunk):
    nb, nh, nct, n_gates, _ = g_part.shape
    ntot = nct * chunk
    return pl.pallas_call(
        _gates_kernel,
        grid=(nb,),
        in_specs=[pl.BlockSpec((1, nh, nct, n_gates, chunk), lambda b: (b, 0, 0, 0, 0)),
                  _full(bias.shape)],
        out_specs=[pl.BlockSpec((1, nct, 6 * nh, chunk), lambda b: (b, 0, 0, 0)),
                   pl.BlockSpec((1, ntot, 2 * nh), lambda b: (b, 0, 0))],
        out_shape=[jax.ShapeDtypeStruct((nb, nct, 6 * nh, chunk), F32),
                   jax.ShapeDtypeStruct((nb, ntot, 2 * nh), F32)],
        compiler_params=_params("parallel"),
        name="mlstm_gates",
    )(g_part, bias)


STATE_PAD_ROWS = 16


def _scan_kernel(k_ref, qt_ref, vt_ref, rows_ref, cols_ref, hs_ref, c_sc, n_sc, m_sc,
                 *, n_lat_chunks, n_ctx_chunks, chunk):
    nh = ML_HEADS
    h = pl.program_id(1)
    dh = c_sc.shape[-1]
    c_sc[...] = jnp.zeros(c_sc.shape, F32)
    n_sc[...] = jnp.zeros(n_sc.shape, F32)
    m_sc[...] = jnp.zeros(m_sc.shape, F32)
    s_idx = lax.broadcasted_iota(jnp.int32, (chunk, chunk), 0)
    t_idx = lax.broadcasted_iota(jnp.int32, (chunk, chunk), 1)
    gate_lane = lax.broadcasted_iota(jnp.int32, (1, 2 * nh), 1)

    def step(d, cidx, with_out):
        off = pl.multiple_of(cidx * chunk, chunk)
        kc = k_ref[0, pl.ds(off, chunk), :]
        qt = qt_ref[0, 0, cidx]
        vt = vt_ref[0, 0, cidx]
        gap_r = rows_ref[0, cidx, pl.ds(d * nh + h, 1), :]
        top_r = rows_ref[0, cidx, pl.ds((2 + d) * nh + h, 1), :]
        cum_r = rows_ref[0, cidx, pl.ds((4 + d) * nh + h, 1), :]
        m = m_sc[d]
        ct = c_sc[d]
        nrow = n_sc[d]
        last = chunk - 1 if d == 0 else 0
        b_end = cum_r[:, last:last + 1]
        top_end = top_r[:, last:last + 1]
        if with_out is not None:
            ctile = cols_ref[0, pl.ds(off, chunk), :]
            gap_c = jnp.sum(jnp.where(gate_lane == d * nh + h, ctile, 0.0), axis=1, keepdims=True)
            lift = jnp.maximum(top_r, m)
            ordered = (s_idx <= t_idx) if d == 0 else (s_idx >= t_idx)
            dw_t = jnp.exp(jnp.where(ordered, gap_c - lift, -jnp.inf))
            iw = jnp.exp(m - lift)
            nb16 = jnp.broadcast_to(nrow.astype(BF16), (STATE_PAD_ROWS, dh))
            a = _dot(jnp.concatenate([kc, ct.astype(BF16), nb16], axis=0), qt)
            s_t = a[:chunk] * dw_t
            num_t = iw * a[chunk:chunk + dh] + _dot(vt, s_t.astype(BF16))
            den = iw * a[chunk + dh:chunk + dh + 1] + jnp.sum(s_t, axis=0, keepdims=True)
            floor = jnp.exp(-(cum_r + lift))
            h_t = num_t * (1.0 / jnp.maximum(jnp.abs(den), floor))
            hs_ref[0, 0, cidx] = h_t if with_out == "assign" else hs_ref[0, 0, cidx] + h_t
        mx = jnp.maximum(m, top_end)
        keep = jnp.exp(m - mx)
        w_r = jnp.exp(gap_r - mx)
        vw = (vt.astype(F32) * w_r).astype(BF16)
        wb16 = jnp.broadcast_to(w_r.astype(BF16), (STATE_PAD_ROWS, chunk))
        upd = _dot(jnp.concatenate([vw, wb16], axis=0), kc)
        c_sc[d] = keep * ct + upd[:dh]
        n_sc[d] = keep * nrow + upd[dh:dh + 1]
        m_sc[d] = b_end + mx

    def ctx_body(c, carry):
        step(0, n_lat_chunks + c, None)
        step(1, n_lat_chunks + n_ctx_chunks - 1 - c, None)
        return carry

    def lat_body(mode, c, carry):
        step(0, c, mode)
        step(1, n_lat_chunks - 1 - c, mode)
        return carry

    lax.fori_loop(0, n_ctx_chunks, ctx_body, 0)
    half = n_lat_chunks // 2
    lax.fori_loop(0, half, functools.partial(lat_body, "assign"), 0, unroll=2)
    lax.fori_loop(half, n_lat_chunks, functools.partial(lat_body, "add"), 0, unroll=2)


def _scan(k, qt, vt, rows, cols, n_lat, chunk):
    nb, ntot, inner = k.shape
    nh = ML_HEADS
    dh = inner // nh
    nct = ntot // chunk
    ncl = n_lat // chunk
    assert ncl % 2 == 0
    kern = functools.partial(_scan_kernel, n_lat_chunks=ncl, n_ctx_chunks=nct - ncl, chunk=chunk)
    tspec = pl.BlockSpec((1, 1, nct, dh, chunk), lambda b, h: (b, h, 0, 0, 0))
    return pl.pallas_call(
        kern,
        grid=(nb, nh),
        in_specs=[pl.BlockSpec((1, ntot, dh), lambda b, h: (b, 0, h)), tspec, tspec,
                  pl.BlockSpec((1, nct, rows.shape[2], chunk), lambda b, h: (b, 0, 0, 0)),
                  pl.BlockSpec((1, ntot, cols.shape[2]), lambda b, h: (b, 0, 0))],
        out_specs=pl.BlockSpec((1, 1, ncl, dh, chunk), lambda b, h: (b, h, 0, 0, 0)),
        out_shape=jax.ShapeDtypeStruct((nb, nh, ncl, dh, chunk), F32),
        scratch_shapes=[pltpu.VMEM((2, dh, dh), F32), pltpu.VMEM((2, 1, dh), F32),
                        pltpu.VMEM((2, 1, 1), F32)],
        compiler_params=_params("parallel", "parallel"),
        name="mlstm_scan",
    )(k, qt, vt, rows, cols)


def _out_odd_kernel(hs_ref, xc_ref, z_ref, h_ref, mod_ref, skip_ref, hg_ref, w_ref, g_ref, b_ref, o_ref):
    parts = []
    for hd in range(ML_HEADS):
        ht = hs_ref[0, hd, 0]
        mu = jnp.mean(ht, axis=0, keepdims=True)
        hc = ht - mu
        var = jnp.mean(hc * hc, axis=0, keepdims=True)
        parts.append(jnp.transpose(hc * lax.rsqrt(var + LN_EPS)))
    hn = jnp.concatenate(parts, axis=1)
    y = (hn * hg_ref[...] + skip_ref[...] * xc_ref[0]) * _silu(z_ref[0])
    yo = _dot(y.astype(BF16), w_ref[...])
    o_ref[0] = _post_norm(h_ref[0], mod_ref[0, 2:3, :] * yo, g_ref[...], b_ref[...])


def _out_odd(hs, xc, z, hcat, mod, skip, head_g, w_out, ln_g, ln_b, n_lat):
    nb, _, d = hcat.shape
    inner = xc.shape[-1]
    _, nh, _, dh, tm = hs.shape
    tok = lambda c: pl.BlockSpec((1, tm, c), lambda b, i: (b, i, 0))
    return pl.pallas_call(
        _out_odd_kernel,
        grid=(nb, n_lat // tm),
        in_specs=[pl.BlockSpec((1, nh, 1, dh, tm), lambda b, i: (b, 0, i, 0, 0)),
                  tok(inner), tok(inner), tok(d),
                  pl.BlockSpec((1, 6, d), lambda b, i: (b, 0, 0)),
                  _full((1, inner)), _full((1, inner)), _full(w_out.shape),
                  _full((1, d)), _full((1, d))],
        out_specs=tok(d),
        out_shape=jax.ShapeDtypeStruct((nb, n_lat, d), F32),
        compiler_params=_params("parallel", "parallel"),
        name="out_odd",
    )(hs, xc, z, hcat, mod, skip, head_g, w_out, ln_g, ln_b)


def _even_in_columns():
    cols, scale = [], []
    quarter = DA_DK // 2
    for hd in range(DA_HEADS):
        base = hd * DA_HEAD_COLS
        for blk in range(2):
            b0 = base + blk * 2 * DA_DK
            for half in range(2):
                for m in range(2):
                    cols += [b0 + m * DA_DK + half * quarter + j for j in range(quarter)]
            scale += [DA_DK ** -0.5 if blk == 0 else 1.0] * (2 * DA_DK)
        cols += list(range(base + 4 * DA_DK, base + DA_HEAD_COLS))
        scale += [1.0] * DA_DV
    cols += list(range(DA_HEADS * DA_HEAD_COLS, DA_HEADS * DA_HEAD_COLS + FN_COLS))
    scale += [1.0] * FN_COLS
    return np.asarray(cols, np.int32), np.asarray(scale, np.float32)


def _rope_tables(n_lat, ntot):
    rows = n_lat // GRID_W
    row = jnp.repeat(jnp.arange(rows, dtype=F32), GRID_W)
    col = jnp.tile(jnp.arange(GRID_W, dtype=F32), rows)
    n_freq = DA_DK // 4
    inv_freq = ROPE_BASE ** (-jnp.arange(n_freq, dtype=F32) / n_freq)
    ang = jnp.concatenate([row[:, None] * inv_freq, col[:, None] * inv_freq], -1)
    cos, sin = jnp.cos(ang), jnp.sin(ang)
    c = jnp.concatenate([cos, cos, cos, cos], axis=1)
    s = jnp.concatenate([-sin, -sin, sin, sin], axis=1)
    n_ctx = ntot - n_lat
    c = jnp.concatenate([c, jnp.ones((n_ctx, LANES), F32)], axis=0)
    s = jnp.concatenate([s, jnp.zeros((n_ctx, LANES), F32)], axis=0)
    return c, s


def _block_diag_heads(w, dh):
    nblk = w.shape[0]
    per_head = dh // ML_BLOCK
    wh = w.reshape(nblk // per_head, per_head, ML_BLOCK, ML_BLOCK)
    eye = jnp.eye(per_head, dtype=w.dtype)
    dense = jnp.einsum('hgij,gk->hgikj', wh, eye)
    return dense.reshape(nblk // per_head, dh, dh)


def kernel(x, c, ctx, c_ctx, w_mod, b_mod, ln_g, ln_b, w_ff1, w_ff3, w_ff2, a_w_in, a_w_out, da_lq1, da_lk1, da_lq2, da_lk2, da_head_g, m_w_in, m_w_out, m_conv_w, m_conv_b, m_wq, m_wk, m_wv, m_w_ig, m_b_ig, m_w_fg, m_b_fg, m_skip, m_head_g):
    nb, n_lat, d = x.shape
    n_ctx = ctx.shape[1]
    ntot = n_lat + n_ctx
    assert w_mod.shape[0] == DEPTH == 2
    assert n_lat % TOKEN_TILE == 0 and n_ctx % TOKEN_TILE == 0 and n_lat % n_ctx == 0 and n_lat % GRID_W == 0

    r = -(-(nb + 1) // MOD_ROWS_PAD) * MOD_ROWS_PAD
    cvec = jnp.concatenate([c, c_ctx[None, :], jnp.zeros((r - nb - 1, d), F32)], axis=0)
    mod_all = _modulation(cvec, w_mod, b_mod).reshape(DEPTH, r, 6, d)

    lam_init0 = 0.8 - 0.6 * math.exp(-0.3 * 0)
    cols, colscale = _even_in_columns()
    w_in = (a_w_in[0][:, cols] * colscale[None, :]).astype(BF16)
    rope_c, rope_s = _rope_tables(n_lat, ntot)
    q, k, v, f = _proj_even(x, ctx, mod_all[0], w_in, rope_c, rope_s)
    lam_params = jnp.stack([da_lq1[0], da_lk1[0], da_lq2[0], da_lk2[0]], axis=0)
    a_n = _attention(q, k, v, lam_params, da_head_g[0][None, :], n_lat, lam_init0)
    fm = _fourier_mix(f, 0, n_lat, None)
    fm = _fourier_mix(f, n_lat, n_ctx, fm)
    h1 = _out_even(a_n, fm, x, ctx, mod_all[0], a_w_out[0].astype(BF16),
                   ln_g[0, 0][None, :], ln_b[0, 0][None, :])
    hcat = _ffn(h1, ntot, mod_all[0], w_ff1[0].astype(BF16), w_ff3[0].astype(BF16), w_ff2[0].astype(BF16),
                ln_g[0, 1][None, :], ln_b[0, 1][None, :], n_lat)

    inner = m_w_in.shape[2] // 2
    dh = inner // ML_HEADS
    chunk = MXU_DIM if (n_lat % MXU_DIM == 0 and n_ctx % MXU_DIM == 0) else LANES
    xm, z = _proj_odd(hcat, mod_all[1], m_w_in[0].astype(BF16), n_lat)
    wq_f = _block_diag_heads(m_wq[0], dh)
    wk_f = _block_diag_heads(m_wk[0], dh) * (dh ** -0.5)
    wv_f = _block_diag_heads(m_wv[0], dh)
    wq, wk, wv = wq_f.astype(BF16), wk_f.astype(BF16), wv_f.astype(BF16)
    wg_all = jnp.concatenate([m_w_ig[0, 0], m_w_ig[0, 1], m_w_fg[0, 0], m_w_fg[0, 1]], axis=1)
    wg_all = wg_all.reshape(3, ML_HEADS, dh, 4 * ML_HEADS) * jnp.asarray([1.0, dh ** 0.5, 1.0], F32)[:, None, None, None]
    wg = jnp.transpose(wg_all, (1, 3, 0, 2)).reshape(ML_HEADS, 4 * ML_HEADS, 3 * dh).astype(BF16)
    g_bias = jnp.concatenate([m_b_ig[0, 0], m_b_ig[0, 1], m_b_fg[0, 0], m_b_fg[0, 1]])[:, None]
    gc, gm = _gate_fold(wg, wq, wk, wv)
    wqt_g = jnp.concatenate([jnp.swapaxes(wq_f, 1, 2).astype(BF16), gc], axis=1)
    wvt_g = jnp.concatenate([jnp.swapaxes(wv_f, 1, 2).astype(BF16), gm], axis=1)
    xc, km, qt, vt, g_part = _mfeat(xm, m_conv_w[0], m_conv_b[0][None, :], wk, wqt_g, wvt_g, n_lat, chunk)
    rows, colsg = _gates(g_part, g_bias, chunk)
    hs = _scan(km, qt, vt, rows, colsg, n_lat, chunk)
    h1 = _out_odd(hs, xc, z, hcat, mod_all[1], m_skip[0][None, :], m_head_g[0][None, :],
                  m_w_out[0].astype(BF16), ln_g[1, 0][None, :], ln_b[1, 0][None, :], n_lat)
    return _ffn(h1, n_lat, mod_all[1], w_ff1[1].astype(BF16), w_ff3[1].astype(BF16), w_ff2[1].astype(BF16),
                ln_g[1, 1][None, :], ln_b[1, 1][None, :], n_lat)
```

```python
import functools
import math

import numpy as np
import jax
import jax.numpy as jnp
from jax import lax
from jax.experimental import pallas as pl
from jax.experimental.pallas import tpu as pltpu

F32 = jnp.float32
BF16 = jnp.bfloat16

DA_HEADS = 6
DA_DK = 64
DA_DV = 2 * DA_DK
DA_HEAD_COLS = 4 * DA_DK + DA_DV
FN_GROUPS = 4
FN_CH = 64
FN_COLS = FN_GROUPS * FN_CH
ML_HEADS = 8
ML_BLOCK = 4
ML_CONV_W = 5
GRID_W = 64
ROPE_BASE = 10000.0
LN_EPS = 1e-5
DEPTH = 2
ALPHA = (2 * DEPTH) ** 0.25

LANES = 128
SUBLANES = 8
MXU_DIM = 256
VMEM_LIMIT_BYTES = 60 * 1024 * 1024

TOKEN_TILE = 256
CONV_TILE = 128
MOD_ROWS_PAD = 8
ATTN_Q_TILE = 512
SWEEP_UNROLL = 4


def _params(*sem):
    return pltpu.CompilerParams(dimension_semantics=sem, vmem_limit_bytes=VMEM_LIMIT_BYTES)


def _ln(x):
    mu = jnp.mean(x, axis=-1, keepdims=True)
    xc = x - mu
    var = jnp.mean(xc * xc, axis=-1, keepdims=True)
    return xc * lax.rsqrt(var + LN_EPS)


def _silu(x):
    return x * jax.nn.sigmoid(x)


def _dot(a, b):
    return jnp.dot(a, b, preferred_element_type=F32)


def _dot_nt(a, b):
    return lax.dot_general(a, b, (((1,), (1,)), ((), ())), preferred_element_type=F32)


def _full(shape):
    n = len(shape)
    return pl.BlockSpec(shape, lambda *_: (0,) * n)


def _mod_kernel(c_ref, w_ref, b_ref, o_ref):
    s = _silu(c_ref[...]).astype(BF16)
    o_ref[0] = _dot(s, w_ref[0].astype(BF16)) + b_ref[0]


def _modulation(cvec, w_mod, b_mod):
    depth, d, d6 = w_mod.shape
    r = cvec.shape[0]
    tn = d6 // 4
    return pl.pallas_call(
        _mod_kernel,
        grid=(depth, d6 // tn),
        in_specs=[pl.BlockSpec((r, d), lambda l, j: (0, 0)),
                  pl.BlockSpec((1, d, tn), lambda l, j: (l, 0, j)),
                  pl.BlockSpec((1, 1, tn), lambda l, j: (l, 0, j))],
        out_specs=pl.BlockSpec((1, r, tn), lambda l, j: (l, 0, j)),
        out_shape=jax.ShapeDtypeStruct((depth, r, d6), F32),
        compiler_params=_params("parallel", "parallel"),
        name="modulation",
    )(cvec, w_mod, b_mod.reshape(depth, 1, d6))


def _mod_spec(nb, n_lat_tiles, d):
    return pl.BlockSpec((1, 6, d), lambda b, i: (jnp.where(i < n_lat_tiles, b, nb), 0, 0))


def _stream_tile(x_ref, ctx_ref, n_lat_tiles):
    return jnp.where(pl.program_id(1) < n_lat_tiles, x_ref[0], ctx_ref[0])


def _stream_specs(tm, d, n_lat_tiles):
    return [pl.BlockSpec((1, tm, d), lambda b, i: (b, jnp.minimum(i, n_lat_tiles - 1), 0)),
            pl.BlockSpec((1, tm, d), lambda b, i: (b, jnp.maximum(i - n_lat_tiles, 0), 0))]


def _proj_even_kernel(x_ref, ctx_ref, mod_ref, w_ref, c_ref, s_ref, q_ref, k_ref, v_ref, f_ref, *, n_lat_tiles):
    x = _stream_tile(x_ref, ctx_ref, n_lat_tiles)
    u = (_ln(x) * (1.0 + mod_ref[0, 1:2, :]) + mod_ref[0, 0:1, :]).astype(BF16)
    p = _dot(u, w_ref[...])
    c = c_ref[...]
    s = s_ref[...]
    blk = DA_HEAD_COLS
    for hd in range(DA_HEADS):
        pq = p[:, hd * blk: hd * blk + LANES]
        pk = p[:, hd * blk + LANES: hd * blk + 2 * LANES]
        q_ref[0, hd] = (pq * c + pltpu.roll(pq, LANES // 2, 1) * s).astype(BF16)
        k_ref[0, hd] = (pk * c + pltpu.roll(pk, LANES // 2, 1) * s).astype(BF16)
        v_ref[0, hd] = p[:, hd * blk + 2 * LANES: (hd + 1) * blk].astype(BF16)
    f_ref[0] = p[:, DA_HEADS * blk:]


def _proj_even(x, ctx, mod, w, rope_c, rope_s):
    nb, n_lat, d = x.shape
    ntot = n_lat + ctx.shape[1]
    tm = TOKEN_TILE
    nt = ntot // tm
    ncols = w.shape[1]
    qkv_shape = jax.ShapeDtypeStruct((nb, DA_HEADS, ntot, LANES), BF16)
    qkv_spec = pl.BlockSpec((1, DA_HEADS, tm, LANES), lambda b, i: (b, 0, i, 0))
    return pl.pallas_call(
        functools.partial(_proj_even_kernel, n_lat_tiles=n_lat // tm),
        grid=(nb, nt),
        in_specs=_stream_specs(tm, d, n_lat // tm) + [
                  _mod_spec(nb, n_lat // tm, d),
                  _full((d, ncols)),
                  pl.BlockSpec((tm, LANES), lambda b, i: (i, 0)),
                  pl.BlockSpec((tm, LANES), lambda b, i: (i, 0))],
        out_specs=[qkv_spec, qkv_spec, qkv_spec,
                   pl.BlockSpec((1, tm, FN_COLS), lambda b, i: (b, i, 0))],
        out_shape=[qkv_shape, qkv_shape, qkv_shape,
                   jax.ShapeDtypeStruct((nb, ntot, FN_COLS), F32)],
        compiler_params=_params("parallel", "parallel"),
        name="proj_even",
    )(x, ctx, mod, w, rope_c, rope_s)


def _stack_maps(q):
    qf = q.astype(F32)
    lane = lax.broadcasted_iota(jnp.int32, (1, LANES), 1)
    map1 = (lane % (LANES // 2)) < (LANES // 4)
    return jnp.concatenate([jnp.where(map1, qf, 0.0), jnp.where(map1, 0.0, qf)], axis=0).astype(BF16)


def _lane_block_max(s):
    blk = s[:, :LANES]
    for c in range(1, s.shape[1] // LANES):
        blk = jnp.maximum(blk, s[:, c * LANES:(c + 1) * LANES])
    return blk


def _exp_blocks(s, mb):
    return [jnp.exp2(s[:, c * LANES:(c + 1) * LANES] - mb) for c in range(s.shape[1] // LANES)]


def _diff_merge(o, lp, head_g, lam_init):
    tq = o.shape[0] // 2
    lam = (jnp.exp(jnp.sum(lp[0:1] * lp[1:2], keepdims=True))
           - jnp.exp(jnp.sum(lp[2:3] * lp[3:4], keepdims=True)) + lam_init)
    a = o[:tq] - lam * o[tq:]
    a_n = a * lax.rsqrt(jnp.mean(a * a, axis=-1, keepdims=True) + LN_EPS) * (head_g * (1.0 - lam_init))
    return a_n.astype(BF16)


def _attn_kernel(lp_ref, g_ref, qc_ref, qn_ref, kc_ref, kn_ref, v_ref, o_ref,
                 sa_sc, sb_sc, mcur_sc, mnxt_sc, l_sc, acc_sc, *, tk, nkt, lam_init):
    step = (pl.program_id(0) * pl.num_programs(1) + pl.program_id(1)) * pl.num_programs(2) + pl.program_id(2)
    rows = mcur_sc.shape[0]

    def chunk_off(j):
        return j * tk if isinstance(j, int) else pl.multiple_of(j * tk, tk)

    def score_chunk(qq, k_ref, s_ref, j):
        off = chunk_off(j)
        s = _dot_nt(qq, k_ref[0, 0, pl.ds(off, tk), :])
        s_ref[j] = s
        mnxt_sc[...] = jnp.maximum(mnxt_sc[...], _lane_block_max(s))

    def row_max():
        return jnp.broadcast_to(jnp.max(mnxt_sc[...], axis=1, keepdims=True), (rows, LANES))

    @pl.when(step == 0)
    def _():
        qq0 = _stack_maps(qc_ref[0, 0])
        mnxt_sc[...] = jnp.full(mnxt_sc.shape, -jnp.inf, F32)

        def first(j, carry):
            score_chunk(qq0, kc_ref, sa_sc, j)
            return carry

        lax.fori_loop(0, nkt, first, 0)
        mcur_sc[...] = row_max()

    qqn = _stack_maps(qn_ref[0, 0])
    mnxt_sc[...] = jnp.full(mnxt_sc.shape, -jnp.inf, F32)
    l_sc[...] = jnp.zeros(l_sc.shape, F32)
    acc_sc[...] = jnp.zeros(acc_sc.shape, F32)

    def sweep(cur_ref, nxt_ref):
        def body(j, carry):
            score_chunk(qqn, kn_ref, nxt_ref, j)
            off = chunk_off(j)
            ps = _exp_blocks(cur_ref[j], mcur_sc[...])
            part = ps[0]
            for pc in ps[1:]:
                part = part + pc
            l_sc[...] += part
            acc_sc[...] += _dot(jnp.concatenate(ps, axis=1).astype(BF16), v_ref[0, 0, pl.ds(off, tk), :])
            return carry

        tail = min(nkt, SWEEP_UNROLL + nkt % SWEEP_UNROLL)
        lax.fori_loop(0, nkt - tail, body, 0, unroll=SWEEP_UNROLL)
        for j in range(nkt - tail, nkt):
            body(j, 0)

    @pl.when(step % 2 == 0)
    def _():
        sweep(sa_sc, sb_sc)

    @pl.when(step % 2 == 1)
    def _():
        sweep(sb_sc, sa_sc)

    o = acc_sc[...] / jnp.sum(l_sc[...], axis=1, keepdims=True)
    o_ref[0] = _diff_merge(o, lp_ref[...], g_ref[...], lam_init)
    mcur_sc[...] = row_max()


def _attn_ctx_kernel(lp_ref, g_ref, q_ref, k_ref, v_ref, buf_ref, o_ref, *, lam_init):
    del buf_ref
    s = _dot_nt(_stack_maps(q_ref[0, 0]), k_ref[0, 0])
    p = jnp.exp2(s - jnp.max(s, axis=1, keepdims=True))
    o = _dot(p.astype(BF16), v_ref[0, 0]) / jnp.sum(p, axis=1, keepdims=True)
    o_ref[0] = _diff_merge(o, lp_ref[...], g_ref[...], lam_init)


def _attention(q, k, v, lam_params, head_g, n_lat, lam_init):
    nb, nh, ntot, _ = q.shape
    n_ctx = ntot - n_lat
    tq = min(ATTN_Q_TILE, n_lat)
    tk = TOKEN_TILE
    nq = n_lat // tq
    nkt = ntot // tk
    rows = 2 * tq
    kv_spec = pl.BlockSpec((1, 1, ntot, LANES), lambda b, h, i: (b, h, 0, 0))

    def following(b, h, i):
        s = jnp.minimum((b * nh + h) * nq + i + 1, nb * nh * nq - 1)
        return s // (nh * nq), (s // nq) % nh, s % nq

    def qn_map(b, h, i):
        b2, h2, i2 = following(b, h, i)
        return b2, h2, i2, 0

    def kn_map(b, h, i):
        b2, h2, _ = following(b, h, i)
        return b2, h2, 0, 0

    a_lat = pl.pallas_call(
        functools.partial(_attn_kernel, tk=tk, nkt=nkt, lam_init=lam_init),
        grid=(nb, nh, nq),
        in_specs=[_full(lam_params.shape), _full(head_g.shape),
                  pl.BlockSpec((1, 1, tq, LANES), lambda b, h, i: (b, h, i, 0)),
                  pl.BlockSpec((1, 1, tq, LANES), qn_map),
                  kv_spec, pl.BlockSpec((1, 1, ntot, LANES), kn_map), kv_spec],
        out_specs=pl.BlockSpec((1, tq, LANES), lambda b, h, i: (b, i, h)),
        out_shape=jax.ShapeDtypeStruct((nb, ntot, nh * LANES), BF16),
        scratch_shapes=[pltpu.VMEM((nkt, rows, tk), F32)] * 2 + [pltpu.VMEM((rows, LANES), F32)] * 4,
        compiler_params=_params("arbitrary", "arbitrary", "arbitrary"),
        name="diff_attention",
    )(lam_params, head_g, q, q, k, k, v)
    cblk = n_lat // n_ctx
    ctx_spec = pl.BlockSpec((1, 1, n_ctx, LANES), lambda b, h: (b, h, cblk, 0))
    return pl.pallas_call(
        functools.partial(_attn_ctx_kernel, lam_init=lam_init),
        grid=(nb, nh),
        in_specs=[_full(lam_params.shape), _full(head_g.shape), ctx_spec, ctx_spec, ctx_spec,
                  pl.BlockSpec(memory_space=pl.ANY)],
        out_specs=pl.BlockSpec((1, n_ctx, LANES), lambda b, h: (b, cblk, h)),
        out_shape=jax.ShapeDtypeStruct(a_lat.shape, a_lat.dtype),
        input_output_aliases={5: 0},
        compiler_params=_params("parallel", "parallel"),
        name="diff_attention_ctx",
    )(lam_params, head_g, q, k, v, a_lat)


def _fourier_factor(n):
    a = 1 << (int(math.log2(n)) // 2)
    assert a * (n // a) == n and a % SUBLANES == 0 and (n // a) % SUBLANES == 0
    return a, n // a


def _fourier_tables(n):
    a, bn = _fourier_factor(n)
    j = np.arange(FN_CH)
    ang = 2.0 * np.pi * np.outer(j, j) / FN_CH
    eye = np.eye(FN_GROUPS)
    wc = np.concatenate([np.kron(eye, np.cos(ang)), -np.kron(eye, np.sin(ang))], axis=1)
    ia = np.arange(a)
    ang_a = 2.0 * np.pi * np.outer(ia, ia) / a
    ca, sa = np.cos(ang_a), np.sin(ang_a)
    m1 = np.block([[ca, sa], [-sa, ca]])
    ib = np.arange(bn)
    ang_t = 2.0 * np.pi * np.outer(ib, ia) / n
    ang_b = 2.0 * np.pi * np.outer(ib, ib) / bn
    m3 = np.concatenate([np.cos(ang_b), np.sin(ang_b)], axis=1) / math.sqrt(n * FN_CH)
    tw_c = jnp.broadcast_to(jnp.asarray(np.cos(ang_t), F32)[:, :, None], (bn, a, LANES))
    tw_s = jnp.broadcast_to(jnp.asarray(np.sin(ang_t), F32)[:, :, None], (bn, a, LANES))
    return (jnp.asarray(wc, BF16), jnp.asarray(m1, BF16), tw_c, tw_s, jnp.asarray(m3, BF16))


FOURIER_ROW_TILE = 512


def _fourier_kernel(x_ref, wc_ref, m1_ref, twc_ref, tws_ref, m3_ref, *rest, a, bn):
    o_ref, g_sc, t_sc, y_sc = rest[-4:]
    n = a * bn
    nhalf = FN_COLS // LANES
    rt = min(FOURIER_ROW_TILE, n)

    def halves(v):
        return [v[:, h * LANES:(h + 1) * LANES] for h in range(v.shape[1] // LANES)]

    for t in range(n // rt):
        tile = slice(t * rt, (t + 1) * rt)
        g = _dot(x_ref[0, tile, :].astype(BF16), wc_ref[...])
        for i, part in enumerate(halves(g)):
            g_sc[i, tile, :] = part

    def gather(buf, rows):
        parts = [jnp.concatenate([buf[p * nhalf + h, rows, :] for h in range(nhalf)], axis=1) for p in range(2)]
        return jnp.concatenate(parts, axis=0).astype(BF16)

    def stage1(b, carry):
        t1 = _dot(m1_ref[...], gather(g_sc, pl.ds(b, a, stride=bn)))
        tr, ti = t1[:a], t1[a:]
        c = jnp.concatenate([twc_ref[b]] * nhalf, axis=1)
        s = jnp.concatenate([tws_ref[b]] * nhalf, axis=1)
        dst = pl.ds(pl.multiple_of(b * a, a), a)
        for i, part in enumerate(halves(tr * c + ti * s) + halves(ti * c - tr * s)):
            t_sc[i, dst, :] = part
        return carry

    lax.fori_loop(0, bn, stage1, 0, unroll=4)

    def stage2(ka, carry):
        rows = pl.ds(ka, bn, stride=a)
        y = _dot(m3_ref[...], gather(t_sc, rows))
        for i, part in enumerate(halves(y)):
            y_sc[i, rows, :] = part
        return carry

    lax.fori_loop(0, a, stage2, 0, unroll=4)
    for t in range(n // rt):
        tile = slice(t * rt, (t + 1) * rt)
        o_ref[0, tile, :] = jnp.concatenate([y_sc[h, tile, :] for h in range(nhalf)], axis=1)


def _fourier_mix(fcat, row0, n, out_buf):
    nb, ntot, _ = fcat.shape
    a, bn = _fourier_factor(n)
    tables = _fourier_tables(n)
    assert row0 % n == 0
    seq = pl.BlockSpec((1, n, FN_COLS), lambda b: (b, row0 // n, 0))
    in_specs = [seq] + [_full(t.shape) for t in tables]
    args = [fcat, *tables]
    aliases = {}
    if out_buf is not None:
        in_specs.append(pl.BlockSpec(memory_space=pl.ANY))
        args.append(out_buf)
        aliases = {len(args) - 1: 0}
    return pl.pallas_call(
        functools.partial(_fourier_kernel, a=a, bn=bn),
        grid=(nb,),
        in_specs=in_specs,
        out_specs=seq,
        out_shape=jax.ShapeDtypeStruct((nb, ntot, FN_COLS), F32),
        scratch_shapes=[pltpu.VMEM((2 * (FN_COLS // LANES), n, LANES), F32)] * 2
                       + [pltpu.VMEM((FN_COLS // LANES, n, LANES), F32)],
        input_output_aliases=aliases,
        compiler_params=_params("parallel"),
        name="fourier_mix",
    )(*args)


def _post_norm(h, y, g, b):
    return _ln(ALPHA * h + y) * g + b


def _out_even_kernel(a_ref, f_ref, x_ref, ctx_ref, mod_ref, w_ref, g_ref, b_ref, o_ref, *, n_lat_tiles):
    na = a_ref.shape[-1]
    y = _dot(a_ref[0], w_ref[:na, :]) + _dot(f_ref[0].astype(BF16), w_ref[na:, :])
    h = _stream_tile(x_ref, ctx_ref, n_lat_tiles)
    o_ref[0] = _post_norm(h, mod_ref[0, 2:3, :] * y, g_ref[...], b_ref[...])


def _out_even(a_n, fm, x, ctx, mod, w_out, ln_g, ln_b):
    nb, n_lat, d = x.shape
    ntot = n_lat + ctx.shape[1]
    tm = TOKEN_TILE
    na = a_n.shape[-1]
    tok = lambda c: pl.BlockSpec((1, tm, c), lambda b, i: (b, i, 0))
    return pl.pallas_call(
        functools.partial(_out_even_kernel, n_lat_tiles=n_lat // tm),
        grid=(nb, ntot // tm),
        in_specs=[tok(na), tok(FN_COLS)] + _stream_specs(tm, d, n_lat // tm) + [
                  _mod_spec(nb, n_lat // tm, d), _full(w_out.shape), _full((1, d)), _full((1, d))],
        out_specs=tok(d),
        out_shape=jax.ShapeDtypeStruct((nb, ntot, d), F32),
        compiler_params=_params("parallel", "parallel"),
        name="out_even",
    )(a_n, fm, x, ctx, mod, w_out, ln_g, ln_b)


def _ffn_kernel(h_ref, mod_ref, w1_ref, w3_ref, w2_ref, g_ref, b_ref, o_ref):
    h = h_ref[0]
    u = (_ln(h) * (1.0 + mod_ref[0, 4:5, :]) + mod_ref[0, 3:4, :]).astype(BF16)
    hid = (_silu(_dot(u, w1_ref[...])) * _dot(u, w3_ref[...])).astype(BF16)
    y = _dot(hid, w2_ref[...])
    o_ref[0] = _post_norm(h, mod_ref[0, 5:6, :] * y, g_ref[...], b_ref[...])


def _ffn(h, n_rows, mod, w1, w3, w2, ln_g, ln_b, n_lat):
    nb, _, d = h.shape
    tm = TOKEN_TILE
    tok = pl.BlockSpec((1, tm, d), lambda b, i: (b, i, 0))
    resident = lambda s: pl.BlockSpec(s, lambda b, i: (0, 0), pipeline_mode=pl.Buffered(1))
    return pl.pallas_call(
        _ffn_kernel,
        grid=(nb, n_rows // tm),
        in_specs=[tok, _mod_spec(nb, n_lat // tm, d),
                  resident(w1.shape), resident(w3.shape), resident(w2.shape),
                  _full((1, d)), _full((1, d))],
        out_specs=tok,
        out_shape=jax.ShapeDtypeStruct((nb, n_rows, d), F32),
        compiler_params=_params("parallel", "parallel"),
        name="ffn",
    )(h, mod, w1, w3, w2, ln_g, ln_b)


def _proj_odd_kernel(h_ref, mod_ref, w_ref, xm_ref, z_ref):
    x = h_ref[0]
    u = (_ln(x) * (1.0 + mod_ref[0, 1:2, :]) + mod_ref[0, 0:1, :]).astype(BF16)
    p = _dot(u, w_ref[...])
    inner = xm_ref.shape[-1]
    xm_ref[0] = p[:, :inner]
    z_ref[0] = p[:, inner:]


def _proj_odd(hcat, mod, w, n_lat):
    nb, ntot, d = hcat.shape
    tm = TOKEN_TILE
    inner = w.shape[1] // 2
    out_spec = pl.BlockSpec((1, tm, inner), lambda b, i: (b, i, 0))
    out_shape = jax.ShapeDtypeStruct((nb, ntot, inner), F32)
    return pl.pallas_call(
        _proj_odd_kernel,
        grid=(nb, ntot // tm),
        in_specs=[pl.BlockSpec((1, tm, d), lambda b, i: (b, i, 0)),
                  _mod_spec(nb, n_lat // tm, d),
                  _full(w.shape)],
        out_specs=[out_spec, out_spec],
        out_shape=[out_shape, out_shape],
        compiler_params=_params("parallel", "parallel"),
        name="proj_odd",
    )(hcat, mod, w)


def _gate_fold_kernel(wg_ref, wq_ref, wk_ref, wv_ref, gc_ref, gm_ref):
    dh = wq_ref.shape[-1]
    wg = wg_ref[0]
    gc_ref[0] = (_dot_nt(wg[:, :dh], wq_ref[0]) + _dot_nt(wg[:, dh:2 * dh], wk_ref[0])).astype(BF16)
    gm_ref[0] = _dot_nt(wg[:, 2 * dh:], wv_ref[0]).astype(BF16)


def _gate_fold(wg, wq, wk, wv):
    nh, n_gates, _ = wg.shape
    dh = wq.shape[-1]
    wblk = pl.BlockSpec((1, dh, dh), lambda h: (h, 0, 0))
    gspec = pl.BlockSpec((1, n_gates, dh), lambda h: (h, 0, 0))
    gshape = jax.ShapeDtypeStruct((nh, n_gates, dh), BF16)
    return pl.pallas_call(
        _gate_fold_kernel,
        grid=(nh,),
        in_specs=[pl.BlockSpec((1, n_gates, 3 * dh), lambda h: (h, 0, 0)), wblk, wblk, wblk],
        out_specs=[gspec, gspec],
        out_shape=[gshape, gshape],
        compiler_params=_params("parallel"),
        name="mlstm_gate_fold",
    )(wg, wq, wk, wv)


def _mfeat_kernel(xm_ref, cw_ref, cb_ref, wk_ref, wqt_ref, wvt_ref,
                  xc_ref, k_ref, qt_ref, vt_ref, gr_ref, *, n_lat, chunk, row_group):
    ntot = xm_ref.shape[1]
    dh = wk_ref.shape[-1]
    ct = CONV_TILE
    half = ML_CONV_W // 2
    cw = cw_ref[...]
    cb = cb_ref[...]

    def conv_tile(t, carry):
        r0 = pl.multiple_of(t * ct, ct)
        at_start = jnp.logical_or(r0 == 0, r0 == n_lat)
        at_end = jnp.logical_or(r0 + ct == n_lat, r0 + ct == ntot)
        p0 = pl.multiple_of(jnp.maximum(r0 - SUBLANES, 0), SUBLANES)
        n0 = pl.multiple_of(jnp.minimum(r0 + ct, ntot - SUBLANES), SUBLANES)
        prev = xm_ref[0, pl.ds(p0, SUBLANES), :]
        nxt = xm_ref[0, pl.ds(n0, SUBLANES), :]
        prev = jnp.where(at_start, 0.0, prev)
        nxt = jnp.where(at_end, 0.0, nxt)
        xe = jnp.concatenate([prev, xm_ref[0, pl.ds(r0, ct), :], nxt], axis=0)
        acc = cb
        for dd in range(ML_CONV_W):
            lo = SUBLANES - half + dd
            acc = acc + cw[dd:dd + 1, :] * xe[lo:lo + ct, :]
        xc_ref[0, pl.ds(r0, ct), :] = _silu(acc)
        return carry

    lax.fori_loop(0, ntot // ct, conv_tile, 0)

    for g in range(ntot // row_group):
        rows = pl.ds(g * row_group, row_group)
        k_ref[0, rows, :] = _dot(xc_ref[0, rows, :].astype(BF16), wk_ref[0]).astype(BF16)

    def t_chunk(c, carry):
        r0 = pl.multiple_of(c * chunk, chunk)
        tq = _dot_nt(wqt_ref[0], xc_ref[0, pl.ds(r0, chunk), :].astype(BF16))
        tv = _dot_nt(wvt_ref[0], xm_ref[0, pl.ds(r0, chunk), :].astype(BF16))
        qt_ref[0, 0, c] = tq[:dh].astype(BF16)
        vt_ref[0, 0, c] = tv[:dh].astype(BF16)
        gr_ref[0, 0, c] = tq[dh:] + tv[dh:]
        return carry

    lax.fori_loop(0, ntot // chunk, t_chunk, 0, unroll=2)


def _mfeat(xm, conv_w, conv_b, wk, wqt_g, wvt_g, n_lat, chunk):
    nb, ntot, inner = xm.shape
    nh = ML_HEADS
    dh = inner // nh
    nct = ntot // chunk
    n_gates = wqt_g.shape[1] - dh
    ng = 2 if ntot % (2 * SUBLANES) == 0 else 1
    kern = functools.partial(_mfeat_kernel, n_lat=n_lat, chunk=chunk, row_group=ntot // ng)
    seq = pl.BlockSpec((1, ntot, dh), lambda b, h: (b, 0, h))
    wstack = pl.BlockSpec((1, dh + n_gates, dh), lambda b, h: (h, 0, 0))
    tspec = pl.BlockSpec((1, 1, nct, dh, chunk), lambda b, h: (b, h, 0, 0, 0))
    tshape = jax.ShapeDtypeStruct((nb, nh, nct, dh, chunk), BF16)
    return pl.pallas_call(
        kern,
        grid=(nb, nh),
        in_specs=[seq,
                  pl.BlockSpec((ML_CONV_W, dh), lambda b, h: (0, h)),
                  pl.BlockSpec((1, dh), lambda b, h: (0, h)),
                  pl.BlockSpec((1, dh, dh), lambda b, h: (h, 0, 0)), wstack, wstack],
        out_specs=[seq, seq, tspec, tspec,
                   pl.BlockSpec((1, 1, nct, n_gates, chunk), lambda b, h: (b, h, 0, 0, 0))],
        out_shape=[jax.ShapeDtypeStruct((nb, ntot, inner), F32),
                   jax.ShapeDtypeStruct((nb, ntot, inner), BF16), tshape, tshape,
                   jax.ShapeDtypeStruct((nb, nh, nct, n_gates, chunk), F32)],
        compiler_params=_params("parallel", "parallel"),
        name="mlstm_features",
    )(xm, conv_w, conv_b, wk, wqt_g, wvt_g)


def _split3(x):
    hi = x.astype(BF16)
    r1 = x - hi.astype(F32)
    mid = r1.astype(BF16)
    lo = (r1 - mid.astype(F32)).astype(BF16)
    return hi, mid, lo


def _running_max(x, reverse):
    n = x.shape[1]
    lane = lax.broadcasted_iota(jnp.int32, x.shape, 1)
    k = 1
    while k < n:
        if reverse:
            shifted = jnp.where(lane < n - k, pltpu.roll(x, n - k, 1), -jnp.inf)
        else:
            shifted = jnp.where(lane >= k, pltpu.roll(x, k, 1), -jnp.inf)
        x = jnp.maximum(x, shifted)
        k *= 2
    return x


def _gates_kernel(g_ref, b_ref, rows_ref, cols_ref):
    nh = ML_HEADS
    nct, chunk = g_ref.shape[2], g_ref.shape[-1]
    bias = b_ref[...]
    li_f, li_b, lf_f, lf_b = [], [], [], []
    for c in range(nct):
        pre = jnp.sum(g_ref[0, :, c], axis=0) + bias
        xf = pre[2 * nh:]
        log_f = jnp.minimum(xf, 0.0) - jnp.log1p(jnp.exp(-jnp.abs(xf)))
        li_f.append(pre[:nh])
        li_b.append(pre[nh:2 * nh])
        lf_f.append(log_f[:nh])
        lf_b.append(log_f[nh:])
    s_idx = lax.broadcasted_iota(jnp.int32, (chunk, chunk), 0)
    t_idx = lax.broadcasted_iota(jnp.int32, (chunk, chunk), 1)
    tri_f = jnp.where(s_idx <= t_idx, 1.0, 0.0).astype(BF16)
    tri_b = jnp.where(s_idx >= t_idx, 1.0, 0.0).astype(BF16)
    cum_f = sum(_dot(p, tri_f) for p in _split3(jnp.concatenate(lf_f, axis=0)))
    cum_b = sum(_dot(p, tri_b) for p in _split3(jnp.concatenate(lf_b, axis=0)))
    gap_f = jnp.concatenate(li_f, axis=0) - cum_f
    gap_b = jnp.concatenate(li_b, axis=0) - cum_b
    top_f = _running_max(gap_f, False)
    top_b = _running_max(gap_b, True)
    pad = jnp.zeros((LANES - 2 * nh, chunk), F32)
    for c in range(nct):
        r = slice(c * nh, (c + 1) * nh)
        rows_ref[0, c] = jnp.concatenate([gap_f[r], gap_b[r], top_f[r], top_b[r], cum_f[r], cum_b[r]], axis=0)
        gap_t = jnp.transpose(jnp.concatenate([gap_f[r], gap_b[r], pad], axis=0))
        cols_ref[0, c * chunk:(c + 1) * chunk, :] = gap_t[:, :2 * nh]


def _gates(g_part, bias, chunk):
    nb, nh, nct, n_gates, _ = g_part.shape
    ntot = nct * chunk
    return pl.pallas_call(
        _gates_kernel,
        grid=(nb,),
        in_specs=[pl.BlockSpec((1, nh, nct, n_gates, chunk), lambda b: (b, 0, 0, 0, 0)),
                  _full(bias.shape)],
        out_specs=[pl.BlockSpec((1, nct, 6 * nh, chunk), lambda b: (b, 0, 0, 0)),
                   pl.BlockSpec((1, ntot, 2 * nh), lambda b: (b, 0, 0))],
        out_shape=[jax.ShapeDtypeStruct((nb, nct, 6 * nh, chunk), F32),
                   jax.ShapeDtypeStruct((nb, ntot, 2 * nh), F32)],
        compiler_params=_params("parallel"),
        name="mlstm_gates",
    )(g_part, bias)


STATE_PAD_ROWS = 16


def _scan_kernel(k_ref, qt_ref, vt_ref, rows_ref, cols_ref, hs_ref, c_sc, n_sc, m_sc,
                 *, n_lat_chunks, n_ctx_chunks, chunk):
    nh = ML_HEADS
    h = pl.program_id(1)
    dh = c_sc.shape[-1]
    c_sc[...] = jnp.zeros(c_sc.shape, F32)
    n_sc[...] = jnp.zeros(n_sc.shape, F32)
    m_sc[...] = jnp.zeros(m_sc.shape, F32)
    s_idx = lax.broadcasted_iota(jnp.int32, (chunk, chunk), 0)
    t_idx = lax.broadcasted_iota(jnp.int32, (chunk, chunk), 1)
    gate_lane = lax.broadcasted_iota(jnp.int32, (1, 2 * nh), 1)

    def step(d, cidx, with_out):
        off = pl.multiple_of(cidx * chunk, chunk)
        kc = k_ref[0, pl.ds(off, chunk), :]
        qt = qt_ref[0, 0, cidx]
        vt = vt_ref[0, 0, cidx]
        gap_r = rows_ref[0, cidx, pl.ds(d * nh + h, 1), :]
        top_r = rows_ref[0, cidx, pl.ds((2 + d) * nh + h, 1), :]
        cum_r = rows_ref[0, cidx, pl.ds((4 + d) * nh + h, 1), :]
        m = m_sc[d]
        ct = c_sc[d]
        nrow = n_sc[d]
        last = chunk - 1 if d == 0 else 0
        b_end = cum_r[:, last:last + 1]
        top_end = top_r[:, last:last + 1]
        if with_out is not None:
            ctile = cols_ref[0, pl.ds(off, chunk), :]
            gap_c = jnp.sum(jnp.where(gate_lane == d * nh + h, ctile, 0.0), axis=1, keepdims=True)
            lift = jnp.maximum(top_r, m)
            ordered = (s_idx <= t_idx) if d == 0 else (s_idx >= t_idx)
            dw_t = jnp.exp(jnp.where(ordered, gap_c - lift, -jnp.inf))
            iw = jnp.exp(m - lift)
            nb16 = jnp.broadcast_to(nrow.astype(BF16), (STATE_PAD_ROWS, dh))
            a = _dot(jnp.concatenate([kc, ct.astype(BF16), nb16], axis=0), qt)
            s_t = a[:chunk] * dw_t
            num_t = iw * a[chunk:chunk + dh] + _dot(vt, s_t.astype(BF16))
            den = iw * a[chunk + dh:chunk + dh + 1] + jnp.sum(s_t, axis=0, keepdims=True)
            floor = jnp.exp(-(cum_r + lift))
            h_t = num_t * (1.0 / jnp.maximum(jnp.abs(den), floor))
            hs_ref[0, 0, cidx] = h_t if with_out == "assign" else hs_ref[0, 0, cidx] + h_t
        mx = jnp.maximum(m, top_end)
        keep = jnp.exp(m - mx)
        w_r = jnp.exp(gap_r - mx)
        vw = (vt.astype(F32) * w_r).astype(BF16)
        wb16 = jnp.broadcast_to(w_r.astype(BF16), (STATE_PAD_ROWS, chunk))
        upd = _dot(jnp.concatenate([vw, wb16], axis=0), kc)
        c_sc[d] = keep * ct + upd[:dh]
        n_sc[d] = keep * nrow + upd[dh:dh + 1]
        m_sc[d] = b_end + mx

    def ctx_body(c, carry):
        step(0, n_lat_chunks + c, None)
        step(1, n_lat_chunks + n_ctx_chunks - 1 - c, None)
        return carry

    def lat_body(mode, c, carry):
        step(0, c, mode)
        step(1, n_lat_chunks - 1 - c, mode)
        return carry

    lax.fori_loop(0, n_ctx_chunks, ctx_body, 0)
    half = n_lat_chunks // 2
    lax.fori_loop(0, half, functools.partial(lat_body, "assign"), 0, unroll=4)
    lax.fori_loop(half, n_lat_chunks, functools.partial(lat_body, "add"), 0, unroll=4)


def _scan(k, qt, vt, rows, cols, n_lat, chunk):
    nb, ntot, inner = k.shape
    nh = ML_HEADS
    dh = inner // nh
    nct = ntot // chunk
    ncl = n_lat // chunk
    assert ncl % 2 == 0
    kern = functools.partial(_scan_kernel, n_lat_chunks=ncl, n_ctx_chunks=nct - ncl, chunk=chunk)
    tspec = pl.BlockSpec((1, 1, nct, dh, chunk), lambda b, h: (b, h, 0, 0, 0))
    return pl.pallas_call(
        kern,
        grid=(nb, nh),
        in_specs=[pl.BlockSpec((1, ntot, dh), lambda b, h: (b, 0, h)), tspec, tspec,
                  pl.BlockSpec((1, nct, rows.shape[2], chunk), lambda b, h: (b, 0, 0, 0)),
                  pl.BlockSpec((1, ntot, cols.shape[2]), lambda b, h: (b, 0, 0))],
        out_specs=pl.BlockSpec((1, 1, ncl, dh, chunk), lambda b, h: (b, h, 0, 0, 0)),
        out_shape=jax.ShapeDtypeStruct((nb, nh, ncl, dh, chunk), F32),
        scratch_shapes=[pltpu.VMEM((2, dh, dh), F32), pltpu.VMEM((2, 1, dh), F32),
                        pltpu.VMEM((2, 1, 1), F32)],
        compiler_params=_params("parallel", "parallel"),
        name="mlstm_scan",
    )(k, qt, vt, rows, cols)


def _out_odd_kernel(hs_ref, xc_ref, z_ref, h_ref, mod_ref, skip_ref, hg_ref, w_ref, g_ref, b_ref, o_ref):
    parts = []
    for hd in range(ML_HEADS):
        ht = hs_ref[0, hd, 0]
        mu = jnp.mean(ht, axis=0, keepdims=True)
        hc = ht - mu
        var = jnp.mean(hc * hc, axis=0, keepdims=True)
        parts.append(jnp.transpose(hc * lax.rsqrt(var + LN_EPS)))
    hn = jnp.concatenate(parts, axis=1)
    y = (hn * hg_ref[...] + skip_ref[...] * xc_ref[0]) * _silu(z_ref[0])
    yo = _dot(y.astype(BF16), w_ref[...])
    o_ref[0] = _post_norm(h_ref[0], mod_ref[0, 2:3, :] * yo, g_ref[...], b_ref[...])


def _out_odd(hs, xc, z, hcat, mod, skip, head_g, w_out, ln_g, ln_b, n_lat):
    nb, _, d = hcat.shape
    inner = xc.shape[-1]
    _, nh, _, dh, tm = hs.shape
    tok = lambda c: pl.BlockSpec((1, tm, c), lambda b, i: (b, i, 0))
    return pl.pallas_call(
        _out_odd_kernel,
        grid=(nb, n_lat // tm),
        in_specs=[pl.BlockSpec((1, nh, 1, dh, tm), lambda b, i: (b, 0, i, 0, 0)),
                  tok(inner), tok(inner), tok(d),
                  pl.BlockSpec((1, 6, d), lambda b, i: (b, 0, 0)),
                  _full((1, inner)), _full((1, inner)), _full(w_out.shape),
                  _full((1, d)), _full((1, d))],
        out_specs=tok(d),
        out_shape=jax.ShapeDtypeStruct((nb, n_lat, d), F32),
        compiler_params=_params("parallel", "parallel"),
        name="out_odd",
    )(hs, xc, z, hcat, mod, skip, head_g, w_out, ln_g, ln_b)


def _even_in_columns():
    cols, scale = [], []
    quarter = DA_DK // 2
    for hd in range(DA_HEADS):
        base = hd * DA_HEAD_COLS
        for blk in range(2):
            b0 = base + blk * 2 * DA_DK
            for half in range(2):
                for m in range(2):
                    cols += [b0 + m * DA_DK + half * quarter + j for j in range(quarter)]
            scale += [DA_DK ** -0.5 * math.log2(math.e) if blk == 0 else 1.0] * (2 * DA_DK)
        cols += list(range(base + 4 * DA_DK, base + DA_HEAD_COLS))
        scale += [1.0] * DA_DV
    cols += list(range(DA_HEADS * DA_HEAD_COLS, DA_HEADS * DA_HEAD_COLS + FN_COLS))
    scale += [1.0] * FN_COLS
    return np.asarray(cols, np.int32), np.asarray(scale, np.float32)


def _rope_tables(n_lat, ntot):
    rows = n_lat // GRID_W
    row = jnp.repeat(jnp.arange(rows, dtype=F32), GRID_W)
    col = jnp.tile(jnp.arange(GRID_W, dtype=F32), rows)
    n_freq = DA_DK // 4
    inv_freq = ROPE_BASE ** (-jnp.arange(n_freq, dtype=F32) / n_freq)
    ang = jnp.concatenate([row[:, None] * inv_freq, col[:, None] * inv_freq], -1)
    cos, sin = jnp.cos(ang), jnp.sin(ang)
    c = jnp.concatenate([cos, cos, cos, cos], axis=1)
    s = jnp.concatenate([-sin, -sin, sin, sin], axis=1)
    n_ctx = ntot - n_lat
    c = jnp.concatenate([c, jnp.ones((n_ctx, LANES), F32)], axis=0)
    s = jnp.concatenate([s, jnp.zeros((n_ctx, LANES), F32)], axis=0)
    return c, s


def _block_diag_heads(w, dh):
    nblk = w.shape[0]
    per_head = dh // ML_BLOCK
    wh = w.reshape(nblk // per_head, per_head, ML_BLOCK, ML_BLOCK)
    eye = jnp.eye(per_head, dtype=w.dtype)
    dense = jnp.einsum('hgij,gk->hgikj', wh, eye)
    return dense.reshape(nblk // per_head, dh, dh)


def kernel(x, c, ctx, c_ctx, w_mod, b_mod, ln_g, ln_b, w_ff1, w_ff3, w_ff2, a_w_in, a_w_out, da_lq1, da_lk1, da_lq2, da_lk2, da_head_g, m_w_in, m_w_out, m_conv_w, m_conv_b, m_wq, m_wk, m_wv, m_w_ig, m_b_ig, m_w_fg, m_b_fg, m_skip, m_head_g):
    nb, n_lat, d = x.shape
    n_ctx = ctx.shape[1]
    ntot = n_lat + n_ctx
    assert w_mod.shape[0] == DEPTH == 2
    assert n_lat % TOKEN_TILE == 0 and n_ctx % TOKEN_TILE == 0 and n_lat % n_ctx == 0 and n_lat % GRID_W == 0

    r = -(-(nb + 1) // MOD_ROWS_PAD) * MOD_ROWS_PAD
    cvec = jnp.concatenate([c, c_ctx[None, :], jnp.zeros((r - nb - 1, d), F32)], axis=0)
    mod_all = _modulation(cvec, w_mod, b_mod).reshape(DEPTH, r, 6, d)

    lam_init0 = 0.8 - 0.6 * math.exp(-0.3 * 0)
    cols, colscale = _even_in_columns()
    w_in = (a_w_in[0][:, cols] * colscale[None, :]).astype(BF16)
    rope_c, rope_s = _rope_tables(n_lat, ntot)
    q, k, v, f = _proj_even(x, ctx, mod_all[0], w_in, rope_c, rope_s)
    lam_params = jnp.stack([da_lq1[0], da_lk1[0], da_lq2[0], da_lk2[0]], axis=0)
    a_n = _attention(q, k, v, lam_params, da_head_g[0][None, :], n_lat, lam_init0)
    fm = _fourier_mix(f, 0, n_lat, None)
    fm = _fourier_mix(f, n_lat, n_ctx, fm)
    h1 = _out_even(a_n, fm, x, ctx, mod_all[0], a_w_out[0].astype(BF16),
                   ln_g[0, 0][None, :], ln_b[0, 0][None, :])
    hcat = _ffn(h1, ntot, mod_all[0], w_ff1[0].astype(BF16), w_ff3[0].astype(BF16), w_ff2[0].astype(BF16),
                ln_g[0, 1][None, :], ln_b[0, 1][None, :], n_lat)

    inner = m_w_in.shape[2] // 2
    dh = inner // ML_HEADS
    chunk = MXU_DIM if (n_lat % MXU_DIM == 0 and n_ctx % MXU_DIM == 0) else LANES
    xm, z = _proj_odd(hcat, mod_all[1], m_w_in[0].astype(BF16), n_lat)
    wq_f = _block_diag_heads(m_wq[0], dh)
    wk_f = _block_diag_heads(m_wk[0], dh) * (dh ** -0.5)
    wv_f = _block_diag_heads(m_wv[0], dh)
    wq, wk, wv = wq_f.astype(BF16), wk_f.astype(BF16), wv_f.astype(BF16)
    wg_all = jnp.concatenate([m_w_ig[0, 0], m_w_ig[0, 1], m_w_fg[0, 0], m_w_fg[0, 1]], axis=1)
    wg_all = wg_all.reshape(3, ML_HEADS, dh, 4 * ML_HEADS) * jnp.asarray([1.0, dh ** 0.5, 1.0], F32)[:, None, None, None]
    wg = jnp.transpose(wg_all, (1, 3, 0, 2)).reshape(ML_HEADS, 4 * ML_HEADS, 3 * dh).astype(BF16)
    g_bias = jnp.concatenate([m_b_ig[0, 0], m_b_ig[0, 1], m_b_fg[0, 0], m_b_fg[0, 1]])[:, None]
    gc, gm = _gate_fold(wg, wq, wk, wv)
    wqt_g = jnp.concatenate([jnp.swapaxes(wq_f, 1, 2).astype(BF16), gc], axis=1)
    wvt_g = jnp.concatenate([jnp.swapaxes(wv_f, 1, 2).astype(BF16), gm], axis=1)
    xc, km, qt, vt, g_part = _mfeat(xm, m_conv_w[0], m_conv_b[0][None, :], wk, wqt_g, wvt_g, n_lat, chunk)
    rows, colsg = _gates(g_part, g_bias, chunk)
    hs = _scan(km, qt, vt, rows, colsg, n_lat, chunk)
    h1 = _out_odd(hs, xc, z, hcat, mod_all[1], m_skip[0][None, :], m_head_g[0][None, :],
                  m_w_out[0].astype(BF16), ln_g[1, 0][None, :], ln_b[1, 0][None, :], n_lat)
    return _ffn(h1, n_lat, mod_all[1], w_ff1[1].astype(BF16), w_ff3[1].astype(BF16), w_ff2[1].astype(BF16),
                ln_g[1, 1][None, :], ln_b[1, 1][None, :], n_lat)
```

```python
import functools
import math

import numpy as np
import jax
import jax.numpy as jnp
from jax import lax
from jax.experimental import pallas as pl
from jax.experimental.pallas import tpu as pltpu

F32 = jnp.float32
BF16 = jnp.bfloat16

DA_HEADS = 6
DA_DK = 64
DA_DV = 2 * DA_DK
DA_HEAD_COLS = 4 * DA_DK + DA_DV
FN_GROUPS = 4
FN_CH = 64
FN_COLS = FN_GROUPS * FN_CH
ML_HEADS = 8
ML_BLOCK = 4
ML_CONV_W = 5
GRID_W = 64
ROPE_BASE = 10000.0
LN_EPS = 1e-5
LOG2_E = math.log2(math.e)
DEPTH = 2
ALPHA = (2 * DEPTH) ** 0.25

LANES = 128
SUBLANES = 8
MXU_DIM = 256
VMEM_LIMIT_BYTES = 60 * 1024 * 1024

TOKEN_TILE = 256
CONV_TILE = 128
MOD_ROWS_PAD = 8
ATTN_Q_TILE = 512
SWEEP_UNROLL = 8


def _params(*sem):
    return pltpu.CompilerParams(dimension_semantics=sem, vmem_limit_bytes=VMEM_LIMIT_BYTES)


def _ln(x):
    mu = jnp.mean(x, axis=-1, keepdims=True)
    xc = x - mu
    var = jnp.mean(xc * xc, axis=-1, keepdims=True)
    return xc * lax.rsqrt(var + LN_EPS)


def _silu(x):
    return x * jax.nn.sigmoid(x)


def _dot(a, b):
    return jnp.dot(a, b, preferred_element_type=F32)


def _dot_nt(a, b):
    return lax.dot_general(a, b, (((1,), (1,)), ((), ())), preferred_element_type=F32)


def _full(shape):
    n = len(shape)
    return pl.BlockSpec(shape, lambda *_: (0,) * n)


def _mod_kernel(c_ref, w_ref, b_ref, o_ref):
    s = _silu(c_ref[...]).astype(BF16)
    o_ref[0] = _dot(s, w_ref[0].astype(BF16)) + b_ref[0]


def _modulation(cvec, w_mod, b_mod):
    depth, d, d6 = w_mod.shape
    r = cvec.shape[0]
    tn = d6 // 4
    return pl.pallas_call(
        _mod_kernel,
        grid=(depth, d6 // tn),
        in_specs=[pl.BlockSpec((r, d), lambda l, j: (0, 0)),
                  pl.BlockSpec((1, d, tn), lambda l, j: (l, 0, j)),
                  pl.BlockSpec((1, 1, tn), lambda l, j: (l, 0, j))],
        out_specs=pl.BlockSpec((1, r, tn), lambda l, j: (l, 0, j)),
        out_shape=jax.ShapeDtypeStruct((depth, r, d6), F32),
        compiler_params=_params("parallel", "parallel"),
        name="modulation",
    )(cvec, w_mod, b_mod.reshape(depth, 1, d6))


def _mod_spec(nb, n_lat_tiles, d):
    return pl.BlockSpec((1, 6, d), lambda b, i: (jnp.where(i < n_lat_tiles, b, nb), 0, 0))


def _stream_tile(x_ref, ctx_ref, n_lat_tiles):
    return jnp.where(pl.program_id(1) < n_lat_tiles, x_ref[0], ctx_ref[0])


def _stream_specs(tm, d, n_lat_tiles):
    return [pl.BlockSpec((1, tm, d), lambda b, i: (b, jnp.minimum(i, n_lat_tiles - 1), 0)),
            pl.BlockSpec((1, tm, d), lambda b, i: (b, jnp.maximum(i - n_lat_tiles, 0), 0))]


def _proj_even_kernel(x_ref, ctx_ref, mod_ref, w_ref, c_ref, s_ref, q_ref, k_ref, v_ref, f_ref, *, n_lat_tiles):
    x = _stream_tile(x_ref, ctx_ref, n_lat_tiles)
    u = (_ln(x) * (1.0 + mod_ref[0, 1:2, :]) + mod_ref[0, 0:1, :]).astype(BF16)
    p = _dot(u, w_ref[...])
    c = c_ref[...]
    s = s_ref[...]
    blk = DA_HEAD_COLS
    for hd in range(DA_HEADS):
        pq = p[:, hd * blk: hd * blk + LANES]
        pk = p[:, hd * blk + LANES: hd * blk + 2 * LANES]
        q_ref[0, hd] = (pq * c + pltpu.roll(pq, LANES // 2, 1) * s).astype(BF16)
        k_ref[0, hd] = (pk * c + pltpu.roll(pk, LANES // 2, 1) * s).astype(BF16)
        v_ref[0, hd] = p[:, hd * blk + 2 * LANES: (hd + 1) * blk].astype(BF16)
    f_ref[0] = p[:, DA_HEADS * blk:]


def _proj_even(x, ctx, mod, w, rope_c, rope_s):
    nb, n_lat, d = x.shape
    ntot = n_lat + ctx.shape[1]
    tm = TOKEN_TILE
    nt = ntot // tm
    ncols = w.shape[1]
    qkv_shape = jax.ShapeDtypeStruct((nb, DA_HEADS, ntot, LANES), BF16)
    qkv_spec = pl.BlockSpec((1, DA_HEADS, tm, LANES), lambda b, i: (b, 0, i, 0))
    return pl.pallas_call(
        functools.partial(_proj_even_kernel, n_lat_tiles=n_lat // tm),
        grid=(nb, nt),
        in_specs=_stream_specs(tm, d, n_lat // tm) + [
                  _mod_spec(nb, n_lat // tm, d),
                  _full((d, ncols)),
                  pl.BlockSpec((tm, LANES), lambda b, i: (i, 0)),
                  pl.BlockSpec((tm, LANES), lambda b, i: (i, 0))],
        out_specs=[qkv_spec, qkv_spec, qkv_spec,
                   pl.BlockSpec((1, tm, FN_COLS), lambda b, i: (b, i, 0))],
        out_shape=[qkv_shape, qkv_shape, qkv_shape,
                   jax.ShapeDtypeStruct((nb, ntot, FN_COLS), F32)],
        compiler_params=_params("parallel", "parallel"),
        name="proj_even",
    )(x, ctx, mod, w, rope_c, rope_s)


def _stack_maps(q):
    qf = q.astype(F32)
    lane = lax.broadcasted_iota(jnp.int32, (1, LANES), 1)
    map1 = (lane % (LANES // 2)) < (LANES // 4)
    return jnp.concatenate([jnp.where(map1, qf, 0.0), jnp.where(map1, 0.0, qf)], axis=0).astype(BF16)


def _lane_block_max(s):
    blk = s[:, :LANES]
    for c in range(1, s.shape[1] // LANES):
        blk = jnp.maximum(blk, s[:, c * LANES:(c + 1) * LANES])
    return blk


def _exp_blocks(s, mb):
    return [jnp.exp2(s[:, c * LANES:(c + 1) * LANES] - mb) for c in range(s.shape[1] // LANES)]


def _diff_merge(o, lp, head_g, lam_init):
    tq = o.shape[0] // 2
    lam = (jnp.exp(jnp.sum(lp[0:1] * lp[1:2], keepdims=True))
           - jnp.exp(jnp.sum(lp[2:3] * lp[3:4], keepdims=True)) + lam_init)
    a = o[:tq] - lam * o[tq:]
    a_n = a * lax.rsqrt(jnp.mean(a * a, axis=-1, keepdims=True) + LN_EPS) * (head_g * (1.0 - lam_init))
    return a_n.astype(BF16)


def _attn_kernel(lp_ref, g_ref, qc_ref, qn_ref, kc_ref, kn_ref, v_ref, o_ref,
                 sa_sc, sb_sc, mcur_sc, mnxt_sc, l_sc, acc_sc, *, tk, nkt, lam_init):
    step = (pl.program_id(0) * pl.num_programs(1) + pl.program_id(1)) * pl.num_programs(2) + pl.program_id(2)
    rows = mcur_sc.shape[0]

    def chunk_off(j):
        return j * tk if isinstance(j, int) else pl.multiple_of(j * tk, tk)

    def score_chunk(qq, k_ref, s_ref, j):
        off = chunk_off(j)
        s = _dot_nt(qq, k_ref[0, 0, pl.ds(off, tk), :])
        s_ref[j] = s
        mnxt_sc[...] = jnp.maximum(mnxt_sc[...], _lane_block_max(s))

    def row_max():
        return jnp.broadcast_to(jnp.max(mnxt_sc[...], axis=1, keepdims=True), (rows, LANES))

    @pl.when(step == 0)
    def _():
        qq0 = _stack_maps(qc_ref[0, 0])
        mnxt_sc[...] = jnp.full(mnxt_sc.shape, -jnp.inf, F32)

        def first(j, carry):
            score_chunk(qq0, kc_ref, sa_sc, j)
            return carry

        lax.fori_loop(0, nkt, first, 0)
        mcur_sc[...] = row_max()

    qqn = _stack_maps(qn_ref[0, 0])
    mnxt_sc[...] = jnp.full(mnxt_sc.shape, -jnp.inf, F32)
    l_sc[...] = jnp.zeros(l_sc.shape, F32)
    acc_sc[...] = jnp.zeros(acc_sc.shape, F32)

    def sweep(cur_ref, nxt_ref):
        def body(j, carry):
            score_chunk(qqn, kn_ref, nxt_ref, j)
            off = chunk_off(j)
            ps = _exp_blocks(cur_ref[j], mcur_sc[...])
            part = ps[0]
            for pc in ps[1:]:
                part = part + pc
            l_sc[...] += part
            acc_sc[...] += _dot(jnp.concatenate(ps, axis=1).astype(BF16), v_ref[0, 0, pl.ds(off, tk), :])
            return carry

        tail = min(nkt, SWEEP_UNROLL + nkt % SWEEP_UNROLL)
        lax.fori_loop(0, nkt - tail, body, 0, unroll=SWEEP_UNROLL)
        for j in range(nkt - tail, nkt):
            body(j, 0)

    @pl.when(step % 2 == 0)
    def _():
        sweep(sa_sc, sb_sc)

    @pl.when(step % 2 == 1)
    def _():
        sweep(sb_sc, sa_sc)

    o = acc_sc[...] / jnp.sum(l_sc[...], axis=1, keepdims=True)
    o_ref[0] = _diff_merge(o, lp_ref[...], g_ref[...], lam_init)
    mcur_sc[...] = row_max()


def _attn_ctx_kernel(lp_ref, g_ref, q_ref, k_ref, v_ref, buf_ref, o_ref, *, lam_init):
    del buf_ref
    s = _dot_nt(_stack_maps(q_ref[0, 0]), k_ref[0, 0])
    p = jnp.exp2(s - jnp.max(s, axis=1, keepdims=True))
    o = _dot(p.astype(BF16), v_ref[0, 0]) / jnp.sum(p, axis=1, keepdims=True)
    o_ref[0] = _diff_merge(o, lp_ref[...], g_ref[...], lam_init)


def _attention(q, k, v, lam_params, head_g, n_lat, lam_init):
    nb, nh, ntot, _ = q.shape
    n_ctx = ntot - n_lat
    tq = min(ATTN_Q_TILE, n_lat)
    tk = TOKEN_TILE
    nq = n_lat // tq
    nkt = ntot // tk
    rows = 2 * tq
    kv_spec = pl.BlockSpec((1, 1, ntot, LANES), lambda b, h, i: (b, h, 0, 0))

    def following(b, h, i):
        s = jnp.minimum((b * nh + h) * nq + i + 1, nb * nh * nq - 1)
        return s // (nh * nq), (s // nq) % nh, s % nq

    def qn_map(b, h, i):
        b2, h2, i2 = following(b, h, i)
        return b2, h2, i2, 0

    def kn_map(b, h, i):
        b2, h2, _ = following(b, h, i)
        return b2, h2, 0, 0

    a_lat = pl.pallas_call(
        functools.partial(_attn_kernel, tk=tk, nkt=nkt, lam_init=lam_init),
        grid=(nb, nh, nq),
        in_specs=[_full(lam_params.shape), _full(head_g.shape),
                  pl.BlockSpec((1, 1, tq, LANES), lambda b, h, i: (b, h, i, 0)),
                  pl.BlockSpec((1, 1, tq, LANES), qn_map),
                  kv_spec, pl.BlockSpec((1, 1, ntot, LANES), kn_map), kv_spec],
        out_specs=pl.BlockSpec((1, tq, LANES), lambda b, h, i: (b, i, h)),
        out_shape=jax.ShapeDtypeStruct((nb, ntot, nh * LANES), BF16),
        scratch_shapes=[pltpu.VMEM((nkt, rows, tk), F32)] * 2 + [pltpu.VMEM((rows, LANES), F32)] * 4,
        compiler_params=_params("arbitrary", "arbitrary", "arbitrary"),
        name="diff_attention",
    )(lam_params, head_g, q, q, k, k, v)
    cblk = n_lat // n_ctx
    ctx_spec = pl.BlockSpec((1, 1, n_ctx, LANES), lambda b, h: (b, h, cblk, 0))
    return pl.pallas_call(
        functools.partial(_attn_ctx_kernel, lam_init=lam_init),
        grid=(nb, nh),
        in_specs=[_full(lam_params.shape), _full(head_g.shape), ctx_spec, ctx_spec, ctx_spec,
                  pl.BlockSpec(memory_space=pl.ANY)],
        out_specs=pl.BlockSpec((1, n_ctx, LANES), lambda b, h: (b, cblk, h)),
        out_shape=jax.ShapeDtypeStruct(a_lat.shape, a_lat.dtype),
        input_output_aliases={5: 0},
        compiler_params=_params("parallel", "parallel"),
        name="diff_attention_ctx",
    )(lam_params, head_g, q, k, v, a_lat)


def _fourier_factor(n):
    a = 1 << (int(math.log2(n)) // 2)
    assert a * (n // a) == n and a % SUBLANES == 0 and (n // a) % SUBLANES == 0
    return a, n // a


def _fourier_tables(n):
    a, bn = _fourier_factor(n)
    j = np.arange(FN_CH)
    ang = 2.0 * np.pi * np.outer(j, j) / FN_CH
    eye = np.eye(FN_GROUPS)
    wc = np.concatenate([np.kron(eye, np.cos(ang)), -np.kron(eye, np.sin(ang))], axis=1)
    ia = np.arange(a)
    ang_a = 2.0 * np.pi * np.outer(ia, ia) / a
    ca, sa = np.cos(ang_a), np.sin(ang_a)
    m1 = np.block([[ca, sa], [-sa, ca]])
    ib = np.arange(bn)
    ang_t = 2.0 * np.pi * np.outer(ib, ia) / n
    ang_b = 2.0 * np.pi * np.outer(ib, ib) / bn
    m3 = np.concatenate([np.cos(ang_b), np.sin(ang_b)], axis=1) / math.sqrt(n * FN_CH)
    tw_c = jnp.broadcast_to(jnp.asarray(np.cos(ang_t), F32)[:, :, None], (bn, a, LANES))
    tw_s = jnp.broadcast_to(jnp.asarray(np.sin(ang_t), F32)[:, :, None], (bn, a, LANES))
    return (jnp.asarray(wc, BF16), jnp.asarray(m1, BF16), tw_c, tw_s, jnp.asarray(m3, BF16))


FOURIER_ROW_TILE = 512


def _fourier_kernel(x_ref, wc_ref, m1_ref, twc_ref, tws_ref, m3_ref, *rest, a, bn):
    o_ref, g_sc, t_sc, y_sc = rest[-4:]
    n = a * bn
    nhalf = FN_COLS // LANES
    rt = min(FOURIER_ROW_TILE, n)

    def halves(v):
        return [v[:, h * LANES:(h + 1) * LANES] for h in range(v.shape[1] // LANES)]

    for t in range(n // rt):
        tile = slice(t * rt, (t + 1) * rt)
        g = _dot(x_ref[0, tile, :].astype(BF16), wc_ref[...])
        for i, part in enumerate(halves(g)):
            g_sc[i, tile, :] = part

    def gather(buf, rows):
        parts = [jnp.concatenate([buf[p * nhalf + h, rows, :] for h in range(nhalf)], axis=1) for p in range(2)]
        return jnp.concatenate(parts, axis=0).astype(BF16)

    def stage1(b, carry):
        t1 = _dot(m1_ref[...], gather(g_sc, pl.ds(b, a, stride=bn)))
        tr, ti = t1[:a], t1[a:]
        c = jnp.concatenate([twc_ref[b]] * nhalf, axis=1)
        s = jnp.concatenate([tws_ref[b]] * nhalf, axis=1)
        dst = pl.ds(pl.multiple_of(b * a, a), a)
        for i, part in enumerate(halves(tr * c + ti * s) + halves(ti * c - tr * s)):
            t_sc[i, dst, :] = part
        return carry

    lax.fori_loop(0, bn, stage1, 0, unroll=4)

    def stage2(ka, carry):
        rows = pl.ds(ka, bn, stride=a)
        y = _dot(m3_ref[...], gather(t_sc, rows))
        for i, part in enumerate(halves(y)):
            y_sc[i, rows, :] = part
        return carry

    lax.fori_loop(0, a, stage2, 0, unroll=4)
    for t in range(n // rt):
        tile = slice(t * rt, (t + 1) * rt)
        o_ref[0, tile, :] = jnp.concatenate([y_sc[h, tile, :] for h in range(nhalf)], axis=1)


def _fourier_mix(fcat, row0, n, out_buf):
    nb, ntot, _ = fcat.shape
    a, bn = _fourier_factor(n)
    tables = _fourier_tables(n)
    assert row0 % n == 0
    seq = pl.BlockSpec((1, n, FN_COLS), lambda b: (b, row0 // n, 0))
    in_specs = [seq] + [_full(t.shape) for t in tables]
    args = [fcat, *tables]
    aliases = {}
    if out_buf is not None:
        in_specs.append(pl.BlockSpec(memory_space=pl.ANY))
        args.append(out_buf)
        aliases = {len(args) - 1: 0}
    return pl.pallas_call(
        functools.partial(_fourier_kernel, a=a, bn=bn),
        grid=(nb,),
        in_specs=in_specs,
        out_specs=seq,
        out_shape=jax.ShapeDtypeStruct((nb, ntot, FN_COLS), F32),
        scratch_shapes=[pltpu.VMEM((2 * (FN_COLS // LANES), n, LANES), F32)] * 2
                       + [pltpu.VMEM((FN_COLS // LANES, n, LANES), F32)],
        input_output_aliases=aliases,
        compiler_params=_params("parallel"),
        name="fourier_mix",
    )(*args)


def _post_norm(h, y, g, b):
    return _ln(ALPHA * h + y) * g + b


def _out_even_kernel(a_ref, f_ref, x_ref, ctx_ref, mod_ref, w_ref, g_ref, b_ref, o_ref, *, n_lat_tiles):
    na = a_ref.shape[-1]
    y = _dot(a_ref[0], w_ref[:na, :]) + _dot(f_ref[0].astype(BF16), w_ref[na:, :])
    h = _stream_tile(x_ref, ctx_ref, n_lat_tiles)
    o_ref[0] = _post_norm(h, mod_ref[0, 2:3, :] * y, g_ref[...], b_ref[...])


def _out_even(a_n, fm, x, ctx, mod, w_out, ln_g, ln_b):
    nb, n_lat, d = x.shape
    ntot = n_lat + ctx.shape[1]
    tm = TOKEN_TILE
    na = a_n.shape[-1]
    tok = lambda c: pl.BlockSpec((1, tm, c), lambda b, i: (b, i, 0))
    return pl.pallas_call(
        functools.partial(_out_even_kernel, n_lat_tiles=n_lat // tm),
        grid=(nb, ntot // tm),
        in_specs=[tok(na), tok(FN_COLS)] + _stream_specs(tm, d, n_lat // tm) + [
                  _mod_spec(nb, n_lat // tm, d), _full(w_out.shape), _full((1, d)), _full((1, d))],
        out_specs=tok(d),
        out_shape=jax.ShapeDtypeStruct((nb, ntot, d), F32),
        compiler_params=_params("parallel", "parallel"),
        name="out_even",
    )(a_n, fm, x, ctx, mod, w_out, ln_g, ln_b)


def _ffn_kernel(h_ref, mod_ref, w1_ref, w3_ref, w2_ref, g_ref, b_ref, o_ref):
    h = h_ref[0]
    u = (_ln(h) * (1.0 + mod_ref[0, 4:5, :]) + mod_ref[0, 3:4, :]).astype(BF16)
    hid = (_silu(_dot(u, w1_ref[...])) * _dot(u, w3_ref[...])).astype(BF16)
    y = _dot(hid, w2_ref[...])
    o_ref[0] = _post_norm(h, mod_ref[0, 5:6, :] * y, g_ref[...], b_ref[...])


def _ffn(h, n_rows, mod, w1, w3, w2, ln_g, ln_b, n_lat):
    nb, _, d = h.shape
    tm = TOKEN_TILE
    tok = pl.BlockSpec((1, tm, d), lambda b, i: (b, i, 0))
    resident = lambda s: pl.BlockSpec(s, lambda b, i: (0, 0), pipeline_mode=pl.Buffered(1))
    return pl.pallas_call(
        _ffn_kernel,
        grid=(nb, n_rows // tm),
        in_specs=[tok, _mod_spec(nb, n_lat // tm, d),
                  resident(w1.shape), resident(w3.shape), resident(w2.shape),
                  _full((1, d)), _full((1, d))],
        out_specs=tok,
        out_shape=jax.ShapeDtypeStruct((nb, n_rows, d), F32),
        compiler_params=_params("parallel", "parallel"),
        name="ffn",
    )(h, mod, w1, w3, w2, ln_g, ln_b)


def _proj_odd_kernel(h_ref, mod_ref, w_ref, xm_ref, z_ref):
    x = h_ref[0]
    u = (_ln(x) * (1.0 + mod_ref[0, 1:2, :]) + mod_ref[0, 0:1, :]).astype(BF16)
    p = _dot(u, w_ref[...])
    inner = xm_ref.shape[-1]
    xm_ref[0] = p[:, :inner]
    z_ref[0] = p[:, inner:]


def _proj_odd(hcat, mod, w, n_lat):
    nb, ntot, d = hcat.shape
    tm = TOKEN_TILE
    inner = w.shape[1] // 2
    out_spec = pl.BlockSpec((1, tm, inner), lambda b, i: (b, i, 0))
    out_shape = jax.ShapeDtypeStruct((nb, ntot, inner), F32)
    return pl.pallas_call(
        _proj_odd_kernel,
        grid=(nb, ntot // tm),
        in_specs=[pl.BlockSpec((1, tm, d), lambda b, i: (b, i, 0)),
                  _mod_spec(nb, n_lat // tm, d),
                  _full(w.shape)],
        out_specs=[out_spec, out_spec],
        out_shape=[out_shape, out_shape],
        compiler_params=_params("parallel", "parallel"),
        name="proj_odd",
    )(hcat, mod, w)


def _gate_fold_kernel(wg_ref, wq_ref, wk_ref, wv_ref, gc_ref, gm_ref):
    dh = wq_ref.shape[-1]
    wg = wg_ref[0]
    gc_ref[0] = (_dot_nt(wg[:, :dh], wq_ref[0]) + _dot_nt(wg[:, dh:2 * dh], wk_ref[0])).astype(BF16)
    gm_ref[0] = _dot_nt(wg[:, 2 * dh:], wv_ref[0]).astype(BF16)


def _gate_fold(wg, wq, wk, wv):
    nh, n_gates, _ = wg.shape
    dh = wq.shape[-1]
    wblk = pl.BlockSpec((1, dh, dh), lambda h: (h, 0, 0))
    gspec = pl.BlockSpec((1, n_gates, dh), lambda h: (h, 0, 0))
    gshape = jax.ShapeDtypeStruct((nh, n_gates, dh), BF16)
    return pl.pallas_call(
        _gate_fold_kernel,
        grid=(nh,),
        in_specs=[pl.BlockSpec((1, n_gates, 3 * dh), lambda h: (h, 0, 0)), wblk, wblk, wblk],
        out_specs=[gspec, gspec],
        out_shape=[gshape, gshape],
        compiler_params=_params("parallel"),
        name="mlstm_gate_fold",
    )(wg, wq, wk, wv)


def _mfeat_kernel(xm_ref, cw_ref, cb_ref, wk_ref, wqt_ref, wvt_ref,
                  xc_ref, k_ref, qt_ref, vt_ref, gr_ref, *, n_lat, chunk, row_group):
    ntot = xm_ref.shape[1]
    dh = wk_ref.shape[-1]
    ct = CONV_TILE
    half = ML_CONV_W // 2
    cw = cw_ref[...]
    cb = cb_ref[...]

    def conv_tile(t, carry):
        r0 = pl.multiple_of(t * ct, ct)
        at_start = jnp.logical_or(r0 == 0, r0 == n_lat)
        at_end = jnp.logical_or(r0 + ct == n_lat, r0 + ct == ntot)
        p0 = pl.multiple_of(jnp.maximum(r0 - SUBLANES, 0), SUBLANES)
        n0 = pl.multiple_of(jnp.minimum(r0 + ct, ntot - SUBLANES), SUBLANES)
        prev = xm_ref[0, pl.ds(p0, SUBLANES), :]
        nxt = xm_ref[0, pl.ds(n0, SUBLANES), :]
        prev = jnp.where(at_start, 0.0, prev)
        nxt = jnp.where(at_end, 0.0, nxt)
        xe = jnp.concatenate([prev, xm_ref[0, pl.ds(r0, ct), :], nxt], axis=0)
        acc = cb
        for dd in range(ML_CONV_W):
            lo = SUBLANES - half + dd
            acc = acc + cw[dd:dd + 1, :] * xe[lo:lo + ct, :]
        xc_ref[0, pl.ds(r0, ct), :] = _silu(acc)
        return carry

    lax.fori_loop(0, ntot // ct, conv_tile, 0)

    for g in range(ntot // row_group):
        rows = pl.ds(g * row_group, row_group)
        k_ref[0, rows, :] = _dot(xc_ref[0, rows, :].astype(BF16), wk_ref[0]).astype(BF16)

    def t_chunk(c, carry):
        r0 = pl.multiple_of(c * chunk, chunk)
        tq = _dot_nt(wqt_ref[0], xc_ref[0, pl.ds(r0, chunk), :].astype(BF16))
        tv = _dot_nt(wvt_ref[0], xm_ref[0, pl.ds(r0, chunk), :].astype(BF16))
        qt_ref[0, 0, c] = tq[:dh].astype(BF16)
        vt_ref[0, 0, c] = tv[:dh].astype(BF16)
        gr_ref[0, 0, c] = tq[dh:] + tv[dh:]
        return carry

    lax.fori_loop(0, ntot // chunk, t_chunk, 0, unroll=2)


def _mfeat(xm, conv_w, conv_b, wk, wqt_g, wvt_g, n_lat, chunk):
    nb, ntot, inner = xm.shape
    nh = ML_HEADS
    dh = inner // nh
    nct = ntot // chunk
    n_gates = wqt_g.shape[1] - dh
    ng = 2 if ntot % (2 * SUBLANES) == 0 else 1
    kern = functools.partial(_mfeat_kernel, n_lat=n_lat, chunk=chunk, row_group=ntot // ng)
    seq = pl.BlockSpec((1, ntot, dh), lambda b, h: (b, 0, h))
    wstack = pl.BlockSpec((1, dh + n_gates, dh), lambda b, h: (h, 0, 0))
    tspec = pl.BlockSpec((1, 1, nct, dh, chunk), lambda b, h: (b, h, 0, 0, 0))
    tshape = jax.ShapeDtypeStruct((nb, nh, nct, dh, chunk), BF16)
    return pl.pallas_call(
        kern,
        grid=(nb, nh),
        in_specs=[seq,
                  pl.BlockSpec((ML_CONV_W, dh), lambda b, h: (0, h)),
                  pl.BlockSpec((1, dh), lambda b, h: (0, h)),
                  pl.BlockSpec((1, dh, dh), lambda b, h: (h, 0, 0)), wstack, wstack],
        out_specs=[seq, seq, tspec, tspec,
                   pl.BlockSpec((1, 1, nct, n_gates, chunk), lambda b, h: (b, h, 0, 0, 0))],
        out_shape=[jax.ShapeDtypeStruct((nb, ntot, inner), F32),
                   jax.ShapeDtypeStruct((nb, ntot, inner), BF16), tshape, tshape,
                   jax.ShapeDtypeStruct((nb, nh, nct, n_gates, chunk), F32)],
        compiler_params=_params("parallel", "parallel"),
        name="mlstm_features",
    )(xm, conv_w, conv_b, wk, wqt_g, wvt_g)


def _split3(x):
    hi = x.astype(BF16)
    r1 = x - hi.astype(F32)
    mid = r1.astype(BF16)
    lo = (r1 - mid.astype(F32)).astype(BF16)
    return hi, mid, lo


def _running_max(x, reverse):
    n = x.shape[1]
    lane = lax.broadcasted_iota(jnp.int32, x.shape, 1)
    k = 1
    while k < n:
        if reverse:
            shifted = jnp.where(lane < n - k, pltpu.roll(x, n - k, 1), -jnp.inf)
        else:
            shifted = jnp.where(lane >= k, pltpu.roll(x, k, 1), -jnp.inf)
        x = jnp.maximum(x, shifted)
        k *= 2
    return x


def _gates_kernel(g_ref, b_ref, rows_ref, cols_ref):
    nh = ML_HEADS
    nct, chunk = g_ref.shape[2], g_ref.shape[-1]
    bias = b_ref[...]
    li_f, li_b, lf_f, lf_b = [], [], [], []
    for c in range(nct):
        pre = jnp.sum(g_ref[0, :, c], axis=0) + bias
        xf = pre[2 * nh:]
        log_f = jnp.minimum(xf, 0.0) - jnp.log1p(jnp.exp(-jnp.abs(xf)))
        li_f.append(pre[:nh] * LOG2_E)
        li_b.append(pre[nh:2 * nh] * LOG2_E)
        lf_f.append(log_f[:nh] * LOG2_E)
        lf_b.append(log_f[nh:] * LOG2_E)
    s_idx = lax.broadcasted_iota(jnp.int32, (chunk, chunk), 0)
    t_idx = lax.broadcasted_iota(jnp.int32, (chunk, chunk), 1)
    tri_f = jnp.where(s_idx <= t_idx, 1.0, 0.0).astype(BF16)
    tri_b = jnp.where(s_idx >= t_idx, 1.0, 0.0).astype(BF16)
    cum_f = sum(_dot(p, tri_f) for p in _split3(jnp.concatenate(lf_f, axis=0)))
    cum_b = sum(_dot(p, tri_b) for p in _split3(jnp.concatenate(lf_b, axis=0)))
    gap_f = jnp.concatenate(li_f, axis=0) - cum_f
    gap_b = jnp.concatenate(li_b, axis=0) - cum_b
    top_f = _running_max(gap_f, False)
    top_b = _running_max(gap_b, True)
    pad = jnp.zeros((LANES - 2 * nh, chunk), F32)
    for c in range(nct):
        r = slice(c * nh, (c + 1) * nh)
        rows_ref[0, c] = jnp.concatenate([gap_f[r], gap_b[r], top_f[r], top_b[r], cum_f[r], cum_b[r]], axis=0)
        gap_t = jnp.transpose(jnp.concatenate([gap_f[r], gap_b[r], pad], axis=0))
        cols_ref[0, c * chunk:(c + 1) * chunk, :] = gap_t[:, :2 * nh]


def _gates(g_part, bias, chunk):
    nb, nh, nct, n_gates, _ = g_part.shape
    ntot = nct * chunk
    return pl.pallas_call(
        _gates_kernel,
        grid=(nb,),
        in_specs=[pl.BlockSpec((1, nh, nct, n_gates, chunk), lambda b: (b, 0, 0, 0, 0)),
                  _full(bias.shape)],
        out_specs=[pl.BlockSpec((1, nct, 6 * nh, chunk), lambda b: (b, 0, 0, 0)),
                   pl.BlockSpec((1, ntot, 2 * nh), lambda b: (b, 0, 0))],
        out_shape=[jax.ShapeDtypeStruct((nb, nct, 6 * nh, chunk), F32),
                   jax.ShapeDtypeStruct((nb, ntot, 2 * nh), F32)],
        compiler_params=_params("parallel"),
        name="mlstm_gates",
    )(g_part, bias)


STATE_PAD_ROWS = 16


def _scan_kernel(k_ref, qt_ref, vt_ref, rows_ref, cols_ref, hs_ref, c_sc, n_sc, m_sc,
                 *, n_lat_chunks, n_ctx_chunks, chunk):
    nh = ML_HEADS
    h = pl.program_id(1)
    dh = c_sc.shape[-1]
    c_sc[...] = jnp.zeros(c_sc.shape, F32)
    n_sc[...] = jnp.zeros(n_sc.shape, F32)
    m_sc[...] = jnp.zeros(m_sc.shape, F32)
    s_idx = lax.broadcasted_iota(jnp.int32, (chunk, chunk), 0)
    t_idx = lax.broadcasted_iota(jnp.int32, (chunk, chunk), 1)
    gate_lane = lax.broadcasted_iota(jnp.int32, (1, 2 * nh), 1)

    def step(d, cidx, with_out):
        off = pl.multiple_of(cidx * chunk, chunk)
        kc = k_ref[0, pl.ds(off, chunk), :]
        qt = qt_ref[0, 0, cidx]
        vt = vt_ref[0, 0, cidx]
        gap_r = rows_ref[0, cidx, pl.ds(d * nh + h, 1), :]
        top_r = rows_ref[0, cidx, pl.ds((2 + d) * nh + h, 1), :]
        cum_r = rows_ref[0, cidx, pl.ds((4 + d) * nh + h, 1), :]
        m = m_sc[d]
        ct = c_sc[d]
        nrow = n_sc[d]
        last = chunk - 1 if d == 0 else 0
        b_end = cum_r[:, last:last + 1]
        top_end = top_r[:, last:last + 1]
        if with_out is not None:
            ctile = cols_ref[0, pl.ds(off, chunk), :]
            gap_c = jnp.sum(jnp.where(gate_lane == d * nh + h, ctile, 0.0), axis=1, keepdims=True)
            lift = jnp.maximum(top_r, m)
            ordered = (s_idx <= t_idx) if d == 0 else (s_idx >= t_idx)
            dw_t = jnp.exp2(jnp.where(ordered, gap_c - lift, -jnp.inf))
            iw = jnp.exp2(m - lift)
            nb16 = jnp.broadcast_to(nrow.astype(BF16), (STATE_PAD_ROWS, dh))
            a = _dot(jnp.concatenate([kc, ct.astype(BF16), nb16], axis=0), qt)
            s_t = a[:chunk] * dw_t
            num_t = iw * a[chunk:chunk + dh] + _dot(vt, s_t.astype(BF16))
            den = iw * a[chunk + dh:chunk + dh + 1] + jnp.sum(s_t, axis=0, keepdims=True)
            floor = jnp.exp2(-(cum_r + lift))
            h_t = num_t * (1.0 / jnp.maximum(jnp.abs(den), floor))
            hs_ref[0, 0, cidx] = h_t if with_out == "assign" else hs_ref[0, 0, cidx] + h_t
        mx = jnp.maximum(m, top_end)
        keep = jnp.exp2(m - mx)
        w_r = jnp.exp2(gap_r - mx)
        vw = (vt.astype(F32) * w_r).astype(BF16)
        wb16 = jnp.broadcast_to(w_r.astype(BF16), (STATE_PAD_ROWS, chunk))
        upd = _dot(jnp.concatenate([vw, wb16], axis=0), kc)
        c_sc[d] = keep * ct + upd[:dh]
        n_sc[d] = keep * nrow + upd[dh:dh + 1]
        m_sc[d] = b_end + mx

    def ctx_body(c, carry):
        step(0, n_lat_chunks + c, None)
        step(1, n_lat_chunks + n_ctx_chunks - 1 - c, None)
        return carry

    def lat_body(mode, c, carry):
        step(0, c, mode)
        step(1, n_lat_chunks - 1 - c, mode)
        return carry

    lax.fori_loop(0, n_ctx_chunks, ctx_body, 0)
    half = n_lat_chunks // 2
    lax.fori_loop(0, half, functools.partial(lat_body, "assign"), 0, unroll=4)
    lax.fori_loop(half, n_lat_chunks, functools.partial(lat_body, "add"), 0, unroll=4)


def _scan(k, qt, vt, rows, cols, n_lat, chunk):
    nb, ntot, inner = k.shape
    nh = ML_HEADS
    dh = inner // nh
    nct = ntot // chunk
    ncl = n_lat // chunk
    assert ncl % 2 == 0
    kern = functools.partial(_scan_kernel, n_lat_chunks=ncl, n_ctx_chunks=nct - ncl, chunk=chunk)
    tspec = pl.BlockSpec((1, 1, nct, dh, chunk), lambda b, h: (b, h, 0, 0, 0))
    return pl.pallas_call(
        kern,
        grid=(nb, nh),
        in_specs=[pl.BlockSpec((1, ntot, dh), lambda b, h: (b, 0, h)), tspec, tspec,
                  pl.BlockSpec((1, nct, rows.shape[2], chunk), lambda b, h: (b, 0, 0, 0)),
                  pl.BlockSpec((1, ntot, cols.shape[2]), lambda b, h: (b, 0, 0))],
        out_specs=pl.BlockSpec((1, 1, ncl, dh, chunk), lambda b, h: (b, h, 0, 0, 0)),
        out_shape=jax.ShapeDtypeStruct((nb, nh, ncl, dh, chunk), F32),
        scratch_shapes=[pltpu.VMEM((2, dh, dh), F32), pltpu.VMEM((2, 1, dh), F32),
                        pltpu.VMEM((2, 1, 1), F32)],
        compiler_params=_params("parallel", "parallel"),
        name="mlstm_scan",
    )(k, qt, vt, rows, cols)


def _out_odd_kernel(hs_ref, xc_ref, z_ref, h_ref, mod_ref, skip_ref, hg_ref, w_ref, g_ref, b_ref, o_ref):
    parts = []
    for hd in range(ML_HEADS):
        ht = hs_ref[0, hd, 0]
        mu = jnp.mean(ht, axis=0, keepdims=True)
        hc = ht - mu
        var = jnp.mean(hc * hc, axis=0, keepdims=True)
        parts.append(jnp.transpose(hc * lax.rsqrt(var + LN_EPS)))
    hn = jnp.concatenate(parts, axis=1)
    y = (hn * hg_ref[...] + skip_ref[...] * xc_ref[0]) * _silu(z_ref[0])
    yo = _dot(y.astype(BF16), w_ref[...])
    o_ref[0] = _post_norm(h_ref[0], mod_ref[0, 2:3, :] * yo, g_ref[...], b_ref[...])


def _out_odd(hs, xc, z, hcat, mod, skip, head_g, w_out, ln_g, ln_b, n_lat):
    nb, _, d = hcat.shape
    inner = xc.shape[-1]
    _, nh, _, dh, tm = hs.shape
    tok = lambda c: pl.BlockSpec((1, tm, c), lambda b, i: (b, i, 0))
    return pl.pallas_call(
        _out_odd_kernel,
        grid=(nb, n_lat // tm),
        in_specs=[pl.BlockSpec((1, nh, 1, dh, tm), lambda b, i: (b, 0, i, 0, 0)),
                  tok(inner), tok(inner), tok(d),
                  pl.BlockSpec((1, 6, d), lambda b, i: (b, 0, 0)),
                  _full((1, inner)), _full((1, inner)), _full(w_out.shape),
                  _full((1, d)), _full((1, d))],
        out_specs=tok(d),
        out_shape=jax.ShapeDtypeStruct((nb, n_lat, d), F32),
        compiler_params=_params("parallel", "parallel"),
        name="out_odd",
    )(hs, xc, z, hcat, mod, skip, head_g, w_out, ln_g, ln_b)


def _even_in_columns():
    cols, scale = [], []
    quarter = DA_DK // 2
    for hd in range(DA_HEADS):
        base = hd * DA_HEAD_COLS
        for blk in range(2):
            b0 = base + blk * 2 * DA_DK
            for half in range(2):
                for m in range(2):
                    cols += [b0 + m * DA_DK + half * quarter + j for j in range(quarter)]
            scale += [DA_DK ** -0.5 * math.log2(math.e) if blk == 0 else 1.0] * (2 * DA_DK)
        cols += list(range(base + 4 * DA_DK, base + DA_HEAD_COLS))
        scale += [1.0] * DA_DV
    cols += list(range(DA_HEADS * DA_HEAD_COLS, DA_HEADS * DA_HEAD_COLS + FN_COLS))
    scale += [1.0] * FN_COLS
    return np.asarray(cols, np.int32), np.asarray(scale, np.float32)


def _rope_tables(n_lat, ntot):
    rows = n_lat // GRID_W
    row = jnp.repeat(jnp.arange(rows, dtype=F32), GRID_W)
    col = jnp.tile(jnp.arange(GRID_W, dtype=F32), rows)
    n_freq = DA_DK // 4
    inv_freq = ROPE_BASE ** (-jnp.arange(n_freq, dtype=F32) / n_freq)
    ang = jnp.concatenate([row[:, None] * inv_freq, col[:, None] * inv_freq], -1)
    cos, sin = jnp.cos(ang), jnp.sin(ang)
    c = jnp.concatenate([cos, cos, cos, cos], axis=1)
    s = jnp.concatenate([-sin, -sin, sin, sin], axis=1)
    n_ctx = ntot - n_lat
    c = jnp.concatenate([c, jnp.ones((n_ctx, LANES), F32)], axis=0)
    s = jnp.concatenate([s, jnp.zeros((n_ctx, LANES), F32)], axis=0)
    return c, s


def _block_diag_heads(w, dh):
    nblk = w.shape[0]
    per_head = dh // ML_BLOCK
    wh = w.reshape(nblk // per_head, per_head, ML_BLOCK, ML_BLOCK)
    eye = jnp.eye(per_head, dtype=w.dtype)
    dense = jnp.einsum('hgij,gk->hgikj', wh, eye)
    return dense.reshape(nblk // per_head, dh, dh)


def kernel(x, c, ctx, c_ctx, w_mod, b_mod, ln_g, ln_b, w_ff1, w_ff3, w_ff2, a_w_in, a_w_out, da_lq1, da_lk1, da_lq2, da_lk2, da_head_g, m_w_in, m_w_out, m_conv_w, m_conv_b, m_wq, m_wk, m_wv, m_w_ig, m_b_ig, m_w_fg, m_b_fg, m_skip, m_head_g):
    nb, n_lat, d = x.shape
    n_ctx = ctx.shape[1]
    ntot = n_lat + n_ctx
    assert w_mod.shape[0] == DEPTH == 2
    assert n_lat % TOKEN_TILE == 0 and n_ctx % TOKEN_TILE == 0 and n_lat % n_ctx == 0 and n_lat % GRID_W == 0

    r = -(-(nb + 1) // MOD_ROWS_PAD) * MOD_ROWS_PAD
    cvec = jnp.concatenate([c, c_ctx[None, :], jnp.zeros((r - nb - 1, d), F32)], axis=0)
    mod_all = _modulation(cvec, w_mod, b_mod).reshape(DEPTH, r, 6, d)

    lam_init0 = 0.8 - 0.6 * math.exp(-0.3 * 0)
    cols, colscale = _even_in_columns()
    w_in = (a_w_in[0][:, cols] * colscale[None, :]).astype(BF16)
    rope_c, rope_s = _rope_tables(n_lat, ntot)
    q, k, v, f = _proj_even(x, ctx, mod_all[0], w_in, rope_c, rope_s)
    lam_params = jnp.stack([da_lq1[0], da_lk1[0], da_lq2[0], da_lk2[0]], axis=0)
    a_n = _attention(q, k, v, lam_params, da_head_g[0][None, :], n_lat, lam_init0)
    fm = _fourier_mix(f, 0, n_lat, None)
    fm = _fourier_mix(f, n_lat, n_ctx, fm)
    h1 = _out_even(a_n, fm, x, ctx, mod_all[0], a_w_out[0].astype(BF16),
                   ln_g[0, 0][None, :], ln_b[0, 0][None, :])
    hcat = _ffn(h1, ntot, mod_all[0], w_ff1[0].astype(BF16), w_ff3[0].astype(BF16), w_ff2[0].astype(BF16),
                ln_g[0, 1][None, :], ln_b[0, 1][None, :], n_lat)

    inner = m_w_in.shape[2] // 2
    dh = inner // ML_HEADS
    chunk = MXU_DIM if (n_lat % MXU_DIM == 0 and n_ctx % MXU_DIM == 0) else LANES
    xm, z = _proj_odd(hcat, mod_all[1], m_w_in[0].astype(BF16), n_lat)
    wq_f = _block_diag_heads(m_wq[0], dh)
    wk_f = _block_diag_heads(m_wk[0], dh) * (dh ** -0.5)
    wv_f = _block_diag_heads(m_wv[0], dh)
    wq, wk, wv = wq_f.astype(BF16), wk_f.astype(BF16), wv_f.astype(BF16)
    wg_all = jnp.concatenate([m_w_ig[0, 0], m_w_ig[0, 1], m_w_fg[0, 0], m_w_fg[0, 1]], axis=1)
    wg_all = wg_all.reshape(3, ML_HEADS, dh, 4 * ML_HEADS) * jnp.asarray([1.0, dh ** 0.5, 1.0], F32)[:, None, None, None]
    wg = jnp.transpose(wg_all, (1, 3, 0, 2)).reshape(ML_HEADS, 4 * ML_HEADS, 3 * dh).astype(BF16)
    g_bias = jnp.concatenate([m_b_ig[0, 0], m_b_ig[0, 1], m_b_fg[0, 0], m_b_fg[0, 1]])[:, None]
    gc, gm = _gate_fold(wg, wq, wk, wv)
    wqt_g = jnp.concatenate([jnp.swapaxes(wq_f, 1, 2).astype(BF16), gc], axis=1)
    wvt_g = jnp.concatenate([jnp.swapaxes(wv_f, 1, 2).astype(BF16), gm], axis=1)
    xc, km, qt, vt, g_part = _mfeat(xm, m_conv_w[0], m_conv_b[0][None, :], wk, wqt_g, wvt_g, n_lat, chunk)
    rows, colsg = _gates(g_part, g_bias, chunk)
    hs = _scan(km, qt, vt, rows, colsg, n_lat, chunk)
    h1 = _out_odd(hs, xc, z, hcat, mod_all[1], m_skip[0][None, :], m_head_g[0][None, :],
                  m_w_out[0].astype(BF16), ln_g[1, 0][None, :], ln_b[1, 0][None, :], n_lat)
    return _ffn(h1, n_lat, mod_all[1], w_ff1[1].astype(BF16), w_ff3[1].astype(BF16), w_ff2[1].astype(BF16),
                ln_g[1, 1][None, :], ln_b[1, 1][None, :], n_lat)
```

```python
import functools
import math

import numpy as np
import jax
import jax.numpy as jnp
from jax import lax
from jax.experimental import pallas as pl
from jax.experimental.pallas import tpu as pltpu

F32 = jnp.float32
BF16 = jnp.bfloat16

DA_HEADS = 6
DA_DK = 64
DA_DV = 2 * DA_DK
DA_HEAD_COLS = 4 * DA_DK + DA_DV
FN_GROUPS = 4
FN_CH = 64
FN_COLS = FN_GROUPS * FN_CH
ML_HEADS = 8
ML_BLOCK = 4
ML_CONV_W = 5
GRID_W = 64
ROPE_BASE = 10000.0
LN_EPS = 1e-5
LOG2_E = math.log2(math.e)
DEPTH = 2
ALPHA = (2 * DEPTH) ** 0.25

LANES = 128
SUBLANES = 8
MXU_DIM = 256
VMEM_LIMIT_BYTES = 60 * 1024 * 1024

TOKEN_TILE = 256
CONV_TILE = 128
MOD_ROWS_PAD = 8
ATTN_Q_TILE = 512
FFN_ROW_GROUPS = 2
SWEEP_UNROLL = 8


def _params(*sem):
    return pltpu.CompilerParams(dimension_semantics=sem, vmem_limit_bytes=VMEM_LIMIT_BYTES)


def _ln(x):
    mu = jnp.mean(x, axis=-1, keepdims=True)
    xc = x - mu
    var = jnp.mean(xc * xc, axis=-1, keepdims=True)
    return xc * lax.rsqrt(var + LN_EPS)


def _silu(x):
    return x * jax.nn.sigmoid(x)


def _dot(a, b):
    return jnp.dot(a, b, preferred_element_type=F32)


def _dot_nt(a, b):
    return lax.dot_general(a, b, (((1,), (1,)), ((), ())), preferred_element_type=F32)


def _full(shape):
    n = len(shape)
    return pl.BlockSpec(shape, lambda *_: (0,) * n)


def _mod_kernel(c_ref, w_ref, b_ref, o_ref):
    s = _silu(c_ref[...]).astype(BF16)
    o_ref[0] = _dot(s, w_ref[0].astype(BF16)) + b_ref[0]


def _modulation(cvec, w_mod, b_mod):
    depth, d, d6 = w_mod.shape
    r = cvec.shape[0]
    tn = d6 // 4
    return pl.pallas_call(
        _mod_kernel,
        grid=(depth, d6 // tn),
        in_specs=[pl.BlockSpec((r, d), lambda l, j: (0, 0)),
                  pl.BlockSpec((1, d, tn), lambda l, j: (l, 0, j)),
                  pl.BlockSpec((1, 1, tn), lambda l, j: (l, 0, j))],
        out_specs=pl.BlockSpec((1, r, tn), lambda l, j: (l, 0, j)),
        out_shape=jax.ShapeDtypeStruct((depth, r, d6), F32),
        compiler_params=_params("parallel", "parallel"),
        name="modulation",
    )(cvec, w_mod, b_mod.reshape(depth, 1, d6))


def _mod_spec(nb, n_lat_tiles, d):
    return pl.BlockSpec((1, 6, d), lambda b, i: (jnp.where(i < n_lat_tiles, b, nb), 0, 0))


def _stream_tile(x_ref, ctx_ref, n_lat_tiles):
    return jnp.where(pl.program_id(1) < n_lat_tiles, x_ref[0], ctx_ref[0])


def _stream_specs(tm, d, n_lat_tiles):
    return [pl.BlockSpec((1, tm, d), lambda b, i: (b, jnp.minimum(i, n_lat_tiles - 1), 0)),
            pl.BlockSpec((1, tm, d), lambda b, i: (b, jnp.maximum(i - n_lat_tiles, 0), 0))]


def _proj_even_kernel(x_ref, ctx_ref, mod_ref, w_ref, c_ref, s_ref, q_ref, k_ref, v_ref, f_ref, *, n_lat_tiles):
    x = _stream_tile(x_ref, ctx_ref, n_lat_tiles)
    u = (_ln(x) * (1.0 + mod_ref[0, 1:2, :]) + mod_ref[0, 0:1, :]).astype(BF16)
    p = _dot(u, w_ref[...])
    c = c_ref[...]
    s = s_ref[...]
    blk = DA_HEAD_COLS
    for hd in range(DA_HEADS):
        pq = p[:, hd * blk: hd * blk + LANES]
        pk = p[:, hd * blk + LANES: hd * blk + 2 * LANES]
        q_ref[0, hd] = (pq * c + pltpu.roll(pq, LANES // 2, 1) * s).astype(BF16)
        k_ref[0, hd] = (pk * c + pltpu.roll(pk, LANES // 2, 1) * s).astype(BF16)
        v_ref[0, hd] = p[:, hd * blk + 2 * LANES: (hd + 1) * blk].astype(BF16)
    f_ref[0] = p[:, DA_HEADS * blk:]


def _proj_even(x, ctx, mod, w, rope_c, rope_s):
    nb, n_lat, d = x.shape
    ntot = n_lat + ctx.shape[1]
    tm = TOKEN_TILE
    nt = ntot // tm
    ncols = w.shape[1]
    qkv_shape = jax.ShapeDtypeStruct((nb, DA_HEADS, ntot, LANES), BF16)
    qkv_spec = pl.BlockSpec((1, DA_HEADS, tm, LANES), lambda b, i: (b, 0, i, 0))
    return pl.pallas_call(
        functools.partial(_proj_even_kernel, n_lat_tiles=n_lat // tm),
        grid=(nb, nt),
        in_specs=_stream_specs(tm, d, n_lat // tm) + [
                  _mod_spec(nb, n_lat // tm, d),
                  _full((d, ncols)),
                  pl.BlockSpec((tm, LANES), lambda b, i: (i, 0)),
                  pl.BlockSpec((tm, LANES), lambda b, i: (i, 0))],
        out_specs=[qkv_spec, qkv_spec, qkv_spec,
                   pl.BlockSpec((1, tm, FN_COLS), lambda b, i: (b, i, 0))],
        out_shape=[qkv_shape, qkv_shape, qkv_shape,
                   jax.ShapeDtypeStruct((nb, ntot, FN_COLS), F32)],
        compiler_params=_params("parallel", "parallel"),
        name="proj_even",
    )(x, ctx, mod, w, rope_c, rope_s)


def _stack_maps(q):
    qf = q.astype(F32)
    lane = lax.broadcasted_iota(jnp.int32, (1, LANES), 1)
    map1 = (lane % (LANES // 2)) < (LANES // 4)
    return jnp.concatenate([jnp.where(map1, qf, 0.0), jnp.where(map1, 0.0, qf)], axis=0).astype(BF16)


def _lane_block_max(s):
    blk = s[:, :LANES]
    for c in range(1, s.shape[1] // LANES):
        blk = jnp.maximum(blk, s[:, c * LANES:(c + 1) * LANES])
    return blk


def _exp_blocks(s, mb):
    return [jnp.exp2(s[:, c * LANES:(c + 1) * LANES] - mb) for c in range(s.shape[1] // LANES)]


def _diff_merge(o, lp, head_g, lam_init):
    tq = o.shape[0] // 2
    lam = (jnp.exp(jnp.sum(lp[0:1] * lp[1:2], keepdims=True))
           - jnp.exp(jnp.sum(lp[2:3] * lp[3:4], keepdims=True)) + lam_init)
    a = o[:tq] - lam * o[tq:]
    a_n = a * lax.rsqrt(jnp.mean(a * a, axis=-1, keepdims=True) + LN_EPS) * (head_g * (1.0 - lam_init))
    return a_n.astype(BF16)


def _attn_kernel(lp_ref, g_ref, qc_ref, qn_ref, kc_ref, kn_ref, v_ref, o_ref,
                 sa_sc, sb_sc, mcur_sc, mnxt_sc, l_sc, acc_sc, *, tk, nkt, lam_init):
    step = (pl.program_id(0) * pl.num_programs(1) + pl.program_id(1)) * pl.num_programs(2) + pl.program_id(2)
    rows = mcur_sc.shape[0]

    def chunk_off(j):
        return j * tk if isinstance(j, int) else pl.multiple_of(j * tk, tk)

    def score_chunk(qq, k_ref, s_ref, j):
        off = chunk_off(j)
        s = _dot_nt(qq, k_ref[0, 0, pl.ds(off, tk), :])
        s_ref[j] = s
        mnxt_sc[...] = jnp.maximum(mnxt_sc[...], _lane_block_max(s))

    def row_max():
        return jnp.broadcast_to(jnp.max(mnxt_sc[...], axis=1, keepdims=True), (rows, LANES))

    @pl.when(step == 0)
    def _():
        qq0 = _stack_maps(qc_ref[0, 0])
        mnxt_sc[...] = jnp.full(mnxt_sc.shape, -jnp.inf, F32)

        def first(j, carry):
            score_chunk(qq0, kc_ref, sa_sc, j)
            return carry

        lax.fori_loop(0, nkt, first, 0)
        mcur_sc[...] = row_max()

    qqn = _stack_maps(qn_ref[0, 0])
    mnxt_sc[...] = jnp.full(mnxt_sc.shape, -jnp.inf, F32)
    l_sc[...] = jnp.zeros(l_sc.shape, F32)
    acc_sc[...] = jnp.zeros(acc_sc.shape, F32)

    def sweep(cur_ref, nxt_ref):
        def body(j, carry):
            score_chunk(qqn, kn_ref, nxt_ref, j)
            off = chunk_off(j)
            ps = _exp_blocks(cur_ref[j], mcur_sc[...])
            part = ps[0]
            for pc in ps[1:]:
                part = part + pc
            l_sc[...] += part
            acc_sc[...] += _dot(jnp.concatenate(ps, axis=1).astype(BF16), v_ref[0, 0, pl.ds(off, tk), :])
            return carry

        tail = min(nkt, SWEEP_UNROLL + nkt % SWEEP_UNROLL)
        lax.fori_loop(0, nkt - tail, body, 0, unroll=SWEEP_UNROLL)
        for j in range(nkt - tail, nkt):
            body(j, 0)

    @pl.when(step % 2 == 0)
    def _():
        sweep(sa_sc, sb_sc)

    @pl.when(step % 2 == 1)
    def _():
        sweep(sb_sc, sa_sc)

    o = acc_sc[...] / jnp.sum(l_sc[...], axis=1, keepdims=True)
    o_ref[0] = _diff_merge(o, lp_ref[...], g_ref[...], lam_init)
    mcur_sc[...] = row_max()


def _attn_ctx_kernel(lp_ref, g_ref, q_ref, k_ref, v_ref, buf_ref, o_ref, *, lam_init):
    del buf_ref
    s = _dot_nt(_stack_maps(q_ref[0, 0]), k_ref[0, 0])
    p = jnp.exp2(s - jnp.max(s, axis=1, keepdims=True))
    o = _dot(p.astype(BF16), v_ref[0, 0]) / jnp.sum(p, axis=1, keepdims=True)
    o_ref[0] = _diff_merge(o, lp_ref[...], g_ref[...], lam_init)


def _attention(q, k, v, lam_params, head_g, n_lat, lam_init):
    nb, nh, ntot, _ = q.shape
    n_ctx = ntot - n_lat
    tq = min(ATTN_Q_TILE, n_lat)
    tk = TOKEN_TILE
    nq = n_lat // tq
    nkt = ntot // tk
    rows = 2 * tq
    kv_spec = pl.BlockSpec((1, 1, ntot, LANES), lambda b, h, i: (b, h, 0, 0))

    def following(b, h, i):
        s = jnp.minimum((b * nh + h) * nq + i + 1, nb * nh * nq - 1)
        return s // (nh * nq), (s // nq) % nh, s % nq

    def qn_map(b, h, i):
        b2, h2, i2 = following(b, h, i)
        return b2, h2, i2, 0

    def kn_map(b, h, i):
        b2, h2, _ = following(b, h, i)
        return b2, h2, 0, 0

    a_lat = pl.pallas_call(
        functools.partial(_attn_kernel, tk=tk, nkt=nkt, lam_init=lam_init),
        grid=(nb, nh, nq),
        in_specs=[_full(lam_params.shape), _full(head_g.shape),
                  pl.BlockSpec((1, 1, tq, LANES), lambda b, h, i: (b, h, i, 0)),
                  pl.BlockSpec((1, 1, tq, LANES), qn_map),
                  kv_spec, pl.BlockSpec((1, 1, ntot, LANES), kn_map), kv_spec],
        out_specs=pl.BlockSpec((1, tq, LANES), lambda b, h, i: (b, i, h)),
        out_shape=jax.ShapeDtypeStruct((nb, ntot, nh * LANES), BF16),
        scratch_shapes=[pltpu.VMEM((nkt, rows, tk), F32)] * 2 + [pltpu.VMEM((rows, LANES), F32)] * 4,
        compiler_params=_params("arbitrary", "arbitrary", "arbitrary"),
        name="diff_attention",
    )(lam_params, head_g, q, q, k, k, v)
    cblk = n_lat // n_ctx
    ctx_spec = pl.BlockSpec((1, 1, n_ctx, LANES), lambda b, h: (b, h, cblk, 0))
    return pl.pallas_call(
        functools.partial(_attn_ctx_kernel, lam_init=lam_init),
        grid=(nb, nh),
        in_specs=[_full(lam_params.shape), _full(head_g.shape), ctx_spec, ctx_spec, ctx_spec,
                  pl.BlockSpec(memory_space=pl.ANY)],
        out_specs=pl.BlockSpec((1, n_ctx, LANES), lambda b, h: (b, cblk, h)),
        out_shape=jax.ShapeDtypeStruct(a_lat.shape, a_lat.dtype),
        input_output_aliases={5: 0},
        compiler_params=_params("parallel", "parallel"),
        name="diff_attention_ctx",
    )(lam_params, head_g, q, k, v, a_lat)


def _fourier_factor(n):
    a = 1 << (int(math.log2(n)) // 2)
    assert a * (n // a) == n and a % SUBLANES == 0 and (n // a) % SUBLANES == 0
    return a, n // a


def _fourier_tables(n):
    a, bn = _fourier_factor(n)
    j = np.arange(FN_CH)
    ang = 2.0 * np.pi * np.outer(j, j) / FN_CH
    eye = np.eye(FN_GROUPS)
    wc = np.concatenate([np.kron(eye, np.cos(ang)), -np.kron(eye, np.sin(ang))], axis=1)
    ia = np.arange(a)
    ang_a = 2.0 * np.pi * np.outer(ia, ia) / a
    ca, sa = np.cos(ang_a), np.sin(ang_a)
    m1 = np.block([[ca, sa], [-sa, ca]])
    ib = np.arange(bn)
    ang_t = 2.0 * np.pi * np.outer(ib, ia) / n
    ang_b = 2.0 * np.pi * np.outer(ib, ib) / bn
    m3 = np.concatenate([np.cos(ang_b), np.sin(ang_b)], axis=1) / math.sqrt(n * FN_CH)
    tw_c = jnp.broadcast_to(jnp.asarray(np.cos(ang_t), F32)[:, :, None], (bn, a, LANES))
    tw_s = jnp.broadcast_to(jnp.asarray(np.sin(ang_t), F32)[:, :, None], (bn, a, LANES))
    return (jnp.asarray(wc, BF16), jnp.asarray(m1, BF16), tw_c, tw_s, jnp.asarray(m3, BF16))


FOURIER_ROW_TILE = 512


def _fourier_kernel(x_ref, wc_ref, m1_ref, twc_ref, tws_ref, m3_ref, *rest, a, bn):
    o_ref, g_sc, t_sc, y_sc = rest[-4:]
    n = a * bn
    nhalf = FN_COLS // LANES
    rt = min(FOURIER_ROW_TILE, n)

    def halves(v):
        return [v[:, h * LANES:(h + 1) * LANES] for h in range(v.shape[1] // LANES)]

    for t in range(n // rt):
        tile = slice(t * rt, (t + 1) * rt)
        g = _dot(x_ref[0, tile, :].astype(BF16), wc_ref[...])
        for i, part in enumerate(halves(g)):
            g_sc[i, tile, :] = part

    def gather(buf, rows):
        parts = [jnp.concatenate([buf[p * nhalf + h, rows, :] for h in range(nhalf)], axis=1) for p in range(2)]
        return jnp.concatenate(parts, axis=0).astype(BF16)

    def stage1(b, carry):
        t1 = _dot(m1_ref[...], gather(g_sc, pl.ds(b, a, stride=bn)))
        tr, ti = t1[:a], t1[a:]
        c = jnp.concatenate([twc_ref[b]] * nhalf, axis=1)
        s = jnp.concatenate([tws_ref[b]] * nhalf, axis=1)
        dst = pl.ds(pl.multiple_of(b * a, a), a)
        for i, part in enumerate(halves(tr * c + ti * s) + halves(ti * c - tr * s)):
            t_sc[i, dst, :] = part
        return carry

    lax.fori_loop(0, bn, stage1, 0, unroll=4)

    def stage2(ka, carry):
        rows = pl.ds(ka, bn, stride=a)
        y = _dot(m3_ref[...], gather(t_sc, rows))
        for i, part in enumerate(halves(y)):
            y_sc[i, rows, :] = part
        return carry

    lax.fori_loop(0, a, stage2, 0, unroll=4)
    for t in range(n // rt):
        tile = slice(t * rt, (t + 1) * rt)
        o_ref[0, tile, :] = jnp.concatenate([y_sc[h, tile, :] for h in range(nhalf)], axis=1)


def _fourier_mix(fcat, row0, n, out_buf):
    nb, ntot, _ = fcat.shape
    a, bn = _fourier_factor(n)
    tables = _fourier_tables(n)
    assert row0 % n == 0
    seq = pl.BlockSpec((1, n, FN_COLS), lambda b: (b, row0 // n, 0))
    in_specs = [seq] + [_full(t.shape) for t in tables]
    args = [fcat, *tables]
    aliases = {}
    if out_buf is not None:
        in_specs.append(pl.BlockSpec(memory_space=pl.ANY))
        args.append(out_buf)
        aliases = {len(args) - 1: 0}
    return pl.pallas_call(
        functools.partial(_fourier_kernel, a=a, bn=bn),
        grid=(nb,),
        in_specs=in_specs,
        out_specs=seq,
        out_shape=jax.ShapeDtypeStruct((nb, ntot, FN_COLS), F32),
        scratch_shapes=[pltpu.VMEM((2 * (FN_COLS // LANES), n, LANES), F32)] * 2
                       + [pltpu.VMEM((FN_COLS // LANES, n, LANES), F32)],
        input_output_aliases=aliases,
        compiler_params=_params("parallel"),
        name="fourier_mix",
    )(*args)


def _post_norm(h, y, g, b):
    return _ln(ALPHA * h + y) * g + b


def _out_even_kernel(a_ref, f_ref, x_ref, ctx_ref, mod_ref, w_ref, g_ref, b_ref, o_ref, *, n_lat_tiles):
    na = a_ref.shape[-1]
    y = _dot(a_ref[0], w_ref[:na, :]) + _dot(f_ref[0].astype(BF16), w_ref[na:, :])
    h = _stream_tile(x_ref, ctx_ref, n_lat_tiles)
    o_ref[0] = _post_norm(h, mod_ref[0, 2:3, :] * y, g_ref[...], b_ref[...])


def _out_even(a_n, fm, x, ctx, mod, w_out, ln_g, ln_b):
    nb, n_lat, d = x.shape
    ntot = n_lat + ctx.shape[1]
    tm = TOKEN_TILE
    na = a_n.shape[-1]
    tok = lambda c: pl.BlockSpec((1, tm, c), lambda b, i: (b, i, 0))
    return pl.pallas_call(
        functools.partial(_out_even_kernel, n_lat_tiles=n_lat // tm),
        grid=(nb, ntot // tm),
        in_specs=[tok(na), tok(FN_COLS)] + _stream_specs(tm, d, n_lat // tm) + [
                  _mod_spec(nb, n_lat // tm, d), _full(w_out.shape), _full((1, d)), _full((1, d))],
        out_specs=tok(d),
        out_shape=jax.ShapeDtypeStruct((nb, ntot, d), F32),
        compiler_params=_params("parallel", "parallel"),
        name="out_even",
    )(a_n, fm, x, ctx, mod, w_out, ln_g, ln_b)


def _ffn_kernel(h_ref, mod_ref, w1_ref, w3_ref, w2_ref, g_ref, b_ref, o_ref):
    rg = h_ref.shape[1] // FFN_ROW_GROUPS
    for r in range(FFN_ROW_GROUPS):
        rows = slice(r * rg, (r + 1) * rg)
        h = h_ref[0, rows, :]
        u = (_ln(h) * (1.0 + mod_ref[0, 4:5, :]) + mod_ref[0, 3:4, :]).astype(BF16)
        hid = (_silu(_dot(u, w1_ref[...])) * _dot(u, w3_ref[...])).astype(BF16)
        y = _dot(hid, w2_ref[...])
        o_ref[0, rows, :] = _post_norm(h, mod_ref[0, 5:6, :] * y, g_ref[...], b_ref[...])


def _ffn(h, n_rows, mod, w1, w3, w2, ln_g, ln_b, n_lat):
    nb, _, d = h.shape
    tm = TOKEN_TILE
    tok = pl.BlockSpec((1, tm, d), lambda b, i: (b, i, 0))
    resident = lambda s: pl.BlockSpec(s, lambda b, i: (0, 0), pipeline_mode=pl.Buffered(1))
    return pl.pallas_call(
        _ffn_kernel,
        grid=(nb, n_rows // tm),
        in_specs=[tok, _mod_spec(nb, n_lat // tm, d),
                  resident(w1.shape), resident(w3.shape), resident(w2.shape),
                  _full((1, d)), _full((1, d))],
        out_specs=tok,
        out_shape=jax.ShapeDtypeStruct((nb, n_rows, d), F32),
        compiler_params=_params("parallel", "parallel"),
        name="ffn",
    )(h, mod, w1, w3, w2, ln_g, ln_b)


def _proj_odd_kernel(h_ref, mod_ref, w_ref, xm_ref, z_ref):
    inner = xm_ref.shape[-1]
    rg = h_ref.shape[1] // FFN_ROW_GROUPS
    for r in range(FFN_ROW_GROUPS):
        rows = slice(r * rg, (r + 1) * rg)
        x = h_ref[0, rows, :]
        u = (_ln(x) * (1.0 + mod_ref[0, 1:2, :]) + mod_ref[0, 0:1, :]).astype(BF16)
        p = _dot(u, w_ref[...])
        xm_ref[0, rows, :] = p[:, :inner]
        z_ref[0, rows, :] = p[:, inner:]


def _proj_odd(hcat, mod, w, n_lat):
    nb, ntot, d = hcat.shape
    tm = TOKEN_TILE
    inner = w.shape[1] // 2
    out_spec = pl.BlockSpec((1, tm, inner), lambda b, i: (b, i, 0))
    out_shape = jax.ShapeDtypeStruct((nb, ntot, inner), F32)
    return pl.pallas_call(
        _proj_odd_kernel,
        grid=(nb, ntot // tm),
        in_specs=[pl.BlockSpec((1, tm, d), lambda b, i: (b, i, 0)),
                  _mod_spec(nb, n_lat // tm, d),
                  _full(w.shape)],
        out_specs=[out_spec, out_spec],
        out_shape=[out_shape, out_shape],
        compiler_params=_params("parallel", "parallel"),
        name="proj_odd",
    )(hcat, mod, w)


def _gate_fold_kernel(wg_ref, wq_ref, wk_ref, wv_ref, gc_ref, gm_ref):
    dh = wq_ref.shape[-1]
    wg = wg_ref[0]
    gc_ref[0] = (_dot_nt(wg[:, :dh], wq_ref[0]) + _dot_nt(wg[:, dh:2 * dh], wk_ref[0])).astype(BF16)
    gm_ref[0] = _dot_nt(wg[:, 2 * dh:], wv_ref[0]).astype(BF16)


def _gate_fold(wg, wq, wk, wv):
    nh, n_gates, _ = wg.shape
    dh = wq.shape[-1]
    wblk = pl.BlockSpec((1, dh, dh), lambda h: (h, 0, 0))
    gspec = pl.BlockSpec((1, n_gates, dh), lambda h: (h, 0, 0))
    gshape = jax.ShapeDtypeStruct((nh, n_gates, dh), BF16)
    return pl.pallas_call(
        _gate_fold_kernel,
        grid=(nh,),
        in_specs=[pl.BlockSpec((1, n_gates, 3 * dh), lambda h: (h, 0, 0)), wblk, wblk, wblk],
        out_specs=[gspec, gspec],
        out_shape=[gshape, gshape],
        compiler_params=_params("parallel"),
        name="mlstm_gate_fold",
    )(wg, wq, wk, wv)


def _mfeat_kernel(xm_ref, cw_ref, cb_ref, wk_ref, wqt_ref, wvt_ref,
                  xc_ref, k_ref, qt_ref, vt_ref, gr_ref, *, n_lat, chunk, row_group):
    ntot = xm_ref.shape[1]
    dh = wk_ref.shape[-1]
    ct = CONV_TILE
    half = ML_CONV_W // 2
    cw = cw_ref[...]
    cb = cb_ref[...]

    def conv_tile(t, carry):
        r0 = pl.multiple_of(t * ct, ct)
        at_start = jnp.logical_or(r0 == 0, r0 == n_lat)
        at_end = jnp.logical_or(r0 + ct == n_lat, r0 + ct == ntot)
        p0 = pl.multiple_of(jnp.maximum(r0 - SUBLANES, 0), SUBLANES)
        n0 = pl.multiple_of(jnp.minimum(r0 + ct, ntot - SUBLANES), SUBLANES)
        prev = xm_ref[0, pl.ds(p0, SUBLANES), :]
        nxt = xm_ref[0, pl.ds(n0, SUBLANES), :]
        prev = jnp.where(at_start, 0.0, prev)
        nxt = jnp.where(at_end, 0.0, nxt)
        xe = jnp.concatenate([prev, xm_ref[0, pl.ds(r0, ct), :], nxt], axis=0)
        acc = cb
        for dd in range(ML_CONV_W):
            lo = SUBLANES - half + dd
            acc = acc + cw[dd:dd + 1, :] * xe[lo:lo + ct, :]
        xc_ref[0, pl.ds(r0, ct), :] = _silu(acc)
        return carry

    lax.fori_loop(0, ntot // ct, conv_tile, 0)

    for g in range(ntot // row_group):
        rows = pl.ds(g * row_group, row_group)
        k_ref[0, rows, :] = _dot(xc_ref[0, rows, :].astype(BF16), wk_ref[0]).astype(BF16)

    def t_chunk(c, carry):
        r0 = pl.multiple_of(c * chunk, chunk)
        tq = _dot_nt(wqt_ref[0], xc_ref[0, pl.ds(r0, chunk), :].astype(BF16))
        tv = _dot_nt(wvt_ref[0], xm_ref[0, pl.ds(r0, chunk), :].astype(BF16))
        qt_ref[0, 0, c] = tq[:dh].astype(BF16)
        vt_ref[0, 0, c] = tv[:dh].astype(BF16)
        gr_ref[0, 0, c] = tq[dh:] + tv[dh:]
        return carry

    lax.fori_loop(0, ntot // chunk, t_chunk, 0, unroll=2)


def _mfeat(xm, conv_w, conv_b, wk, wqt_g, wvt_g, n_lat, chunk):
    nb, ntot, inner = xm.shape
    nh = ML_HEADS
    dh = inner // nh
    nct = ntot // chunk
    n_gates = wqt_g.shape[1] - dh
    ng = 2 if ntot % (2 * SUBLANES) == 0 else 1
    kern = functools.partial(_mfeat_kernel, n_lat=n_lat, chunk=chunk, row_group=ntot // ng)
    seq = pl.BlockSpec((1, ntot, dh), lambda b, h: (b, 0, h))
    wstack = pl.BlockSpec((1, dh + n_gates, dh), lambda b, h: (h, 0, 0))
    tspec = pl.BlockSpec((1, 1, nct, dh, chunk), lambda b, h: (b, h, 0, 0, 0))
    tshape = jax.ShapeDtypeStruct((nb, nh, nct, dh, chunk), BF16)
    return pl.pallas_call(
        kern,
        grid=(nb, nh),
        in_specs=[seq,
                  pl.BlockSpec((ML_CONV_W, dh), lambda b, h: (0, h)),
                  pl.BlockSpec((1, dh), lambda b, h: (0, h)),
                  pl.BlockSpec((1, dh, dh), lambda b, h: (h, 0, 0)), wstack, wstack],
        out_specs=[seq, seq, tspec, tspec,
                   pl.BlockSpec((1, 1, nct, n_gates, chunk), lambda b, h: (b, h, 0, 0, 0))],
        out_shape=[jax.ShapeDtypeStruct((nb, ntot, inner), F32),
                   jax.ShapeDtypeStruct((nb, ntot, inner), BF16), tshape, tshape,
                   jax.ShapeDtypeStruct((nb, nh, nct, n_gates, chunk), F32)],
        compiler_params=_params("parallel", "parallel"),
        name="mlstm_features",
    )(xm, conv_w, conv_b, wk, wqt_g, wvt_g)


def _split3(x):
    hi = x.astype(BF16)
    r1 = x - hi.astype(F32)
    mid = r1.astype(BF16)
    lo = (r1 - mid.astype(F32)).astype(BF16)
    return hi, mid, lo


def _running_max(x, reverse):
    n = x.shape[1]
    lane = lax.broadcasted_iota(jnp.int32, x.shape, 1)
    k = 1
    while k < n:
        if reverse:
            shifted = jnp.where(lane < n - k, pltpu.roll(x, n - k, 1), -jnp.inf)
        else:
            shifted = jnp.where(lane >= k, pltpu.roll(x, k, 1), -jnp.inf)
        x = jnp.maximum(x, shifted)
        k *= 2
    return x


def _gates_kernel(g_ref, b_ref, rows_ref, cols_ref):
    nh = ML_HEADS
    nct, chunk = g_ref.shape[2], g_ref.shape[-1]
    bias = b_ref[...]
    li_f, li_b, lf_f, lf_b = [], [], [], []
    for c in range(nct):
        pre = jnp.sum(g_ref[0, :, c], axis=0) + bias
        xf = pre[2 * nh:]
        log_f = jnp.minimum(xf, 0.0) - jnp.log1p(jnp.exp(-jnp.abs(xf)))
        li_f.append(pre[:nh] * LOG2_E)
        li_b.append(pre[nh:2 * nh] * LOG2_E)
        lf_f.append(log_f[:nh] * LOG2_E)
        lf_b.append(log_f[nh:] * LOG2_E)
    s_idx = lax.broadcasted_iota(jnp.int32, (chunk, chunk), 0)
    t_idx = lax.broadcasted_iota(jnp.int32, (chunk, chunk), 1)
    tri_f = jnp.where(s_idx <= t_idx, 1.0, 0.0).astype(BF16)
    tri_b = jnp.where(s_idx >= t_idx, 1.0, 0.0).astype(BF16)
    cum_f = sum(_dot(p, tri_f) for p in _split3(jnp.concatenate(lf_f, axis=0)))
    cum_b = sum(_dot(p, tri_b) for p in _split3(jnp.concatenate(lf_b, axis=0)))
    gap_f = jnp.concatenate(li_f, axis=0) - cum_f
    gap_b = jnp.concatenate(li_b, axis=0) - cum_b
    top_f = _running_max(gap_f, False)
    top_b = _running_max(gap_b, True)
    pad = jnp.zeros((LANES - 2 * nh, chunk), F32)
    for c in range(nct):
        r = slice(c * nh, (c + 1) * nh)
        rows_ref[0, c] = jnp.concatenate([gap_f[r], gap_b[r], top_f[r], top_b[r], cum_f[r], cum_b[r]], axis=0)
        gap_t = jnp.transpose(jnp.concatenate([gap_f[r], gap_b[r], pad], axis=0))
        cols_ref[0, c * chunk:(c + 1) * chunk, :] = gap_t[:, :2 * nh]


def _gates(g_part, bias, chunk):
    nb, nh, nct, n_gates, _ = g_part.shape
    ntot = nct * chunk
    return pl.pallas_call(
        _gates_kernel,
        grid=(nb,),
        in_specs=[pl.BlockSpec((1, nh, nct, n_gates, chunk), lambda b: (b, 0, 0, 0, 0)),
                  _full(bias.shape)],
        out_specs=[pl.BlockSpec((1, nct, 6 * nh, chunk), lambda b: (b, 0, 0, 0)),
                   pl.BlockSpec((1, ntot, 2 * nh), lambda b: (b, 0, 0))],
        out_shape=[jax.ShapeDtypeStruct((nb, nct, 6 * nh, chunk), F32),
                   jax.ShapeDtypeStruct((nb, ntot, 2 * nh), F32)],
        compiler_params=_params("parallel"),
        name="mlstm_gates",
    )(g_part, bias)


STATE_PAD_ROWS = 16


def _scan_kernel(k_ref, qt_ref, vt_ref, rows_ref, cols_ref, hs_ref, c_sc, n_sc, m_sc,
                 *, n_lat_chunks, n_ctx_chunks, chunk):
    nh = ML_HEADS
    h = pl.program_id(1)
    dh = c_sc.shape[-1]
    c_sc[...] = jnp.zeros(c_sc.shape, F32)
    n_sc[...] = jnp.zeros(n_sc.shape, F32)
    m_sc[...] = jnp.zeros(m_sc.shape, F32)
    s_idx = lax.broadcasted_iota(jnp.int32, (chunk, chunk), 0)
    t_idx = lax.broadcasted_iota(jnp.int32, (chunk, chunk), 1)
    gate_lane = lax.broadcasted_iota(jnp.int32, (1, 2 * nh), 1)

    def step(d, cidx, with_out):
        off = pl.multiple_of(cidx * chunk, chunk)
        kc = k_ref[0, pl.ds(off, chunk), :]
        qt = qt_ref[0, 0, cidx]
        vt = vt_ref[0, 0, cidx]
        gap_r = rows_ref[0, cidx, pl.ds(d * nh + h, 1), :]
        top_r = rows_ref[0, cidx, pl.ds((2 + d) * nh + h, 1), :]
        cum_r = rows_ref[0, cidx, pl.ds((4 + d) * nh + h, 1), :]
        m = m_sc[d]
        ct = c_sc[d]
        nrow = n_sc[d]
        last = chunk - 1 if d == 0 else 0
        b_end = cum_r[:, last:last + 1]
        top_end = top_r[:, last:last + 1]
        if with_out is not None:
            ctile = cols_ref[0, pl.ds(off, chunk), :]
            gap_c = jnp.sum(jnp.where(gate_lane == d * nh + h, ctile, 0.0), axis=1, keepdims=True)
            lift = jnp.maximum(top_r, m)
            ordered = (s_idx <= t_idx) if d == 0 else (s_idx >= t_idx)
            dw_t = jnp.exp2(jnp.where(ordered, gap_c - lift, -jnp.inf))
            iw = jnp.exp2(m - lift)
            nb16 = jnp.broadcast_to(nrow.astype(BF16), (STATE_PAD_ROWS, dh))
            a = _dot(jnp.concatenate([kc, ct.astype(BF16), nb16], axis=0), qt)
            s_t = a[:chunk] * dw_t
            num_t = iw * a[chunk:chunk + dh] + _dot(vt, s_t.astype(BF16))
            den = iw * a[chunk + dh:chunk + dh + 1] + jnp.sum(s_t, axis=0, keepdims=True)
            floor = jnp.exp2(-(cum_r + lift))
            h_t = num_t * (1.0 / jnp.maximum(jnp.abs(den), floor))
            hs_ref[0, 0, cidx] = h_t if with_out == "assign" else hs_ref[0, 0, cidx] + h_t
        mx = jnp.maximum(m, top_end)
        keep = jnp.exp2(m - mx)
        w_r = jnp.exp2(gap_r - mx)
        vw = (vt.astype(F32) * w_r).astype(BF16)
        wb16 = jnp.broadcast_to(w_r.astype(BF16), (STATE_PAD_ROWS, chunk))
        upd = _dot(jnp.concatenate([vw, wb16], axis=0), kc)
        c_sc[d] = keep * ct + upd[:dh]
        n_sc[d] = keep * nrow + upd[dh:dh + 1]
        m_sc[d] = b_end + mx

    def ctx_body(c, carry):
        step(0, n_lat_chunks + c, None)
        step(1, n_lat_chunks + n_ctx_chunks - 1 - c, None)
        return carry

    def lat_body(mode, c, carry):
        step(0, c, mode)
        step(1, n_lat_chunks - 1 - c, mode)
        return carry

    lax.fori_loop(0, n_ctx_chunks, ctx_body, 0)
    half = n_lat_chunks // 2
    lax.fori_loop(0, half, functools.partial(lat_body, "assign"), 0, unroll=8)
    lax.fori_loop(half, n_lat_chunks, functools.partial(lat_body, "add"), 0, unroll=8)


def _scan(k, qt, vt, rows, cols, n_lat, chunk):
    nb, ntot, inner = k.shape
    nh = ML_HEADS
    dh = inner // nh
    nct = ntot // chunk
    ncl = n_lat // chunk
    assert ncl % 2 == 0
    kern = functools.partial(_scan_kernel, n_lat_chunks=ncl, n_ctx_chunks=nct - ncl, chunk=chunk)
    tspec = pl.BlockSpec((1, 1, nct, dh, chunk), lambda b, h: (b, h, 0, 0, 0))
    return pl.pallas_call(
        kern,
        grid=(nb, nh),
        in_specs=[pl.BlockSpec((1, ntot, dh), lambda b, h: (b, 0, h)), tspec, tspec,
                  pl.BlockSpec((1, nct, rows.shape[2], chunk), lambda b, h: (b, 0, 0, 0)),
                  pl.BlockSpec((1, ntot, cols.shape[2]), lambda b, h: (b, 0, 0))],
        out_specs=pl.BlockSpec((1, 1, ncl, dh, chunk), lambda b, h: (b, h, 0, 0, 0)),
        out_shape=jax.ShapeDtypeStruct((nb, nh, ncl, dh, chunk), F32),
        scratch_shapes=[pltpu.VMEM((2, dh, dh), F32), pltpu.VMEM((2, 1, dh), F32),
                        pltpu.VMEM((2, 1, 1), F32)],
        compiler_params=_params("parallel", "parallel"),
        name="mlstm_scan",
    )(k, qt, vt, rows, cols)


def _out_odd_kernel(hs_ref, xc_ref, z_ref, h_ref, mod_ref, skip_ref, hg_ref, w_ref, g_ref, b_ref, o_ref):
    parts = []
    for hd in range(ML_HEADS):
        ht = hs_ref[0, hd, 0]
        mu = jnp.mean(ht, axis=0, keepdims=True)
        hc = ht - mu
        var = jnp.mean(hc * hc, axis=0, keepdims=True)
        parts.append(jnp.transpose(hc * lax.rsqrt(var + LN_EPS)))
    hn = jnp.concatenate(parts, axis=1)
    y = (hn * hg_ref[...] + skip_ref[...] * xc_ref[0]) * _silu(z_ref[0])
    yo = _dot(y.astype(BF16), w_ref[...])
    o_ref[0] = _post_norm(h_ref[0], mod_ref[0, 2:3, :] * yo, g_ref[...], b_ref[...])


def _out_odd(hs, xc, z, hcat, mod, skip, head_g, w_out, ln_g, ln_b, n_lat):
    nb, _, d = hcat.shape
    inner = xc.shape[-1]
    _, nh, _, dh, tm = hs.shape
    tok = lambda c: pl.BlockSpec((1, tm, c), lambda b, i: (b, i, 0))
    return pl.pallas_call(
        _out_odd_kernel,
        grid=(nb, n_lat // tm),
        in_specs=[pl.BlockSpec((1, nh, 1, dh, tm), lambda b, i: (b, 0, i, 0, 0)),
                  tok(inner), tok(inner), tok(d),
                  pl.BlockSpec((1, 6, d), lambda b, i: (b, 0, 0)),
                  _full((1, inner)), _full((1, inner)), _full(w_out.shape),
                  _full((1, d)), _full((1, d))],
        out_specs=tok(d),
        out_shape=jax.ShapeDtypeStruct((nb, n_lat, d), F32),
        compiler_params=_params("parallel", "parallel"),
        name="out_odd",
    )(hs, xc, z, hcat, mod, skip, head_g, w_out, ln_g, ln_b)


def _even_in_columns():
    cols, scale = [], []
    quarter = DA_DK // 2
    for hd in range(DA_HEADS):
        base = hd * DA_HEAD_COLS
        for blk in range(2):
            b0 = base + blk * 2 * DA_DK
            for half in range(2):
                for m in range(2):
                    cols += [b0 + m * DA_DK + half * quarter + j for j in range(quarter)]
            scale += [DA_DK ** -0.5 * math.log2(math.e) if blk == 0 else 1.0] * (2 * DA_DK)
        cols += list(range(base + 4 * DA_DK, base + DA_HEAD_COLS))
        scale += [1.0] * DA_DV
    cols += list(range(DA_HEADS * DA_HEAD_COLS, DA_HEADS * DA_HEAD_COLS + FN_COLS))
    scale += [1.0] * FN_COLS
    return np.asarray(cols, np.int32), np.asarray(scale, np.float32)


def _rope_tables(n_lat, ntot):
    rows = n_lat // GRID_W
    row = jnp.repeat(jnp.arange(rows, dtype=F32), GRID_W)
    col = jnp.tile(jnp.arange(GRID_W, dtype=F32), rows)
    n_freq = DA_DK // 4
    inv_freq = ROPE_BASE ** (-jnp.arange(n_freq, dtype=F32) / n_freq)
    ang = jnp.concatenate([row[:, None] * inv_freq, col[:, None] * inv_freq], -1)
    cos, sin = jnp.cos(ang), jnp.sin(ang)
    c = jnp.concatenate([cos, cos, cos, cos], axis=1)
    s = jnp.concatenate([-sin, -sin, sin, sin], axis=1)
    n_ctx = ntot - n_lat
    c = jnp.concatenate([c, jnp.ones((n_ctx, LANES), F32)], axis=0)
    s = jnp.concatenate([s, jnp.zeros((n_ctx, LANES), F32)], axis=0)
    return c, s


def _block_diag_heads(w, dh):
    nblk = w.shape[0]
    per_head = dh // ML_BLOCK
    wh = w.reshape(nblk // per_head, per_head, ML_BLOCK, ML_BLOCK)
    eye = jnp.eye(per_head, dtype=w.dtype)
    dense = jnp.einsum('hgij,gk->hgikj', wh, eye)
    return dense.reshape(nblk // per_head, dh, dh)


def kernel(x, c, ctx, c_ctx, w_mod, b_mod, ln_g, ln_b, w_ff1, w_ff3, w_ff2, a_w_in, a_w_out, da_lq1, da_lk1, da_lq2, da_lk2, da_head_g, m_w_in, m_w_out, m_conv_w, m_conv_b, m_wq, m_wk, m_wv, m_w_ig, m_b_ig, m_w_fg, m_b_fg, m_skip, m_head_g):
    nb, n_lat, d = x.shape
    n_ctx = ctx.shape[1]
    ntot = n_lat + n_ctx
    assert w_mod.shape[0] == DEPTH == 2
    assert n_lat % TOKEN_TILE == 0 and n_ctx % TOKEN_TILE == 0 and n_lat % n_ctx == 0 and n_lat % GRID_W == 0

    r = -(-(nb + 1) // MOD_ROWS_PAD) * MOD_ROWS_PAD
    cvec = jnp.concatenate([c, c_ctx[None, :], jnp.zeros((r - nb - 1, d), F32)], axis=0)
    mod_all = _modulation(cvec, w_mod, b_mod).reshape(DEPTH, r, 6, d)

    lam_init0 = 0.8 - 0.6 * math.exp(-0.3 * 0)
    cols, colscale = _even_in_columns()
    w_in = (a_w_in[0][:, cols] * colscale[None, :]).astype(BF16)
    rope_c, rope_s = _rope_tables(n_lat, ntot)
    q, k, v, f = _proj_even(x, ctx, mod_all[0], w_in, rope_c, rope_s)
    lam_params = jnp.stack([da_lq1[0], da_lk1[0], da_lq2[0], da_lk2[0]], axis=0)
    a_n = _attention(q, k, v, lam_params, da_head_g[0][None, :], n_lat, lam_init0)
    fm = _fourier_mix(f, 0, n_lat, None)
    fm = _fourier_mix(f, n_lat, n_ctx, fm)
    h1 = _out_even(a_n, fm, x, ctx, mod_all[0], a_w_out[0].astype(BF16),
                   ln_g[0, 0][None, :], ln_b[0, 0][None, :])
    hcat = _ffn(h1, ntot, mod_all[0], w_ff1[0].astype(BF16), w_ff3[0].astype(BF16), w_ff2[0].astype(BF16),
                ln_g[0, 1][None, :], ln_b[0, 1][None, :], n_lat)

    inner = m_w_in.shape[2] // 2
    dh = inner // ML_HEADS
    chunk = MXU_DIM if (n_lat % MXU_DIM == 0 and n_ctx % MXU_DIM == 0) else LANES
    xm, z = _proj_odd(hcat, mod_all[1], m_w_in[0].astype(BF16), n_lat)
    wq_f = _block_diag_heads(m_wq[0], dh)
    wk_f = _block_diag_heads(m_wk[0], dh) * (dh ** -0.5)
    wv_f = _block_diag_heads(m_wv[0], dh)
    wq, wk, wv = wq_f.astype(BF16), wk_f.astype(BF16), wv_f.astype(BF16)
    wg_all = jnp.concatenate([m_w_ig[0, 0], m_w_ig[0, 1], m_w_fg[0, 0], m_w_fg[0, 1]], axis=1)
    wg_all = wg_all.reshape(3, ML_HEADS, dh, 4 * ML_HEADS) * jnp.asarray([1.0, dh ** 0.5, 1.0], F32)[:, None, None, None]
    wg = jnp.transpose(wg_all, (1, 3, 0, 2)).reshape(ML_HEADS, 4 * ML_HEADS, 3 * dh).astype(BF16)
    g_bias = jnp.concatenate([m_b_ig[0, 0], m_b_ig[0, 1], m_b_fg[0, 0], m_b_fg[0, 1]])[:, None]
    gc, gm = _gate_fold(wg, wq, wk, wv)
    wqt_g = jnp.concatenate([jnp.swapaxes(wq_f, 1, 2).astype(BF16), gc], axis=1)
    wvt_g = jnp.concatenate([jnp.swapaxes(wv_f, 1, 2).astype(BF16), gm], axis=1)
    xc, km, qt, vt, g_part = _mfeat(xm, m_conv_w[0], m_conv_b[0][None, :], wk, wqt_g, wvt_g, n_lat, chunk)
    rows, colsg = _gates(g_part, g_bias, chunk)
    hs = _scan(km, qt, vt, rows, colsg, n_lat, chunk)
    h1 = _out_odd(hs, xc, z, hcat, mod_all[1], m_skip[0][None, :], m_head_g[0][None, :],
                  m_w_out[0].astype(BF16), ln_g[1, 0][None, :], ln_b[1, 0][None, :], n_lat)
    return _ffn(h1, n_lat, mod_all[1], w_ff1[1].astype(BF16), w_ff3[1].astype(BF16), w_ff2[1].astype(BF16),
                ln_g[1, 1][None, :], ln_b[1, 1][None, :], n_lat)
```
